```python
import math
import jax, jax.numpy as jnp
from jax import lax
import numpy as np

D_MODEL = 1024
BATCH = 16
SEQ = 256
DEPTH = 2
DEC_BATCH = 4
DEC_SEQ = 1024
PAST_LEN = 512

GRID_W = 64
HGRN_HEADS = 8
HGRN_DK = 128
HGRN_DV = D_MODEL // HGRN_HEADS
HGRN_KW = HGRN_HEADS * HGRN_DK
HGRN_W = HGRN_HEADS * HGRN_DV
HGRN_CHUNK = 16
SSD_W = D_MODEL
SSD_HEADDIM = 64
SSD_HEADS = SSD_W // SSD_HEADDIM
SSD_GROUPS = 4
SSD_STATE = 128
SSD_CHUNK = 64
MIX_W = HGRN_W + SSD_W
CONV_CH = SSD_W + 2 * SSD_GROUPS * SSD_STATE
D_FF = 2816
IN_COLS = 3 * HGRN_KW + 2 * HGRN_W + SSD_W + CONV_CH + 2 * SSD_HEADS
SPLITS = (HGRN_KW, 2 * HGRN_KW, 3 * HGRN_KW, 3 * HGRN_KW + HGRN_W, 3 * HGRN_KW + 2 * HGRN_W,
          3 * HGRN_KW + 2 * HGRN_W + SSD_W, 3 * HGRN_KW + 2 * HGRN_W + SSD_W + CONV_CH)
EPS = 1e-6

kernel_name = 'hybrid_hgrn2_ssd_diffusion_step'


def _rmsnorm(x, w):
    xf = x.astype(jnp.float32)
    y = xf * lax.rsqrt(jnp.mean(xf * xf, axis=-1, keepdims=True) + EPS)
    return (y * w.astype(jnp.float32)).astype(x.dtype)


def _dwconv_grid(x, w, b, rows):
    bsz, l, ch = x.shape
    img = x.reshape(bsz, rows, l // rows, ch)
    out = lax.conv_general_dilated(img, w.astype(x.dtype), (1, 1), 'SAME',
                                   dimension_numbers=('NHWC', 'HWIO', 'NHWC'),
                                   feature_group_count=ch)
    return out.reshape(bsz, l, ch) + b.astype(x.dtype)


def _chunk_state_scan(decay, u, s0):
    def step(s, inp):
        d, uc = inp
        return d * s + uc, s
    s_fin, s_prev = lax.scan(step, s0, (jnp.moveaxis(decay, 1, 0), jnp.moveaxis(u, 1, 0)))
    return s_fin, jnp.moveaxis(s_prev, 0, 1)


def _gla_chunked(q, k, v, log_f, s0):
    f32 = jnp.float32
    bsz, l, h, dk = q.shape
    dv = v.shape[-1]
    c = HGRN_CHUNK
    nc = l // c
    q, k, log_f = [t.astype(f32).reshape(bsz, nc, c, h, dk) for t in (q, k, log_f)]
    v = v.astype(f32).reshape(bsz, nc, c, h, dv)
    cum = jnp.cumsum(log_f, axis=2)
    mask = jnp.tril(jnp.ones((c, c), dtype=bool))[:, :, None, None]
    decay = jnp.exp(jnp.where(mask, cum[:, :, :, None] - cum[:, :, None], -jnp.inf))
    scores = jnp.einsum('bcthk,bctshk,bcshk->bcths', q, decay, k)
    o_intra = jnp.einsum('bcths,bcshv->bcthv', scores, v)
    last = cum[:, :, -1]
    u = jnp.einsum('bcshk,bcshv->bchkv', k * jnp.exp(last[:, :, None] - cum), v)
    s_fin, s_prev = _chunk_state_scan(jnp.exp(last)[..., None], u, s0.astype(f32))
    o_inter = jnp.einsum('bcthk,bchkv->bcthv', q * jnp.exp(cum), s_prev)
    return (o_intra + o_inter).reshape(bsz, l, h, dv), s_fin


def _ssd_chunked(x, dt, a, bm, cm, s0):
    f32 = jnp.float32
    bsz, l, h, p = x.shape
    g, n = bm.shape[2], bm.shape[3]
    r = h // g
    q = SSD_CHUNK
    nc = l // q
    x = x.astype(f32).reshape(bsz, nc, q, g, r, p)
    dt = dt.astype(f32).reshape(bsz, nc, q, g, r)
    bm = bm.astype(f32).reshape(bsz, nc, q, g, n)
    cm = cm.astype(f32).reshape(bsz, nc, q, g, n)
    cum = jnp.cumsum(dt * a.reshape(g, r), axis=2)
    mask = jnp.tril(jnp.ones((q, q), dtype=bool))[:, :, None, None]
    decay = jnp.exp(jnp.where(mask, cum[:, :, :, None] - cum[:, :, None], -jnp.inf))
    cb = jnp.einsum('bctgn,bcsgn->bctsg', cm, bm)
    y_intra = jnp.einsum('bctsg,bctsgr,bcsgr,bcsgrp->bctgrp', cb, decay, dt, x)
    last = cum[:, :, -1]
    w_end = jnp.exp(last[:, :, None] - cum) * dt
    u = jnp.einsum('bcsgr,bcsgn,bcsgrp->bcgrpn', w_end, bm, x)
    s_fin, s_prev = _chunk_state_scan(jnp.exp(last)[..., None, None], u,
                                      s0.astype(f32).reshape(bsz, g, r, p, n))
    y_inter = jnp.einsum('bctgn,bctgr,bcgrpn->bctgrp', cm, jnp.exp(cum), s_prev)
    return (y_intra + y_inter).reshape(bsz, l, h, p), s_fin.reshape(bsz, h, p, n)


def _hgrn_mixer(q, f_fw, f_bw, i, g, lb, norm_w, s0):
    f32 = jnp.float32
    bsz, l, _ = q.shape
    heads = lambda t, d: t.reshape(bsz, l, HGRN_HEADS, d)
    qh = heads(jax.nn.silu(q.astype(f32)), HGRN_DK) * (HGRN_DK ** -0.5)
    vh = heads(i, HGRN_DV)
    outs, finals = [], []
    for d, f_pre in enumerate((f_fw, f_bw)):
        f_pre = f_pre.astype(f32)
        lb_d = lb[d]
        log_f = jnp.logaddexp(jnp.log(lb_d), jnp.log1p(-lb_d) + jax.nn.log_sigmoid(f_pre))
        k = (1.0 - lb_d) * jax.nn.sigmoid(-f_pre)
        args = (qh, heads(k, HGRN_DK), vh, heads(log_f, HGRN_DK))
        if d == 1:
            args = tuple(jnp.flip(t, axis=1) for t in args)
        o, s_fin = _gla_chunked(args[0], args[1], args[2], args[3], s0[:, d])
        if d == 1:
            o = jnp.flip(o, axis=1)
        outs.append(o)
        finals.append(s_fin)
    o = _rmsnorm(outs[0] + outs[1], norm_w) * jax.nn.silu(heads(g, HGRN_DV).astype(f32))
    return o.reshape(bsz, l, HGRN_W).astype(q.dtype), jnp.stack(finals, axis=1)


def _ssd_mixer(z, xbc, dt_raw, conv_w, conv_b, a_log, dt_bias, d_skip, norm_w, s0, rows):
    f32 = jnp.float32
    bsz, l, _ = z.shape
    xbc = jax.nn.silu(_dwconv_grid(xbc, conv_w, conv_b, rows))
    xs, bm, cm = jnp.split(xbc, [SSD_W, SSD_W + SSD_GROUPS * SSD_STATE], axis=-1)
    xs = xs.reshape(bsz, l, SSD_HEADS, SSD_HEADDIM)
    bm = bm.reshape(bsz, l, SSD_GROUPS, SSD_STATE)
    cm = cm.reshape(bsz, l, SSD_GROUPS, SSD_STATE)
    dt_raw = dt_raw.reshape(bsz, l, 2, SSD_HEADS).astype(f32)
    ys, finals = [], []
    for d in range(2):
        dt = jax.nn.softplus(dt_raw[:, :, d] + dt_bias[d].astype(f32))
        a = -jnp.exp(a_log[d].astype(f32))
        args = (xs, dt, bm, cm)
        if d == 1:
            args = tuple(jnp.flip(t, axis=1) for t in args)
        y, s_fin = _ssd_chunked(args[0], args[1], a, args[2], args[3], s0[:, d])
        if d == 1:
            y = jnp.flip(y, axis=1)
        ys.append(y)
        finals.append(s_fin)
    y = ys[0] + ys[1] + d_skip.astype(f32)[:, None] * xs.astype(f32)
    y = y.reshape(bsz, l, SSD_W) * jax.nn.silu(z.astype(f32))
    y = _rmsnorm(y.reshape(bsz, l, SSD_GROUPS, SSD_W // SSD_GROUPS),
                 norm_w.reshape(SSD_GROUPS, SSD_W // SSD_GROUPS))
    return y.reshape(bsz, l, SSD_W).astype(z.dtype), jnp.stack(finals, axis=1)


def _conv_ffn(h, w_up, conv_w, conv_b, w_down, rows):
    u = _dwconv_grid(h @ w_up, conv_w, conv_b, rows)
    gate, val = jnp.split(u, 2, axis=-1)
    return (jax.nn.silu(gate) * val) @ w_down


def _layer(x, cond, s_h0, s_s0, rows, lb, p):
    mod = jax.nn.silu(cond) @ p['w_ada'] + p['b_ada']
    sh1, sc1, g1, sh2, sc2, g2 = [m[:, None] for m in jnp.split(mod, 6, axis=-1)]
    h = _rmsnorm(x, p['norm_w'][0]) * (1 + sc1) + sh1
    proj = h @ p['w_in']
    q, f_fw, f_bw, i, g, z, xbc, dt_raw = jnp.split(proj, list(SPLITS), axis=-1)
    o_h, s_h = _hgrn_mixer(q, f_fw, f_bw, i, g, lb, p['hgrn_norm_w'], s_h0)
    o_s, s_s = _ssd_mixer(z, xbc, dt_raw, p['ssd_conv_w'], p['ssd_conv_b'], p['ssd_a_log'],
                          p['ssd_dt_bias'], p['ssd_d'], p['ssd_norm_w'], s_s0, rows)
    x = x + g1 * (jnp.concatenate([o_h, o_s], axis=-1) @ p['w_out'])
    h = _rmsnorm(x, p['norm_w'][1]) * (1 + sc2) + sh2
    x = x + g2 * _conv_ffn(h, p['ffn_up'], p['ffn_conv_w'], p['ffn_conv_b'], p['ffn_down'], rows)
    return x, s_h, s_s


def setup_inputs(seed: int = 0) -> dict:
    key = jax.random.key(seed)
    ks = jax.random.split(key, 24)
    f32 = jnp.float32
    nrm = lambda k, shape, s: jax.random.normal(k, shape, f32) * s
    dt0 = jnp.exp(jax.random.uniform(ks[16], (DEPTH, 2, SSD_HEADS), f32,
                                     minval=math.log(1e-3), maxval=math.log(1e-1)))
    return {
        'x_prompt': nrm(ks[0], (BATCH, SEQ, D_MODEL), 1.0),
        'x_sample': nrm(ks[1], (DEC_BATCH, DEC_SEQ, D_MODEL), 1.0),
        'c': nrm(ks[2], (DEC_BATCH, D_MODEL), 1.0),
        'state_hgrn': nrm(ks[3], (DEC_BATCH, DEPTH, 2, HGRN_HEADS, HGRN_DK, HGRN_DV), 0.3),
        'state_ssd': nrm(ks[4], (DEC_BATCH, DEPTH, 2, SSD_HEADS, SSD_HEADDIM, SSD_STATE), 0.3),
        'c_ctx': nrm(ks[5], (D_MODEL,), 1.0),
        'norm_w': 1.0 + nrm(ks[6], (DEPTH, 2, D_MODEL), 0.02),
        'final_norm_w': 1.0 + nrm(ks[7], (D_MODEL,), 0.02),
        'w_ada': nrm(ks[8], (DEPTH, D_MODEL, 6 * D_MODEL), 0.5 * D_MODEL ** -0.5),
        'b_ada': nrm(ks[9], (DEPTH, 6 * D_MODEL), 0.02),
        'w_in': nrm(ks[10], (DEPTH, D_MODEL, IN_COLS), D_MODEL ** -0.5),
        'w_out': nrm(ks[11], (DEPTH, MIX_W, D_MODEL), MIX_W ** -0.5),
        'hgrn_lb': nrm(ks[12], (DEPTH, 2, HGRN_KW), 0.1),
        'hgrn_norm_w': 1.0 + nrm(ks[13], (DEPTH, HGRN_DV), 0.02),
        'ssd_conv_w': nrm(ks[14], (DEPTH, 3, 3, 1, CONV_CH), 1.0 / 3.0),
        'ssd_conv_b': nrm(ks[15], (DEPTH, CONV_CH), 0.02),
        'ssd_a_log': jnp.log(jax.random.uniform(ks[17], (DEPTH, 2, SSD_HEADS), f32, minval=1.0, maxval=16.0)),
        'ssd_dt_bias': dt0 + jnp.log(-jnp.expm1(-dt0)),
        'ssd_d': 1.0 + nrm(ks[18], (DEPTH, SSD_HEADS), 0.02),
        'ssd_norm_w': 1.0 + nrm(ks[19], (DEPTH, SSD_W), 0.02),
        'ffn_up': nrm(ks[20], (DEPTH, D_MODEL, 2 * D_FF), D_MODEL ** -0.5),
        'ffn_conv_w': nrm(ks[21], (DEPTH, 3, 3, 1, 2 * D_FF), 1.0 / 3.0),
        'ffn_conv_b': nrm(ks[22], (DEPTH, 2 * D_FF), 0.02),
        'ffn_down': nrm(ks[23], (DEPTH, D_FF, D_MODEL), D_FF ** -0.5),
    }


def reference(x_prompt, x_sample, c, state_hgrn, state_ssd, c_ctx, norm_w, final_norm_w,
              w_ada, b_ada, w_in, w_out, hgrn_lb, hgrn_norm_w, ssd_conv_w, ssd_conv_b,
              ssd_a_log, ssd_dt_bias, ssd_d, ssd_norm_w, ffn_up, ffn_conv_w, ffn_conv_b,
              ffn_down):
    f32 = jnp.float32
    lb_all = jnp.cumsum(jax.nn.softmax(hgrn_lb.astype(f32), axis=0), axis=0)
    lb_all = lb_all - lb_all[:1]
    layers = [dict(norm_w=norm_w[l], w_ada=w_ada[l], b_ada=b_ada[l], w_in=w_in[l],
                   w_out=w_out[l], hgrn_norm_w=hgrn_norm_w[l], ssd_conv_w=ssd_conv_w[l],
                   ssd_conv_b=ssd_conv_b[l], ssd_a_log=ssd_a_log[l], ssd_dt_bias=ssd_dt_bias[l],
                   ssd_d=ssd_d[l], ssd_norm_w=ssd_norm_w[l], ffn_up=ffn_up[l],
                   ffn_conv_w=ffn_conv_w[l], ffn_conv_b=ffn_conv_b[l], ffn_down=ffn_down[l])
              for l in range(DEPTH)]

    bp = x_prompt.shape[0]
    cond_ctx = jnp.broadcast_to(c_ctx[None], (bp, c_ctx.shape[0]))
    zero_h = jnp.zeros((bp, 2, HGRN_HEADS, HGRN_DK, HGRN_DV), f32)
    zero_s = jnp.zeros((bp, 2, SSD_HEADS, SSD_HEADDIM, SSD_STATE), f32)
    h = x_prompt
    new_h, new_s = [], []
    for l in range(DEPTH):
        h, s_h, s_s = _layer(h, cond_ctx, zero_h, zero_s, 1, lb_all[l], layers[l])
        new_h.append(s_h)
        new_s.append(s_s)
    y_prompt = _rmsnorm(h, final_norm_w)
    new_state_hgrn = jnp.stack(new_h, axis=1)
    new_state_ssd = jnp.stack(new_s, axis=1)

    rows = x_sample.shape[1] // GRID_W
    z = x_sample
    for l in range(DEPTH):
        z, _, _ = _layer(z, c, state_hgrn[:, l], state_ssd[:, l], rows, lb_all[l], layers[l])
    y_sample = _rmsnorm(z, final_norm_w)
    return (y_prompt, y_sample, new_state_hgrn, new_state_ssd)
```

```python
import functools
import math

import numpy as np
import jax
import jax.numpy as jnp
from jax import lax
from jax.experimental import pallas as pl
from jax.experimental.pallas import tpu as pltpu

F32 = jnp.float32
BF16 = jnp.bfloat16

D_MODEL = 1024
DEPTH = 2
GRID_W = 64
HGRN_HEADS = 8
HGRN_DK = 128
HGRN_DV = 128
HGRN_KW = HGRN_HEADS * HGRN_DK
HGRN_W = HGRN_HEADS * HGRN_DV
SSD_W = 1024
SSD_HEADDIM = 64
SSD_HEADS = 16
SSD_GROUPS = 4
SSD_STATE = 128
HEADS_PER_GROUP = SSD_HEADS // SSD_GROUPS
GROUP_W = SSD_W // SSD_GROUPS
MIX_W = HGRN_W + SSD_W
CONV_CH = SSD_W + 2 * SSD_GROUPS * SSD_STATE
D_FF = 2816
MAIN_COLS = 3 * HGRN_KW + 2 * HGRN_W + SSD_W + CONV_CH
DT_COLS = 2 * SSD_HEADS
EPS = 1e-6

LANES = 128
CHUNK = 128
N_LEVELS = 7
VMEM_LIMIT = 56 * 1024 * 1024


def _cparams(sem):
    return pltpu.CompilerParams(dimension_semantics=sem, vmem_limit_bytes=VMEM_LIMIT)


def _sigmoid(x):
    return 1.0 / (1.0 + jnp.exp(-x))


def _silu(x):
    return x * _sigmoid(x)


def _softplus(x):
    return jnp.maximum(x, 0.0) + jnp.log1p(jnp.exp(-jnp.abs(x)))


def _rms(x):
    return x * lax.rsqrt(jnp.mean(x * x, axis=-1, keepdims=True) + EPS)


def _split3(x):
    hi = x.astype(BF16)
    r = x - hi.astype(F32)
    mid = r.astype(BF16)
    lo = (r - mid.astype(F32)).astype(BF16)
    return hi, mid, lo


def _sum_rows_exact(w01, x):
    n = x.shape[1]
    hi, mid, lo = _split3(x)
    p = jnp.dot(w01, jnp.concatenate([hi, mid, lo], axis=1), preferred_element_type=F32)
    return p[:, :n] + p[:, n:2 * n] + p[:, 2 * n:]


def _sum_cols_exact(x, w01):
    m = x.shape[0]
    hi, mid, lo = _split3(x)
    p = jnp.dot(jnp.concatenate([hi, mid, lo], axis=0), w01, preferred_element_type=F32)
    return p[:m] + p[m:2 * m] + p[2 * m:]


def _dot_nt(a, b):
    return lax.dot_general(a, b, (((1,), (1,)), ((), ())), preferred_element_type=F32)


def _dot_tn(a, b):
    return lax.dot_general(a, b, (((0,), (0,)), ((), ())), preferred_element_type=F32)


def _dwconv(x, w_ref, b_ref, width, rows):
    seq, _ = x.shape
    t = lax.broadcasted_iota(jnp.int32, x.shape, 0)
    col = jnp.bitwise_and(t, width - 1)
    xl = jnp.where(col == 0, 0.0, pltpu.roll(x, 1, axis=0))
    xr = jnp.where(col == width - 1, 0.0, pltpu.roll(x, seq - 1, axis=0))

    def hrow(kh, a, b, c):
        return w_ref[3 * kh:3 * kh + 1, :] * a + w_ref[3 * kh + 1:3 * kh + 2, :] * b \
            + w_ref[3 * kh + 2:3 * kh + 3, :] * c

    out = b_ref[...] + hrow(1, xl, x, xr)
    if rows > 1:
        up = hrow(0, pltpu.roll(xl, width, axis=0), pltpu.roll(x, width, axis=0),
                  pltpu.roll(xr, width, axis=0))
        out = out + jnp.where(t >= width, up, 0.0)
        dn = hrow(2, pltpu.roll(xl, seq - width, axis=0), pltpu.roll(x, seq - width, axis=0),
                  pltpu.roll(xr, seq - width, axis=0))
        out = out + jnp.where(t < seq - width, dn, 0.0)
    return out


def _scan_constants():
    c = CHUNK
    t = np.arange(c)[:, None]
    i = np.arange(c)[None, :]
    wf = [(i <= t), (i > t)]
    wb = [(i >= t), (i < t)]
    for lev in range(1, N_LEVELS + 1):
        m = 1 << (lev - 1)
        up = ((t >> (lev - 1)) & 1) == 1
        p = (t // (2 * m)) * (2 * m) + m
        wf.append(np.where(up, (i >= p) & (i <= t), (i > t) & (i < p)))
        wb.append(np.where(up, (i >= p) & (i < t), (i >= t) & (i < p)))
    x = t ^ i
    lv = np.where(x == 0, 0, np.floor(np.log2(np.maximum(x, 1))).astype(np.int64) + 1)
    return (jnp.asarray(np.concatenate(wf, 0), BF16), jnp.asarray(np.concatenate(wb, 0), BF16),
            jnp.asarray(lv, jnp.int32))


def _mod_kernel(cond_ref, w_ref, b_ref, o_ref):
    a = _silu(cond_ref[...]).astype(BF16)
    o_ref[...] = jnp.dot(a, w_ref[...].astype(BF16), preferred_element_type=F32) + b_ref[...]


def _mod_call(cond8, w_ada, b_ada):
    n = 6 * D_MODEL
    tn = D_MODEL
    return pl.pallas_call(
        _mod_kernel,
        grid=(DEPTH, n // tn),
        in_specs=[
            pl.BlockSpec((8, D_MODEL), lambda l, j: (0, 0)),
            pl.BlockSpec((None, D_MODEL, tn), lambda l, j: (l, 0, j)),
            pl.BlockSpec((None, 1, tn), lambda l, j: (l, 0, j)),
        ],
        out_specs=pl.BlockSpec((None, 8, tn), lambda l, j: (l, 0, j)),
        out_shape=jax.ShapeDtypeStruct((DEPTH, 8, n), F32),
        compiler_params=_cparams(("arbitrary", "arbitrary")),
        name="mod",
    )(cond8, w_ada, b_ada.reshape(DEPTH, 1, n))


def _inproj_kernel(x_ref, sh_ref, sc_ref, nw_ref, w_ref, wdt_ref, o_ref, odt_ref, h_scr):
    @pl.when(pl.program_id(1) == 0)
    def _():
        h = _rms(x_ref[...]) * nw_ref[...] * (1.0 + sc_ref[...]) + sh_ref[...]
        hb = h.astype(BF16)
        h_scr[...] = hb
        odt_ref[...] = jnp.dot(hb, wdt_ref[...], preferred_element_type=F32)

    o_ref[...] = jnp.dot(h_scr[...], w_ref[...], preferred_element_type=F32)


def _inproj_call(x2, mod3, nw, w_main, w_dt, seq_len):
    m = x2.shape[0]
    tm, tn = 512, 1024
    bc = mod3.shape[0]
    seq_of = (lambda i: (i * tm) // seq_len) if bc > 1 else (lambda i: 0)
    return pl.pallas_call(
        _inproj_kernel,
        grid=(m // tm, MAIN_COLS // tn),
        in_specs=[
            pl.BlockSpec((tm, D_MODEL), lambda i, j: (i, 0)),
            pl.BlockSpec((None, 1, D_MODEL), lambda i, j: (seq_of(i), 0, 0)),
            pl.BlockSpec((None, 1, D_MODEL), lambda i, j: (seq_of(i), 0, 1)),
            pl.BlockSpec((1, D_MODEL), lambda i, j: (0, 0)),
            pl.BlockSpec((D_MODEL, tn), lambda i, j: (0, j)),
            pl.BlockSpec((D_MODEL, LANES), lambda i, j: (0, 0)),
        ],
        out_specs=[
            pl.BlockSpec((tm, tn), lambda i, j: (i, j)),
            pl.BlockSpec((tm, LANES), lambda i, j: (i, 0)),
        ],
        out_shape=[jax.ShapeDtypeStruct((m, MAIN_COLS), F32),
                   jax.ShapeDtypeStruct((m, LANES), F32)],
        scratch_shapes=[pltpu.VMEM((tm, D_MODEL), BF16)],
        compiler_params=_cparams(("arbitrary", "arbitrary")),
        name="inproj",
    )(x2, mod3, mod3, nw, w_main, w_dt)


def _hgrn_kernel(*refs, layer, seq_len, has_s0, want_state):
    it = iter(refs)
    q_ref, ff_ref, fb_ref, v_ref, g_ref, lb_ref, nw_ref, wf_ref, wb_ref, lv_ref = (
        next(it) for _ in range(10))
    s0_ref = next(it) if has_s0 else None
    o_ref = next(it)
    sf_ref = next(it) if want_state else None
    (q_scr, kf_scr, kb_scr, lff_scr, lfb_scr, acc_scr, qc_scr, u_scr, dl_scr, st_scr) = (
        next(it) for _ in range(10))

    c = CHUNK
    n_chunks = seq_len // c

    lbr = lb_ref[...]
    e = jnp.exp(lbr - jnp.max(lbr, axis=0, keepdims=True))
    sm = e / jnp.sum(e, axis=0, keepdims=True)
    lb = jnp.zeros((2, HGRN_DK), F32)
    for i in range(1, layer + 1):
        lb = lb + sm[i]

    q_scr[...] = _silu(q_ref[...]) * (HGRN_DK ** -0.5)
    for d, (f_ref, k_scr, lf_scr) in enumerate(((ff_ref, kf_scr, lff_scr), (fb_ref, kb_scr, lfb_scr))):
        lbd = lb[d:d + 1, :]
        f_pre = f_ref[...]
        ea = jnp.exp(-jnp.abs(f_pre))
        log_sig = jnp.minimum(f_pre, 0.0) - jnp.log1p(ea)
        sig_neg = jnp.where(f_pre >= 0.0, ea, 1.0) / (1.0 + ea)
        a = jnp.log(lbd)
        b = jnp.log1p(-lbd) + log_sig
        lf_scr[...] = jnp.maximum(a, b) + jnp.log1p(jnp.exp(-jnp.abs(a - b)))
        k_scr[...] = (1.0 - lbd) * sig_neg

    lv = lv_ref[...]
    row = lax.broadcasted_iota(jnp.int32, (c, HGRN_DK), 0)

    def intra(ci, carry):
        r0 = pl.multiple_of(ci * c, c)
        sl = pl.ds(r0, c)
        q = q_scr[sl, :]
        kf = kf_scr[sl, :]
        kb = kb_scr[sl, :]
        vb = v_ref[sl, :].astype(BF16)
        ef = jnp.exp(_sum_rows_exact(wf_ref[...], lff_scr[sl, :]))
        eb = jnp.exp(_sum_rows_exact(wb_ref[...], lfb_scr[sl, :]))
        zero = jnp.zeros_like(q)

        a_sum = jnp.where(lv == 0, _dot_nt(q.astype(BF16), (kf + kb).astype(BF16)), 0.0)
        for lev in range(1, N_LEVELS + 1):
            up = jnp.bitwise_and(jnp.right_shift(row, lev - 1), 1) == 1
            ef_l = ef[(lev + 1) * c:(lev + 2) * c]
            eb_l = eb[(lev + 1) * c:(lev + 2) * c]
            qe = q * jnp.where(up, ef_l, eb_l)
            ke = jnp.where(up, kb * eb_l, kf * ef_l)
            lhs = jnp.concatenate([jnp.where(up, qe, zero), jnp.where(up, zero, qe)], axis=1)
            rhs = jnp.concatenate([jnp.where(up, zero, ke), jnp.where(up, ke, zero)], axis=1)
            a_sum = a_sum + jnp.where(lv == lev, _dot_nt(lhs.astype(BF16), rhs.astype(BF16)), 0.0)
        acc_scr[sl, :] = jnp.dot(a_sum.astype(BF16), vb, preferred_element_type=F32)

        qc_scr[sl, :] = jnp.concatenate([q * ef[0:c], q * eb[0:c]], axis=1).astype(BF16)
        u_scr[ci, 0] = _dot_tn(vb, (kf * ef[c:2 * c]).astype(BF16))
        u_scr[ci, 1] = _dot_tn(vb, (kb * eb[c:2 * c]).astype(BF16))
        dl_scr[ci, 0:1, :] = ef[c - 1:c]
        dl_scr[ci, 1:2, :] = eb[0:1]
        return carry

    lax.fori_loop(0, n_chunks, intra, 0)

    if has_s0:
        st_f = s0_ref[0].T
        st_b = s0_ref[1].T
    else:
        st_f = jnp.zeros((HGRN_DV, HGRN_DK), F32)
        st_b = jnp.zeros((HGRN_DV, HGRN_DK), F32)
    for ci in range(n_chunks):
        st_scr[ci, :, 0:HGRN_DK] = st_f.astype(BF16)
        st_f = st_f * dl_scr[ci, 0:1, :] + u_scr[ci, 0]
    for ci in reversed(range(n_chunks)):
        st_scr[ci, :, HGRN_DK:2 * HGRN_DK] = st_b.astype(BF16)
        st_b = st_b * dl_scr[ci, 1:2, :] + u_scr[ci, 1]
    if want_state:
        sf_ref[0] = st_f.T
        sf_ref[1] = st_b.T

    nw = nw_ref[...]

    def inter(ci, carry):
        r0 = pl.multiple_of(ci * c, c)
        sl = pl.ds(r0, c)
        o = acc_scr[sl, :] + _dot_nt(qc_scr[sl, :], st_scr[ci])
        o_ref[sl, :] = _rms(o) * nw * _silu(g_ref[sl, :])
        return carry

    lax.fori_loop(0, n_chunks, inter, 0)


def _hgrn_call(proj, lb_raw, nw, consts, s0, layer, batch, seq_len, want_state):
    wf, wb, lv = consts
    has_s0 = s0 is not None
    n_chunks = seq_len // CHUNK
    m = proj.shape[0]
    col = lambda base: pl.BlockSpec((seq_len, LANES), lambda b, h, base=base: (b, base + h))
    full = lambda arr: pl.BlockSpec(arr.shape, lambda b, h: (0,) * arr.ndim)
    in_specs = [col(0), col(HGRN_HEADS), col(2 * HGRN_HEADS), col(3 * HGRN_HEADS), col(4 * HGRN_HEADS),
                pl.BlockSpec((DEPTH, 2, LANES), lambda b, h: (0, 0, h)),
                pl.BlockSpec((1, HGRN_DV), lambda b, h: (0, 0)),
                full(wf), full(wb), full(lv)]
    args = [proj, proj, proj, proj, proj, lb_raw, nw, wf, wb, lv]
    if has_s0:
        in_specs.append(pl.BlockSpec((None, 2, None, HGRN_DK, HGRN_DV), lambda b, h: (b, 0, h, 0, 0)))
        args.append(s0)
    out_specs = [pl.BlockSpec((seq_len, LANES), lambda b, h: (b, h))]
    out_shape = [jax.ShapeDtypeStruct((m, HGRN_W), F32)]
    if want_state:
        out_specs.append(pl.BlockSpec((None, 2, None, HGRN_DK, HGRN_DV), lambda b, h: (b, 0, h, 0, 0)))
        out_shape.append(jax.ShapeDtypeStruct((batch, 2, HGRN_HEADS, HGRN_DK, HGRN_DV), F32))
    scratch = [pltpu.VMEM((seq_len, HGRN_DK), F32) for _ in range(6)]
    scratch += [pltpu.VMEM((seq_len, 2 * HGRN_DK), BF16),
                pltpu.VMEM((n_chunks, 2, HGRN_DV, HGRN_DK), F32),
                pltpu.VMEM((n_chunks, 8, HGRN_DK), F32),
                pltpu.VMEM((n_chunks, HGRN_DV, 2 * HGRN_DK), BF16)]
    res = pl.pallas_call(
        functools.partial(_hgrn_kernel, layer=layer, seq_len=seq_len, has_s0=has_s0,
                          want_state=want_state),
        grid=(batch, HGRN_HEADS),
        in_specs=in_specs, out_specs=out_specs, out_shape=out_shape,
        scratch_shapes=scratch,
        compiler_params=_cparams(("arbitrary", "arbitrary")),
        name="hgrn",
    )(*args)
    return (res[0], res[1]) if want_state else (res[0], None)


def _ssd_kernel(*refs, seq_len, width, rows, has_s0, want_state):
    it = iter(refs)
    (z_ref, x_ref, b_ref, c_ref, dt_ref, cwx_ref, cwb_ref, cwc_ref, cbx_ref, cbb_ref, cbc_ref,
     alog_ref, dtb_ref, dsk_ref, nw_ref, tl_ref, tu_ref) = (next(it) for _ in range(17))
    s0_ref = next(it) if has_s0 else None
    o_ref = next(it)
    sf_ref = next(it) if want_state else None
    (xc_scr, bc_scr, cc_scr, dtg_scr, acc_scr, ecum_scr, u_scr, dl_scr, st_scr) = (
        next(it) for _ in range(9))

    q = CHUNK
    n_chunks = seq_len // q
    hp = HEADS_PER_GROUP
    p = SSD_HEADDIM
    g = pl.program_id(1)

    xc_scr[...] = _silu(_dwconv(x_ref[...], cwx_ref, cbx_ref, width, rows))
    bc_scr[...] = _silu(_dwconv(b_ref[...], cwb_ref, cbb_ref, width, rows)).astype(BF16)
    cc_scr[...] = _silu(_dwconv(c_ref[...], cwc_ref, cbc_ref, width, rows)).astype(BF16)
    shift = jnp.bitwise_and(LANES - 2 * hp * g, LANES - 1)
    dtg_scr[...] = _softplus(pltpu.roll(dt_ref[...], shift, axis=1) + dtb_ref[...])
    neg_a = -jnp.exp(alog_ref[...])

    tl = tl_ref[...]
    tu = tu_ref[...]
    ti = lax.broadcasted_iota(jnp.int32, (q, q), 0)
    si = lax.broadcasted_iota(jnp.int32, (q, q), 1)
    lower = si <= ti
    upper = si >= ti

    def intra(ci, carry):
        r0 = pl.multiple_of(ci * q, q)
        sl = pl.ds(r0, q)
        dt = dtg_scr[sl, :]
        da = dt * neg_a
        cum_f = _sum_rows_exact(tl, da)
        cum_b = _sum_rows_exact(tu, da)
        da_t = da.T[0:8, :]
        dt_t = dt.T[0:8, :]
        row_f = _sum_cols_exact(da_t, tu)
        row_b = _sum_cols_exact(da_t, tl)
        xc = xc_scr[sl, :]
        bm = bc_scr[sl, :]
        cm = cc_scr[sl, :]
        gmat = _dot_nt(cm, bm)
        wend_f = jnp.exp(cum_f[q - 1:q, :] - cum_f) * dt
        wend_b = jnp.exp(cum_b[0:1, :] - cum_b) * dt
        ecum_scr[sl, 0:LANES] = jnp.exp(cum_f)
        ecum_scr[sl, LANES:2 * LANES] = jnp.exp(cum_b)
        ys, xw_f, xw_b = [], [], []
        for j in range(hp):
            jf, jb = j, hp + j
            dec_f = jnp.exp(jnp.minimum(cum_f[:, jf:jf + 1] - row_f[jf:jf + 1, :], 0.0)) * dt_t[jf:jf + 1, :]
            dec_b = jnp.exp(jnp.minimum(cum_b[:, jb:jb + 1] - row_b[jb:jb + 1, :], 0.0)) * dt_t[jb:jb + 1, :]
            mh = gmat * (jnp.where(lower, dec_f, 0.0) + jnp.where(upper, dec_b, 0.0))
            xh = xc[:, j * p:(j + 1) * p]
            ys.append(jnp.dot(mh.astype(BF16), xh.astype(BF16), preferred_element_type=F32))
            xw_f.append(xh * wend_f[:, jf:jf + 1])
            xw_b.append(xh * wend_b[:, jb:jb + 1])
        acc_scr[sl, :] = jnp.concatenate(ys, axis=1)
        u_scr[ci, 0] = _dot_tn(jnp.concatenate(xw_f, axis=1).astype(BF16), bm)
        u_scr[ci, 1] = _dot_tn(jnp.concatenate(xw_b, axis=1).astype(BF16), bm)
        dl_scr[ci, 0:1, :] = jnp.exp(cum_f[q - 1:q, :])
        dl_scr[ci, 1:2, :] = jnp.exp(cum_b[0:1, :])
        return carry

    lax.fori_loop(0, n_chunks, intra, 0)

    def head_scale(vec, lane0):
        return jnp.concatenate(
            [jnp.broadcast_to(vec[0:1, lane0 + j:lane0 + j + 1], (p, SSD_STATE)) for j in range(hp)], axis=0)

    if has_s0:
        st_f = s0_ref[0].reshape(hp * p, SSD_STATE)
        st_b = s0_ref[1].reshape(hp * p, SSD_STATE)
    else:
        st_f = jnp.zeros((hp * p, SSD_STATE), F32)
        st_b = jnp.zeros((hp * p, SSD_STATE), F32)
    for ci in range(n_chunks):
        st_scr[ci, 0:hp * p, :] = st_f.astype(BF16)
        st_f = st_f * head_scale(dl_scr[ci, 0:1, :], 0) + u_scr[ci, 0]
    for ci in reversed(range(n_chunks)):
        st_scr[ci, hp * p:2 * hp * p, :] = st_b.astype(BF16)
        st_b = st_b * head_scale(dl_scr[ci, 1:2, :], hp) + u_scr[ci, 1]
    if want_state:
        sf_ref[0] = st_f.reshape(hp, p, SSD_STATE)
        sf_ref[1] = st_b.reshape(hp, p, SSD_STATE)

    dsk = dsk_ref[...]
    nw = nw_ref[...]

    def inter(ci, carry):
        r0 = pl.multiple_of(ci * q, q)
        sl = pl.ds(r0, q)
        yi = _dot_nt(cc_scr[sl, :], st_scr[ci])
        ec = ecum_scr[sl, :]
        parts = []
        for j in range(hp):
            parts.append(yi[:, j * p:(j + 1) * p] * ec[:, j:j + 1]
                         + yi[:, (hp + j) * p:(hp + j + 1) * p] * ec[:, LANES + hp + j:LANES + hp + j + 1])
        y = acc_scr[sl, :] + jnp.concatenate(parts, axis=1) + dsk * xc_scr[sl, :]
        y = y * _silu(z_ref[sl, :])
        o_ref[sl, :] = _rms(y) * nw
        return carry

    lax.fori_loop(0, n_chunks, inter, 0)


def _ssd_call(proj, dt_raw, p, consts, s0, batch, seq_len, width, rows, want_state):
    tl, tu = consts
    has_s0 = s0 is not None
    n_chunks = seq_len // CHUNK
    m = proj.shape[0]
    gw = GROUP_W // LANES
    z0 = (3 * HGRN_KW + 2 * HGRN_W) // GROUP_W
    x0 = (3 * HGRN_KW + 2 * HGRN_W + SSD_W) // GROUP_W
    b0 = (3 * HGRN_KW + 2 * HGRN_W + 2 * SSD_W) // LANES
    c0 = b0 + SSD_GROUPS
    del gw
    full = lambda arr: pl.BlockSpec(arr.shape, lambda b, g: (0,) * arr.ndim)
    in_specs = [
        pl.BlockSpec((seq_len, GROUP_W), lambda b, g: (b, z0 + g)),
        pl.BlockSpec((seq_len, GROUP_W), lambda b, g: (b, x0 + g)),
        pl.BlockSpec((seq_len, LANES), lambda b, g: (b, b0 + g)),
        pl.BlockSpec((seq_len, LANES), lambda b, g: (b, c0 + g)),
        pl.BlockSpec((seq_len, LANES), lambda b, g: (b, 0)),
        pl.BlockSpec((9, GROUP_W), lambda b, g: (0, g)),
        pl.BlockSpec((9, LANES), lambda b, g: (0, SSD_W // LANES + g)),
        pl.BlockSpec((9, LANES), lambda b, g: (0, SSD_W // LANES + SSD_GROUPS + g)),
        pl.BlockSpec((1, GROUP_W), lambda b, g: (0, g)),
        pl.BlockSpec((1, LANES), lambda b, g: (0, SSD_W // LANES + g)),
        pl.BlockSpec((1, LANES), lambda b, g: (0, SSD_W // LANES + SSD_GROUPS + g)),
        pl.BlockSpec((None, 1, LANES), lambda b, g: (g, 0, 0)),
        pl.BlockSpec((None, 1, LANES), lambda b, g: (g, 0, 0)),
        pl.BlockSpec((1, GROUP_W), lambda b, g: (0, g)),
        pl.BlockSpec((1, GROUP_W), lambda b, g: (0, g)),
        full(tl), full(tu),
    ]
    args = [proj, proj, proj, proj, dt_raw, p['conv_w'], p['conv_w'], p['conv_w'],
            p['conv_b'], p['conv_b'], p['conv_b'], p['a_log_rows'], p['dt_bias_rows'],
            p['d_rows'], p['norm_w'], tl, tu]
    state_spec = pl.BlockSpec((None, 2, HEADS_PER_GROUP, SSD_HEADDIM, SSD_STATE),
                              lambda b, g: (b, 0, g, 0, 0))
    if has_s0:
        in_specs.append(state_spec)
        args.append(s0)
    out_specs = [pl.BlockSpec((seq_len, GROUP_W), lambda b, g: (b, g))]
    out_shape = [jax.ShapeDtypeStruct((m, SSD_W), F32)]
    if want_state:
        out_specs.append(state_spec)
        out_shape.append(jax.ShapeDtypeStruct((batch, 2, SSD_HEADS, SSD_HEADDIM, SSD_STATE), F32))
    hpp = HEADS_PER_GROUP * SSD_HEADDIM
    scratch = [pltpu.VMEM((seq_len, GROUP_W), F32),
               pltpu.VMEM((seq_len, SSD_STATE), BF16),
               pltpu.VMEM((seq_len, SSD_STATE), BF16),
               pltpu.VMEM((seq_len, LANES), F32),
               pltpu.VMEM((seq_len, GROUP_W), F32),
               pltpu.VMEM((seq_len, 2 * LANES), F32),
               pltpu.VMEM((n_chunks, 2, hpp, SSD_STATE), F32),
               pltpu.VMEM((n_chunks, 8, LANES), F32),
               pltpu.VMEM((n_chunks, 2 * hpp, SSD_STATE), BF16)]
    res = pl.pallas_call(
        functools.partial(_ssd_kernel, seq_len=seq_len, width=width, rows=rows, has_s0=has_s0,
                          want_state=want_state),
        grid=(batch, SSD_GROUPS),
        in_specs=in_specs, out_specs=out_specs, out_shape=out_shape,
        scratch_shapes=scratch,
        compiler_params=_cparams(("arbitrary", "arbitrary")),
        name="ssd",
    )(*args)
    return (res[0], res[1]) if want_state else (res[0], None)


def _outup_kernel(x_ref, oh_ref, os_ref, g1_ref, sh_ref, sc_ref, nw_ref, wo_ref, wu_ref,
                  x1_ref, u_ref, h_scr):
    @pl.when(pl.program_id(1) == 0)
    def _():
        mix = jnp.dot(oh_ref[...].astype(BF16), wo_ref[0:HGRN_W, :], preferred_element_type=F32)
        mix = mix + jnp.dot(os_ref[...].astype(BF16), wo_ref[HGRN_W:MIX_W, :], preferred_element_type=F32)
        x1 = x_ref[...] + g1_ref[...] * mix
        x1_ref[...] = x1
        h = _rms(x1) * nw_ref[...] * (1.0 + sc_ref[...]) + sh_ref[...]
        h_scr[...] = h.astype(BF16)

    u_ref[...] = jnp.dot(h_scr[...], wu_ref[...], preferred_element_type=F32)


def _outup_call(x2, o_h, o_s, mod3, nw, w_out, w_up, seq_len):
    m = x2.shape[0]
    tm, tn = 512, 512
    bc = mod3.shape[0]
    seq_of = (lambda i: (i * tm) // seq_len) if bc > 1 else (lambda i: 0)
    modspec = lambda part: pl.BlockSpec((None, 1, D_MODEL), lambda i, j, part=part: (seq_of(i), 0, part))
    return pl.pallas_call(
        _outup_kernel,
        grid=(m // tm, 2 * D_FF // tn),
        in_specs=[
            pl.BlockSpec((tm, D_MODEL), lambda i, j: (i, 0)),
            pl.BlockSpec((tm, HGRN_W), lambda i, j: (i, 0)),
            pl.BlockSpec((tm, SSD_W), lambda i, j: (i, 0)),
            modspec(2), modspec(3), modspec(4),
            pl.BlockSpec((1, D_MODEL), lambda i, j: (0, 0)),
            pl.BlockSpec((MIX_W, D_MODEL), lambda i, j: (0, 0)),
            pl.BlockSpec((D_MODEL, tn), lambda i, j: (0, j)),
        ],
        out_specs=[
            pl.BlockSpec((tm, D_MODEL), lambda i, j: (i, 0)),
            pl.BlockSpec((tm, tn), lambda i, j: (i, j)),
        ],
        out_shape=[jax.ShapeDtypeStruct((m, D_MODEL), F32),
                   jax.ShapeDtypeStruct((m, 2 * D_FF), F32)],
        scratch_shapes=[pltpu.VMEM((tm, D_MODEL), BF16)],
        compiler_params=_cparams(("arbitrary", "arbitrary")),
        name="outup",
    )(x2, o_h, o_s, mod3, mod3, mod3, nw, w_out, w_up)


def _ffndown_kernel(x1_ref, ug_ref, uv_ref, g2_ref, cwg_ref, cwv_ref, cbg_ref, cbv_ref, wd_ref, fw_ref,
                    o_ref, acc_scr, *, width, rows, final):
    k = pl.program_id(1)

    @pl.when(k == 0)
    def _():
        acc_scr[...] = jnp.zeros_like(acc_scr)

    gate = _dwconv(ug_ref[...], cwg_ref, cbg_ref, width, rows)
    val = _dwconv(uv_ref[...], cwv_ref, cbv_ref, width, rows)
    act = (_silu(gate) * val).astype(BF16)
    acc_scr[...] += jnp.dot(act, wd_ref[...], preferred_element_type=F32)

    @pl.when(k == pl.num_programs(1) - 1)
    def _():
        x2 = x1_ref[...] + g2_ref[...] * acc_scr[...]
        if final:
            x2 = _rms(x2) * fw_ref[...]
        o_ref[...] = x2


def _ffndown_call(x1, u, mod3, conv_w, conv_b, w_down, final_w, batch, seq_len, width, rows, final):
    m = x1.shape[0]
    tk = 256
    nk = D_FF // tk
    bc = mod3.shape[0]
    seq_of = (lambda b: b) if bc > 1 else (lambda b: 0)
    return pl.pallas_call(
        functools.partial(_ffndown_kernel, width=width, rows=rows, final=final),
        grid=(batch, nk),
        in_specs=[
            pl.BlockSpec((seq_len, D_MODEL), lambda b, k: (b, 0)),
            pl.BlockSpec((seq_len, tk), lambda b, k: (b, k)),
            pl.BlockSpec((seq_len, tk), lambda b, k: (b, nk + k)),
            pl.BlockSpec((None, 1, D_MODEL), lambda b, k: (seq_of(b), 0, 5)),
            pl.BlockSpec((9, tk), lambda b, k: (0, k)),
            pl.BlockSpec((9, tk), lambda b, k: (0, nk + k)),
            pl.BlockSpec((1, tk), lambda b, k: (0, k)),
            pl.BlockSpec((1, tk), lambda b, k: (0, nk + k)),
            pl.BlockSpec((tk, D_MODEL), lambda b, k: (k, 0)),
            pl.BlockSpec((1, D_MODEL), lambda b, k: (0, 0)),
        ],
        out_specs=pl.BlockSpec((seq_len, D_MODEL), lambda b, k: (b, 0)),
        out_shape=jax.ShapeDtypeStruct((m, D_MODEL), F32),
        scratch_shapes=[pltpu.VMEM((seq_len, D_MODEL), F32)],
        compiler_params=_cparams(("arbitrary", "arbitrary")),
        name="ffndown",
    )(x1, u, u, mod3, conv_w, conv_w, conv_b, conv_b, w_down, final_w)


def _ssd_param_rows(a_log, dt_bias, d_skip):
    def rows(v):
        r = v.reshape(2, SSD_GROUPS, HEADS_PER_GROUP).transpose(1, 0, 2).reshape(SSD_GROUPS, 2 * HEADS_PER_GROUP)
        return jnp.pad(r, ((0, 0), (0, LANES - 2 * HEADS_PER_GROUP))).reshape(SSD_GROUPS, 1, LANES)
    return rows(a_log), rows(dt_bias), jnp.repeat(d_skip, SSD_HEADDIM).reshape(1, SSD_W)


def _permute_dt_weight(w_dt):
    w = w_dt.reshape(D_MODEL, 2, SSD_GROUPS, HEADS_PER_GROUP).transpose(0, 2, 1, 3).reshape(D_MODEL, DT_COLS)
    return jnp.pad(w, ((0, 0), (0, LANES - DT_COLS)))


def _run_pass(x, mod_all_rows, s_h0, s_s0, layers, hgrn_lb, final_norm_w, consts, width, rows, want_state):
    batch, seq_len, _ = x.shape
    m = batch * seq_len
    x2 = x.reshape(m, D_MODEL)
    wf, wb, lv = consts
    tl, tu = wf[0:CHUNK], wb[0:CHUNK]
    new_h, new_s = [], []
    for l, p in enumerate(layers):
        mod3 = mod_all_rows[l]
        proj, dt_raw = _inproj_call(x2, mod3, p['norm_w1'], p['w_in_main'], p['w_in_dt'], seq_len)
        o_h, s_h = _hgrn_call(proj, hgrn_lb, p['hgrn_norm_w'], (wf, wb, lv),
                              None if s_h0 is None else s_h0[:, l], l, batch, seq_len, want_state)
        o_s, s_s = _ssd_call(proj, dt_raw, p['ssd'], (tl, tu),
                             None if s_s0 is None else s_s0[:, l], batch, seq_len, width, rows, want_state)
        x1, u = _outup_call(x2, o_h, o_s, mod3, p['norm_w2'], p['w_out'], p['ffn_up'], seq_len)
        x2 = _ffndown_call(x1, u, mod3, p['ffn_conv_w'], p['ffn_conv_b'], p['ffn_down'], final_norm_w,
                           batch, seq_len, width, rows, final=(l == DEPTH - 1))
        new_h.append(s_h)
        new_s.append(s_s)
    return x2.reshape(batch, seq_len, D_MODEL), new_h, new_s


def kernel(x_prompt, x_sample, c, state_hgrn, state_ssd, c_ctx, norm_w, final_norm_w, w_ada, b_ada,
           w_in, w_out, hgrn_lb, hgrn_norm_w, ssd_conv_w, ssd_conv_b, ssd_a_log, ssd_dt_bias, ssd_d,
           ssd_norm_w, ffn_up, ffn_conv_w, ffn_conv_b, ffn_down):
    dec_batch = c.shape[0]
    consts = _scan_constants()

    cond8 = jnp.concatenate([c_ctx[None], c, jnp.zeros((8 - 1 - dec_batch, D_MODEL), F32)], axis=0)
    mod_all = _mod_call(cond8, w_ada, b_ada)
    mod_ctx = [mod_all[l, 0:1].reshape(1, 1, 6 * D_MODEL) for l in range(DEPTH)]
    mod_lat = [mod_all[l, 1:1 + dec_batch].reshape(dec_batch, 1, 6 * D_MODEL) for l in range(DEPTH)]

    layers = []
    for l in range(DEPTH):
        a_rows, b_rows, d_rows = _ssd_param_rows(ssd_a_log[l], ssd_dt_bias[l], ssd_d[l])
        layers.append(dict(
            norm_w1=norm_w[l, 0].reshape(1, D_MODEL), norm_w2=norm_w[l, 1].reshape(1, D_MODEL),
            w_in_main=w_in[l, :, :MAIN_COLS].astype(BF16),
            w_in_dt=_permute_dt_weight(w_in[l, :, MAIN_COLS:]).astype(BF16),
            w_out=w_out[l].astype(BF16),
            hgrn_norm_w=hgrn_norm_w[l].reshape(1, HGRN_DV),
            ssd=dict(conv_w=ssd_conv_w[l].reshape(9, CONV_CH), conv_b=ssd_conv_b[l].reshape(1, CONV_CH),
                     a_log_rows=a_rows, dt_bias_rows=b_rows, d_rows=d_rows,
                     norm_w=ssd_norm_w[l].reshape(1, SSD_W)),
            ffn_up=ffn_up[l].astype(BF16),
            ffn_conv_w=ffn_conv_w[l].reshape(9, 2 * D_FF), ffn_conv_b=ffn_conv_b[l].reshape(1, 2 * D_FF),
            ffn_down=ffn_down[l].astype(BF16),
        ))
    fnw = final_norm_w.reshape(1, D_MODEL)

    y_prompt, new_h, new_s = _run_pass(x_prompt, mod_ctx, None, None, layers, hgrn_lb, fnw, consts,
                                       width=x_prompt.shape[1], rows=1, want_state=True)
    y_sample, _, _ = _run_pass(x_sample, mod_lat, state_hgrn, state_ssd, layers, hgrn_lb, fnw, consts,
                               width=GRID_W, rows=x_sample.shape[1] // GRID_W, want_state=False)
    return (y_prompt, y_sample, jnp.stack(new_h, axis=1), jnp.stack(new_s, axis=1))
```

```python
import functools
import math

import numpy as np
import jax
import jax.numpy as jnp
from jax import lax
from jax.experimental import pallas as pl
from jax.experimental.pallas import tpu as pltpu

F32 = jnp.float32
BF16 = jnp.bfloat16

D_MODEL = 1024
DEPTH = 2
GRID_W = 64
HGRN_HEADS = 8
HGRN_DK = 128
HGRN_DV = 128
HGRN_KW = HGRN_HEADS * HGRN_DK
HGRN_W = HGRN_HEADS * HGRN_DV
SSD_W = 1024
SSD_HEADDIM = 64
SSD_HEADS = 16
SSD_GROUPS = 4
SSD_STATE = 128
HEADS_PER_GROUP = SSD_HEADS // SSD_GROUPS
GROUP_W = SSD_W // SSD_GROUPS
MIX_W = HGRN_W + SSD_W
CONV_CH = SSD_W + 2 * SSD_GROUPS * SSD_STATE
D_FF = 2816
MAIN_COLS = 3 * HGRN_KW + 2 * HGRN_W + SSD_W + CONV_CH
DT_COLS = 2 * SSD_HEADS
EPS = 1e-6
LOG2E = math.log2(math.e)

LANES = 128
CHUNK = 128
N_LEVELS = 7
VMEM_LIMIT = 56 * 1024 * 1024


def _cparams(sem):
    return pltpu.CompilerParams(dimension_semantics=sem, vmem_limit_bytes=VMEM_LIMIT)


def _sigmoid(x):
    return 1.0 / (1.0 + jnp.exp(-x))


def _silu(x):
    return x * _sigmoid(x)


def _softplus(x):
    return jnp.maximum(x, 0.0) + jnp.log1p(jnp.exp(-jnp.abs(x)))


def _rms(x):
    return x * lax.rsqrt(jnp.mean(x * x, axis=-1, keepdims=True) + EPS)


def _split3(x):
    hi = x.astype(BF16)
    r = x - hi.astype(F32)
    mid = r.astype(BF16)
    lo = (r - mid.astype(F32)).astype(BF16)
    return hi, mid, lo


def _sum_rows_exact(w01, x):
    n = x.shape[1]
    hi, mid, lo = _split3(x)
    p = jnp.dot(w01, jnp.concatenate([hi, mid, lo], axis=1), preferred_element_type=F32)
    return p[:, :n] + p[:, n:2 * n] + p[:, 2 * n:]


def _sum_cols_exact(x, w01):
    m = x.shape[0]
    hi, mid, lo = _split3(x)
    p = jnp.dot(jnp.concatenate([hi, mid, lo], axis=0), w01, preferred_element_type=F32)
    return p[:m] + p[m:2 * m] + p[2 * m:]


def _dot_nt(a, b):
    return lax.dot_general(a, b, (((1,), (1,)), ((), ())), preferred_element_type=F32)


def _dot_tn(a, b):
    return lax.dot_general(a, b, (((0,), (0,)), ((), ())), preferred_element_type=F32)


def _dwconv(x, w_ref, b_ref, width, rows):
    seq, _ = x.shape
    t = lax.broadcasted_iota(jnp.int32, x.shape, 0)
    col = jnp.bitwise_and(t, width - 1)
    xl = jnp.where(col == 0, 0.0, pltpu.roll(x, 1, axis=0))
    xr = jnp.where(col == width - 1, 0.0, pltpu.roll(x, seq - 1, axis=0))

    def hrow(kh, a, b, c):
        return w_ref[3 * kh:3 * kh + 1, :] * a + w_ref[3 * kh + 1:3 * kh + 2, :] * b \
            + w_ref[3 * kh + 2:3 * kh + 3, :] * c

    out = b_ref[...] + hrow(1, xl, x, xr)
    if rows > 1:
        up = hrow(0, pltpu.roll(xl, width, axis=0), pltpu.roll(x, width, axis=0),
                  pltpu.roll(xr, width, axis=0))
        out = out + jnp.where(t >= width, up, 0.0)
        dn = hrow(2, pltpu.roll(xl, seq - width, axis=0), pltpu.roll(x, seq - width, axis=0),
                  pltpu.roll(xr, seq - width, axis=0))
        out = out + jnp.where(t < seq - width, dn, 0.0)
    return out


def _scan_constants():
    c = CHUNK
    t = np.arange(c)[:, None]
    i = np.arange(c)[None, :]
    x = t ^ i
    lv = np.where(x == 0, 0, np.floor(np.log2(np.maximum(x, 1))).astype(np.int64) + 1)
    return jnp.asarray(i <= t, BF16), jnp.asarray(i >= t, BF16), jnp.asarray(lv, jnp.int32)


def _level_signs():
    t = np.arange(CHUNK)[:, None]
    sg = [np.where(((t >> lev) & 1) == 1, 1.0, -1.0) * np.ones((1, LANES)) for lev in range(N_LEVELS)]
    return jnp.asarray(np.concatenate(sg, 0), F32)


def _mod_kernel(cond_ref, w_ref, b_ref, o_ref):
    a = _silu(cond_ref[...]).astype(BF16)
    o_ref[...] = jnp.dot(a, w_ref[...].astype(BF16), preferred_element_type=F32) + b_ref[...]


def _mod_call(cond8, w_ada, b_ada):
    n = 6 * D_MODEL
    tn = D_MODEL
    return pl.pallas_call(
        _mod_kernel,
        grid=(DEPTH, n // tn),
        in_specs=[
            pl.BlockSpec((8, D_MODEL), lambda l, j: (0, 0)),
            pl.BlockSpec((None, D_MODEL, tn), lambda l, j: (l, 0, j)),
            pl.BlockSpec((None, 1, tn), lambda l, j: (l, 0, j)),
        ],
        out_specs=pl.BlockSpec((None, 8, tn), lambda l, j: (l, 0, j)),
        out_shape=jax.ShapeDtypeStruct((DEPTH, 8, n), F32),
        compiler_params=_cparams(("arbitrary", "arbitrary")),
        name="mod",
    )(cond8, w_ada, b_ada.reshape(DEPTH, 1, n))


def _inproj_kernel(x_ref, sh_ref, sc_ref, nw_ref, w_ref, wdt_ref, o_ref, odt_ref, hn_ref):
    @pl.when(pl.program_id(1) == 0)
    def _():
        h = _rms(x_ref[...]) * nw_ref[...] * (1.0 + sc_ref[...]) + sh_ref[...]
        hb = h.astype(BF16)
        hn_ref[...] = hb
        odt_ref[...] = jnp.dot(hb, wdt_ref[...], preferred_element_type=F32)

    o_ref[...] = jnp.dot(hn_ref[...], w_ref[...], preferred_element_type=F32)


def _inproj_call(x2, mod3, nw, w_main, w_dt, seq_len):
    m = x2.shape[0]
    n = w_main.shape[1]
    tm, tn = 512, 1024
    bc = mod3.shape[0]
    seq_of = (lambda i: (i * tm) // seq_len) if bc > 1 else (lambda i: 0)
    return pl.pallas_call(
        _inproj_kernel,
        grid=(m // tm, n // tn),
        in_specs=[
            pl.BlockSpec((tm, D_MODEL), lambda i, j: (i, 0)),
            pl.BlockSpec((None, 1, D_MODEL), lambda i, j: (seq_of(i), 0, 0)),
            pl.BlockSpec((None, 1, D_MODEL), lambda i, j: (seq_of(i), 0, 1)),
            pl.BlockSpec((1, D_MODEL), lambda i, j: (0, 0)),
            pl.BlockSpec((D_MODEL, tn), lambda i, j: (0, j)),
            pl.BlockSpec((D_MODEL, LANES), lambda i, j: (0, 0)),
        ],
        out_specs=[
            pl.BlockSpec((tm, tn), lambda i, j: (i, j)),
            pl.BlockSpec((tm, LANES), lambda i, j: (i, 0)),
            pl.BlockSpec((tm, D_MODEL), lambda i, j: (i, 0)),
        ],
        out_shape=[jax.ShapeDtypeStruct((m, n), F32),
                   jax.ShapeDtypeStruct((m, LANES), F32),
                   jax.ShapeDtypeStruct((m, D_MODEL), BF16)],
        compiler_params=_cparams(("arbitrary", "arbitrary")),
        name="inproj",
    )(x2, mod3, mod3, nw, w_main, w_dt)


def _hgrn_gates(f_pre, lbd):
    ea = jnp.exp(-jnp.abs(f_pre))
    log_sig = jnp.minimum(f_pre, 0.0) - jnp.log1p(ea)
    sig_neg = jnp.where(f_pre >= 0.0, ea, 1.0) / (1.0 + ea)
    if lbd is None:
        return log_sig * LOG2E, sig_neg
    a = jnp.log(lbd)
    b = jnp.log1p(-lbd) + log_sig
    log_f = jnp.maximum(a, b) + jnp.log1p(jnp.exp(-jnp.abs(a - b)))
    return log_f * LOG2E, (1.0 - lbd) * sig_neg


def _block_mid(cum, scr, m):
    c = CHUNK
    if m >= 4:
        return jnp.concatenate(
            [jnp.broadcast_to(scr[r0 + m - 1:r0 + m, :], (2 * m, scr.shape[1])) for r0 in range(0, c, 2 * m)],
            axis=0)
    row = lax.broadcasted_iota(jnp.int32, cum.shape, 0)
    if m == 1:
        return jnp.where(jnp.bitwise_and(row, 1) == 1, pltpu.roll(cum, 1, axis=0), cum)
    r = jnp.bitwise_and(row, 3)
    return jnp.where(r == 0, pltpu.roll(cum, c - 1, axis=0),
                     jnp.where(r == 1, cum,
                               jnp.where(r == 2, pltpu.roll(cum, 1, axis=0), pltpu.roll(cum, 2, axis=0))))


def _pick_halves(up_val, low_val, m, up_mask):
    if m >= 8:
        pieces = []
        for r0 in range(0, CHUNK, 2 * m):
            pieces += [low_val[r0:r0 + m], up_val[r0 + m:r0 + 2 * m]]
        return jnp.concatenate(pieces, axis=0)
    return jnp.where(up_mask, up_val, low_val)


def _hgrn_kernel(*refs, layer, seq_len, seqs, has_s0, want_state):
    it = iter(refs)
    hn_ref, w_ref, lb_ref, nw_ref, tl_ref, lv_ref, sg_ref = (next(it) for _ in range(7))
    s0_ref = next(it) if has_s0 else None
    o_ref = next(it)
    sf_ref = next(it) if want_state else None
    (p_scr, acc_scr, qc_scr, u_scr, dl_scr, st_scr, cumf_scr, cumb_scr) = (next(it) for _ in range(8))

    c = CHUNK
    dk = HGRN_DK
    n_chunks = seqs * seq_len // c
    cps = seq_len // c

    p_scr[...] = jnp.dot(hn_ref[...], w_ref[...], preferred_element_type=F32)

    lb = None
    if layer > 0:
        lbr = lb_ref[...]
        e = jnp.exp(lbr - jnp.max(lbr, axis=0, keepdims=True))
        sm = e / jnp.sum(e, axis=0, keepdims=True)
        lb = sm[1]
        for i in range(2, layer + 1):
            lb = lb + sm[i]

    lv = lv_ref[...]
    tl = tl_ref[...]
    row = lax.broadcasted_iota(jnp.int32, (c, dk), 0)
    zero = jnp.zeros((c, dk), F32)

    def intra(ci, carry):
        r0 = pl.multiple_of(ci * c, c)
        sl = pl.ds(r0, c)
        q = _silu(p_scr[sl, 0:dk]) * (dk ** -0.5)
        lf_f, kf = _hgrn_gates(p_scr[sl, dk:2 * dk], None if lb is None else lb[0:1, :])
        lf_b, kb = _hgrn_gates(p_scr[sl, 2 * dk:3 * dk], None if lb is None else lb[1:2, :])
        vb = p_scr[sl, 3 * dk:4 * dk].astype(BF16)

        cum_f = _sum_rows_exact(tl, lf_f)
        cum_b = _sum_rows_exact(tl, lf_b)
        cx_b = cum_b - lf_b
        cumf_scr[...] = cum_f
        cumb_scr[...] = cum_b
        tot_f = cum_f[c - 1:c, :]
        tot_b = cum_b[c - 1:c, :]

        a_sum = _dot_nt(q.astype(BF16), (kf + kb).astype(BF16))
        for lev in range(1, N_LEVELS + 1):
            m = 1 << (lev - 1)
            up = jnp.bitwise_and(jnp.right_shift(row, lev - 1), 1) == 1
            sgn = sg_ref[(lev - 1) * c:lev * c, :]
            ef = jnp.exp2((cum_f - _block_mid(cum_f, cumf_scr, m)) * sgn)
            eb = jnp.exp2((cx_b - _block_mid(cum_b, cumb_scr, m)) * sgn)
            qe = q * _pick_halves(ef, eb, m, up)
            ke = _pick_halves(kb, kf, m, up) * _pick_halves(eb, ef, m, up)
            lhs = jnp.concatenate([_pick_halves(qe, zero, m, up), _pick_halves(zero, qe, m, up)], axis=1)
            rhs = jnp.concatenate([_pick_halves(zero, ke, m, up), _pick_halves(ke, zero, m, up)], axis=1)
            a_sum = jnp.where(lv == lev, _dot_nt(lhs.astype(BF16), rhs.astype(BF16)), a_sum)
        acc_scr[sl, :] = jnp.dot(a_sum.astype(BF16), vb, preferred_element_type=F32)

        qc_scr[sl, :] = jnp.concatenate(
            [q * jnp.exp2(cum_f), q * jnp.exp2(jnp.minimum(tot_b - cx_b, 0.0))], axis=1).astype(BF16)
        u_scr[ci, 0] = _dot_tn(vb, (kf * jnp.exp2(jnp.minimum(tot_f - cum_f, 0.0))).astype(BF16))
        u_scr[ci, 1] = _dot_tn(vb, (kb * jnp.exp2(jnp.minimum(cx_b, 0.0))).astype(BF16))
        dl_scr[ci, 0:1, :] = jnp.exp2(tot_f)
        dl_scr[ci, 1:2, :] = jnp.exp2(tot_b)
        return carry

    lax.fori_loop(0, n_chunks, intra, 0)

    def init(s, d):
        return s0_ref[s, d].T if has_s0 else jnp.zeros((HGRN_DV, dk), F32)

    st = None
    for ci in range(n_chunks):
        s = ci // cps
        if ci % cps == 0:
            st = init(s, 0)
        st_scr[ci, :, 0:dk] = st.astype(BF16)
        st = st * dl_scr[ci, 0:1, :] + u_scr[ci, 0]
        if want_state and ci % cps == cps - 1:
            sf_ref[s, 0] = st.T
    for ci in reversed(range(n_chunks)):
        s = ci // cps
        if ci % cps == cps - 1:
            st = init(s, 1)
        st_scr[ci, :, dk:2 * dk] = st.astype(BF16)
        st = st * dl_scr[ci, 1:2, :] + u_scr[ci, 1]
        if want_state and ci % cps == 0:
            sf_ref[s, 1] = st.T

    nw = nw_ref[...]

    def inter(ci, carry):
        r0 = pl.multiple_of(ci * c, c)
        sl = pl.ds(r0, c)
        o = acc_scr[sl, :] + _dot_nt(qc_scr[sl, :], st_scr[ci])
        o_ref[sl, :] = _rms(o) * nw * _silu(p_scr[sl, 4 * dk:5 * dk])
        return carry

    lax.fori_loop(0, n_chunks, inter, 0)


def _hgrn_call(hn, w_heads, lb_raw, nw, consts, s0, layer, batch, seq_len, want_state):
    tl, _, lv, sg = consts
    has_s0 = s0 is not None
    m = hn.shape[0]
    tokens = 1024
    seqs = tokens // seq_len
    n_chunks = tokens // CHUNK
    full = lambda arr: pl.BlockSpec(arr.shape, lambda h, b: (0,) * arr.ndim)
    state_spec = pl.BlockSpec((seqs, 2, None, HGRN_DK, HGRN_DV), lambda h, b: (b, 0, h, 0, 0))
    in_specs = [pl.BlockSpec((tokens, D_MODEL), lambda h, b: (b, 0)),
                pl.BlockSpec((None, D_MODEL, 5 * LANES), lambda h, b: (h, 0, 0)),
                pl.BlockSpec((DEPTH, 2, LANES), lambda h, b: (0, 0, h)),
                pl.BlockSpec((1, HGRN_DV), lambda h, b: (0, 0)),
                full(tl), full(lv), full(sg)]
    args = [hn, w_heads, lb_raw, nw, tl, lv, sg]
    if has_s0:
        in_specs.append(state_spec)
        args.append(s0)
    out_specs = [pl.BlockSpec((tokens, LANES), lambda h, b: (b, h))]
    out_shape = [jax.ShapeDtypeStruct((m, HGRN_W), F32)]
    if want_state:
        out_specs.append(state_spec)
        out_shape.append(jax.ShapeDtypeStruct((batch, 2, HGRN_HEADS, HGRN_DK, HGRN_DV), F32))
    scratch = [pltpu.VMEM((tokens, 5 * LANES), F32),
               pltpu.VMEM((tokens, HGRN_DV), F32),
               pltpu.VMEM((tokens, 2 * HGRN_DK), BF16),
               pltpu.VMEM((n_chunks, 2, HGRN_DV, HGRN_DK), F32),
               pltpu.VMEM((n_chunks, 8, HGRN_DK), F32),
               pltpu.VMEM((n_chunks, HGRN_DV, 2 * HGRN_DK), BF16),
               pltpu.VMEM((CHUNK, HGRN_DK), F32),
               pltpu.VMEM((CHUNK, HGRN_DK), F32)]
    res = pl.pallas_call(
        functools.partial(_hgrn_kernel, layer=layer, seq_len=seq_len, seqs=seqs, has_s0=has_s0,
                          want_state=want_state),
        grid=(HGRN_HEADS, m // tokens),
        in_specs=in_specs, out_specs=out_specs, out_shape=out_shape,
        scratch_shapes=scratch,
        compiler_params=_cparams(("arbitrary", "arbitrary")),
        name="hgrn",
    )(*args)
    return (res[0], res[1]) if want_state else (res[0], None)


def _ssd_kernel(*refs, seq_len, width, rows, has_s0, want_state):
    it = iter(refs)
    (z_ref, x_ref, b_ref, c_ref, dt_ref, cwx_ref, cwb_ref, cwc_ref, cbx_ref, cbb_ref, cbc_ref,
     alog_ref, dtb_ref, dsk_ref, nw_ref, tl_ref, tu_ref) = (next(it) for _ in range(17))
    s0_ref = next(it) if has_s0 else None
    o_ref = next(it)
    sf_ref = next(it) if want_state else None
    (xc_scr, bc_scr, cc_scr, dtg_scr, acc_scr, ecum_scr, u_scr, dl_scr, st_scr) = (
        next(it) for _ in range(9))

    q = CHUNK
    n_chunks = seq_len // q
    hp = HEADS_PER_GROUP
    p = SSD_HEADDIM
    g = pl.program_id(1)

    xc_scr[...] = _silu(_dwconv(x_ref[...], cwx_ref, cbx_ref, width, rows))
    bc_scr[...] = _silu(_dwconv(b_ref[...], cwb_ref, cbb_ref, width, rows)).astype(BF16)
    cc_scr[...] = _silu(_dwconv(c_ref[...], cwc_ref, cbc_ref, width, rows)).astype(BF16)
    shift = jnp.bitwise_and(LANES - 2 * hp * g, LANES - 1)
    dtg_scr[...] = _softplus(pltpu.roll(dt_ref[...], shift, axis=1) + dtb_ref[...])
    neg_a = -jnp.exp(alog_ref[...])

    tl = tl_ref[...]
    tu = tu_ref[...]
    ti = lax.broadcasted_iota(jnp.int32, (q, q), 0)
    si = lax.broadcasted_iota(jnp.int32, (q, q), 1)
    lower = si <= ti
    upper = si >= ti

    def intra(ci, carry):
        r0 = pl.multiple_of(ci * q, q)
        sl = pl.ds(r0, q)
        dt = dtg_scr[sl, :]
        da = dt * neg_a
        cum_f = _sum_rows_exact(tl, da)
        cum_b = _sum_rows_exact(tu, da)
        da_t = da.T[0:8, :]
        dt_t = dt.T[0:8, :]
        row_f = _sum_cols_exact(da_t, tu)
        row_b = _sum_cols_exact(da_t, tl)
        xc = xc_scr[sl, :]
        bm = bc_scr[sl, :]
        cm = cc_scr[sl, :]
        gmat = _dot_nt(cm, bm)
        wend_f = jnp.exp(cum_f[q - 1:q, :] - cum_f) * dt
        wend_b = jnp.exp(cum_b[0:1, :] - cum_b) * dt
        ecum_scr[sl, 0:LANES] = jnp.exp(cum_f)
        ecum_scr[sl, LANES:2 * LANES] = jnp.exp(cum_b)
        ys, xw_f, xw_b = [], [], []
        for j in range(hp):
            jf, jb = j, hp + j
            dec_f = jnp.exp(jnp.minimum(cum_f[:, jf:jf + 1] - row_f[jf:jf + 1, :], 0.0)) * dt_t[jf:jf + 1, :]
            dec_b = jnp.exp(jnp.minimum(cum_b[:, jb:jb + 1] - row_b[jb:jb + 1, :], 0.0)) * dt_t[jb:jb + 1, :]
            mh = gmat * (jnp.where(lower, dec_f, 0.0) + jnp.where(upper, dec_b, 0.0))
            xh = xc[:, j * p:(j + 1) * p]
            ys.append(jnp.dot(mh.astype(BF16), xh.astype(BF16), preferred_element_type=F32))
            xw_f.append(xh * wend_f[:, jf:jf + 1])
            xw_b.append(xh * wend_b[:, jb:jb + 1])
        acc_scr[sl, :] = jnp.concatenate(ys, axis=1)
        u_scr[ci, 0] = _dot_tn(jnp.concatenate(xw_f, axis=1).astype(BF16), bm)
        u_scr[ci, 1] = _dot_tn(jnp.concatenate(xw_b, axis=1).astype(BF16), bm)
        dl_scr[ci, 0:1, :] = jnp.exp(cum_f[q - 1:q, :])
        dl_scr[ci, 1:2, :] = jnp.exp(cum_b[0:1, :])
        return carry

    lax.fori_loop(0, n_chunks, intra, 0)

    def head_scale(vec, lane0):
        return jnp.concatenate(
            [jnp.broadcast_to(vec[0:1, lane0 + j:lane0 + j + 1], (p, SSD_STATE)) for j in range(hp)], axis=0)

    if has_s0:
        st_f = s0_ref[0].reshape(hp * p, SSD_STATE)
        st_b = s0_ref[1].reshape(hp * p, SSD_STATE)
    else:
        st_f = jnp.zeros((hp * p, SSD_STATE), F32)
        st_b = jnp.zeros((hp * p, SSD_STATE), F32)
    for ci in range(n_chunks):
        st_scr[ci, 0:hp * p, :] = st_f.astype(BF16)
        st_f = st_f * head_scale(dl_scr[ci, 0:1, :], 0) + u_scr[ci, 0]
    for ci in reversed(range(n_chunks)):
        st_scr[ci, hp * p:2 * hp * p, :] = st_b.astype(BF16)
        st_b = st_b * head_scale(dl_scr[ci, 1:2, :], hp) + u_scr[ci, 1]
    if want_state:
        sf_ref[0] = st_f.reshape(hp, p, SSD_STATE)
        sf_ref[1] = st_b.reshape(hp, p, SSD_STATE)

    dsk = dsk_ref[...]
    nw = nw_ref[...]

    def inter(ci, carry):
        r0 = pl.multiple_of(ci * q, q)
        sl = pl.ds(r0, q)
        yi = _dot_nt(cc_scr[sl, :], st_scr[ci])
        ec = ecum_scr[sl, :]
        parts = []
        for j in range(hp):
            parts.append(yi[:, j * p:(j + 1) * p] * ec[:, j:j + 1]
                         + yi[:, (hp + j) * p:(hp + j + 1) * p] * ec[:, LANES + hp + j:LANES + hp + j + 1])
        y = acc_scr[sl, :] + jnp.concatenate(parts, axis=1) + dsk * xc_scr[sl, :]
        y = y * _silu(z_ref[sl, :])
        o_ref[sl, :] = _rms(y) * nw
        return carry

    lax.fori_loop(0, n_chunks, inter, 0)


def _ssd_call(proj, dt_raw, p, consts, s0, batch, seq_len, width, rows, want_state):
    tl, tu = consts
    has_s0 = s0 is not None
    n_chunks = seq_len // CHUNK
    m = proj.shape[0]
    z0 = 0
    x0 = SSD_W // GROUP_W
    b0 = 2 * SSD_W // LANES
    c0 = b0 + SSD_GROUPS
    full = lambda arr: pl.BlockSpec(arr.shape, lambda b, g: (0,) * arr.ndim)
    in_specs = [
        pl.BlockSpec((seq_len, GROUP_W), lambda b, g: (b, z0 + g)),
        pl.BlockSpec((seq_len, GROUP_W), lambda b, g: (b, x0 + g)),
        pl.BlockSpec((seq_len, LANES), lambda b, g: (b, b0 + g)),
        pl.BlockSpec((seq_len, LANES), lambda b, g: (b, c0 + g)),
        pl.BlockSpec((seq_len, LANES), lambda b, g: (b, 0)),
        pl.BlockSpec((9, GROUP_W), lambda b, g: (0, g)),
        pl.BlockSpec((9, LANES), lambda b, g: (0, SSD_W // LANES + g)),
        pl.BlockSpec((9, LANES), lambda b, g: (0, SSD_W // LANES + SSD_GROUPS + g)),
        pl.BlockSpec((1, GROUP_W), lambda b, g: (0, g)),
        pl.BlockSpec((1, LANES), lambda b, g: (0, SSD_W // LANES + g)),
        pl.BlockSpec((1, LANES), lambda b, g: (0, SSD_W // LANES + SSD_GROUPS + g)),
        pl.BlockSpec((None, 1, LANES), lambda b, g: (g, 0, 0)),
        pl.BlockSpec((None, 1, LANES), lambda b, g: (g, 0, 0)),
        pl.BlockSpec((1, GROUP_W), lambda b, g: (0, g)),
        pl.BlockSpec((1, GROUP_W), lambda b, g: (0, g)),
        full(tl), full(tu),
    ]
    args = [proj, proj, proj, proj, dt_raw, p['conv_w'], p['conv_w'], p['conv_w'],
            p['conv_b'], p['conv_b'], p['conv_b'], p['a_log_rows'], p['dt_bias_rows'],
            p['d_rows'], p['norm_w'], tl, tu]
    state_spec = pl.BlockSpec((None, 2, HEADS_PER_GROUP, SSD_HEADDIM, SSD_STATE),
                              lambda b, g: (b, 0, g, 0, 0))
    if has_s0:
        in_specs.append(state_spec)
        args.append(s0)
    out_specs = [pl.BlockSpec((seq_len, GROUP_W), lambda b, g: (b, g))]
    out_shape = [jax.ShapeDtypeStruct((m, SSD_W), F32)]
    if want_state:
        out_specs.append(state_spec)
        out_shape.append(jax.ShapeDtypeStruct((batch, 2, SSD_HEADS, SSD_HEADDIM, SSD_STATE), F32))
    hpp = HEADS_PER_GROUP * SSD_HEADDIM
    scratch = [pltpu.VMEM((seq_len, GROUP_W), F32),
               pltpu.VMEM((seq_len, SSD_STATE), BF16),
               pltpu.VMEM((seq_len, SSD_STATE), BF16),
               pltpu.VMEM((seq_len, LANES), F32),
               pltpu.VMEM((seq_len, GROUP_W), F32),
               pltpu.VMEM((seq_len, 2 * LANES), F32),
               pltpu.VMEM((n_chunks, 2, hpp, SSD_STATE), F32),
               pltpu.VMEM((n_chunks, 8, LANES), F32),
               pltpu.VMEM((n_chunks, 2 * hpp, SSD_STATE), BF16)]
    res = pl.pallas_call(
        functools.partial(_ssd_kernel, seq_len=seq_len, width=width, rows=rows, has_s0=has_s0,
                          want_state=want_state),
        grid=(batch, SSD_GROUPS),
        in_specs=in_specs, out_specs=out_specs, out_shape=out_shape,
        scratch_shapes=scratch,
        compiler_params=_cparams(("arbitrary", "arbitrary")),
        name="ssd",
    )(*args)
    return (res[0], res[1]) if want_state else (res[0], None)


def _outup_kernel(x_ref, oh_ref, os_ref, g1_ref, sh_ref, sc_ref, nw_ref, wo_ref, wu_ref,
                  x1_ref, u_ref, h_scr):
    @pl.when(pl.program_id(1) == 0)
    def _():
        mix = jnp.dot(oh_ref[...].astype(BF16), wo_ref[0:HGRN_W, :], preferred_element_type=F32)
        mix = mix + jnp.dot(os_ref[...].astype(BF16), wo_ref[HGRN_W:MIX_W, :], preferred_element_type=F32)
        x1 = x_ref[...] + g1_ref[...] * mix
        x1_ref[...] = x1
        h = _rms(x1) * nw_ref[...] * (1.0 + sc_ref[...]) + sh_ref[...]
        h_scr[...] = h.astype(BF16)

    u_ref[...] = jnp.dot(h_scr[...], wu_ref[...], preferred_element_type=F32)


def _outup_call(x2, o_h, o_s, mod3, nw, w_out, w_up, seq_len):
    m = x2.shape[0]
    tm, tn = 512, 512
    bc = mod3.shape[0]
    seq_of = (lambda i: (i * tm) // seq_len) if bc > 1 else (lambda i: 0)
    modspec = lambda part: pl.BlockSpec((None, 1, D_MODEL), lambda i, j, part=part: (seq_of(i), 0, part))
    return pl.pallas_call(
        _outup_kernel,
        grid=(m // tm, 2 * D_FF // tn),
        in_specs=[
            pl.BlockSpec((tm, D_MODEL), lambda i, j: (i, 0)),
            pl.BlockSpec((tm, HGRN_W), lambda i, j: (i, 0)),
            pl.BlockSpec((tm, SSD_W), lambda i, j: (i, 0)),
            modspec(2), modspec(3), modspec(4),
            pl.BlockSpec((1, D_MODEL), lambda i, j: (0, 0)),
            pl.BlockSpec((MIX_W, D_MODEL), lambda i, j: (0, 0)),
            pl.BlockSpec((D_MODEL, tn), lambda i, j: (0, j)),
        ],
        out_specs=[
            pl.BlockSpec((tm, D_MODEL), lambda i, j: (i, 0)),
            pl.BlockSpec((tm, tn), lambda i, j: (i, j)),
        ],
        out_shape=[jax.ShapeDtypeStruct((m, D_MODEL), F32),
                   jax.ShapeDtypeStruct((m, 2 * D_FF), F32)],
        scratch_shapes=[pltpu.VMEM((tm, D_MODEL), BF16)],
        compiler_params=_cparams(("arbitrary", "arbitrary")),
        name="outup",
    )(x2, o_h, o_s, mod3, mod3, mod3, nw, w_out, w_up)


def _ffndown_kernel(x1_ref, ug_ref, uv_ref, g2_ref, cwg_ref, cwv_ref, cbg_ref, cbv_ref, wd_ref, fw_ref,
                    o_ref, acc_scr, *, width, rows, final):
    k = pl.program_id(1)

    @pl.when(k == 0)
    def _():
        acc_scr[...] = jnp.zeros_like(acc_scr)

    gate = _dwconv(ug_ref[...], cwg_ref, cbg_ref, width, rows)
    val = _dwconv(uv_ref[...], cwv_ref, cbv_ref, width, rows)
    act = (_silu(gate) * val).astype(BF16)
    acc_scr[...] += jnp.dot(act, wd_ref[...], preferred_element_type=F32)

    @pl.when(k == pl.num_programs(1) - 1)
    def _():
        x2 = x1_ref[...] + g2_ref[...] * acc_scr[...]
        if final:
            x2 = _rms(x2) * fw_ref[...]
        o_ref[...] = x2


def _ffndown_call(x1, u, mod3, conv_w, conv_b, w_down, final_w, batch, seq_len, width, rows, final):
    m = x1.shape[0]
    tk = 256
    nk = D_FF // tk
    bc = mod3.shape[0]
    seq_of = (lambda b: b) if bc > 1 else (lambda b: 0)
    return pl.pallas_call(
        functools.partial(_ffndown_kernel, width=width, rows=rows, final=final),
        grid=(batch, nk),
        in_specs=[
            pl.BlockSpec((seq_len, D_MODEL), lambda b, k: (b, 0)),
            pl.BlockSpec((seq_len, tk), lambda b, k: (b, k)),
            pl.BlockSpec((seq_len, tk), lambda b, k: (b, nk + k)),
            pl.BlockSpec((None, 1, D_MODEL), lambda b, k: (seq_of(b), 0, 5)),
            pl.BlockSpec((9, tk), lambda b, k: (0, k)),
            pl.BlockSpec((9, tk), lambda b, k: (0, nk + k)),
            pl.BlockSpec((1, tk), lambda b, k: (0, k)),
            pl.BlockSpec((1, tk), lambda b, k: (0, nk + k)),
            pl.BlockSpec((tk, D_MODEL), lambda b, k: (k, 0)),
            pl.BlockSpec((1, D_MODEL), lambda b, k: (0, 0)),
        ],
        out_specs=pl.BlockSpec((seq_len, D_MODEL), lambda b, k: (b, 0)),
        out_shape=jax.ShapeDtypeStruct((m, D_MODEL), F32),
        scratch_shapes=[pltpu.VMEM((seq_len, D_MODEL), F32)],
        compiler_params=_cparams(("arbitrary", "arbitrary")),
        name="ffndown",
    )(x1, u, u, mod3, conv_w, conv_w, conv_b, conv_b, w_down, final_w)


def _ssd_param_rows(a_log, dt_bias, d_skip):
    def rows(v):
        r = v.reshape(2, SSD_GROUPS, HEADS_PER_GROUP).transpose(1, 0, 2).reshape(SSD_GROUPS, 2 * HEADS_PER_GROUP)
        return jnp.pad(r, ((0, 0), (0, LANES - 2 * HEADS_PER_GROUP))).reshape(SSD_GROUPS, 1, LANES)
    return rows(a_log), rows(dt_bias), jnp.repeat(d_skip, SSD_HEADDIM).reshape(1, SSD_W)


def _permute_dt_weight(w_dt):
    w = w_dt.reshape(D_MODEL, 2, SSD_GROUPS, HEADS_PER_GROUP).transpose(0, 2, 1, 3).reshape(D_MODEL, DT_COLS)
    return jnp.pad(w, ((0, 0), (0, LANES - DT_COLS)))


def _run_pass(x, mod_all_rows, s_h0, s_s0, layers, hgrn_lb, final_norm_w, consts, width, rows, want_state):
    batch, seq_len, _ = x.shape
    m = batch * seq_len
    x2 = x.reshape(m, D_MODEL)
    tl, tu = consts[0], consts[1]
    new_h, new_s = [], []
    for l, p in enumerate(layers):
        mod3 = mod_all_rows[l]
        proj, dt_raw, hn = _inproj_call(x2, mod3, p['norm_w1'], p['w_in_ssd'], p['w_in_dt'], seq_len)
        o_h, s_h = _hgrn_call(hn, p['w_in_hgrn'], hgrn_lb, p['hgrn_norm_w'], consts,
                              None if s_h0 is None else s_h0[:, l], l, batch, seq_len, want_state)
        o_s, s_s = _ssd_call(proj, dt_raw, p['ssd'], (tl, tu),
                             None if s_s0 is None else s_s0[:, l], batch, seq_len, width, rows, want_state)
        x1, u = _outup_call(x2, o_h, o_s, mod3, p['norm_w2'], p['w_out'], p['ffn_up'], seq_len)
        x2 = _ffndown_call(x1, u, mod3, p['ffn_conv_w'], p['ffn_conv_b'], p['ffn_down'], final_norm_w,
                           batch, seq_len, width, rows, final=(l == DEPTH - 1))
        new_h.append(s_h)
        new_s.append(s_s)
    return x2.reshape(batch, seq_len, D_MODEL), new_h, new_s


def kernel(x_prompt, x_sample, c, state_hgrn, state_ssd, c_ctx, norm_w, final_norm_w, w_ada, b_ada,
           w_in, w_out, hgrn_lb, hgrn_norm_w, ssd_conv_w, ssd_conv_b, ssd_a_log, ssd_dt_bias, ssd_d,
           ssd_norm_w, ffn_up, ffn_conv_w, ffn_conv_b, ffn_down):
    dec_batch = c.shape[0]
    consts = _scan_constants() + (_level_signs(),)

    cond8 = jnp.concatenate([c_ctx[None], c, jnp.zeros((8 - 1 - dec_batch, D_MODEL), F32)], axis=0)
    mod_all = _mod_call(cond8, w_ada, b_ada)
    mod_ctx = [mod_all[l, 0:1].reshape(1, 1, 6 * D_MODEL) for l in range(DEPTH)]
    mod_lat = [mod_all[l, 1:1 + dec_batch].reshape(dec_batch, 1, 6 * D_MODEL) for l in range(DEPTH)]

    layers = []
    for l in range(DEPTH):
        a_rows, b_rows, d_rows = _ssd_param_rows(ssd_a_log[l], ssd_dt_bias[l], ssd_d[l])
        layers.append(dict(
            norm_w1=norm_w[l, 0].reshape(1, D_MODEL), norm_w2=norm_w[l, 1].reshape(1, D_MODEL),
            w_in_hgrn=w_in[l, :, :5 * HGRN_KW].reshape(D_MODEL, 5, HGRN_HEADS, LANES)
            .transpose(2, 0, 1, 3).reshape(HGRN_HEADS, D_MODEL, 5 * LANES).astype(BF16),
            w_in_ssd=w_in[l, :, 5 * HGRN_KW:MAIN_COLS].astype(BF16),
            w_in_dt=_permute_dt_weight(w_in[l, :, MAIN_COLS:]).astype(BF16),
            w_out=w_out[l].astype(BF16),
            hgrn_norm_w=hgrn_norm_w[l].reshape(1, HGRN_DV),
            ssd=dict(conv_w=ssd_conv_w[l].reshape(9, CONV_CH), conv_b=ssd_conv_b[l].reshape(1, CONV_CH),
                     a_log_rows=a_rows, dt_bias_rows=b_rows, d_rows=d_rows,
                     norm_w=ssd_norm_w[l].reshape(1, SSD_W)),
            ffn_up=ffn_up[l].astype(BF16),
            ffn_conv_w=ffn_conv_w[l].reshape(9, 2 * D_FF), ffn_conv_b=ffn_conv_b[l].reshape(1, 2 * D_FF),
            ffn_down=ffn_down[l].astype(BF16),
        ))
    fnw = final_norm_w.reshape(1, D_MODEL)

    y_prompt, new_h, new_s = _run_pass(x_prompt, mod_ctx, None, None, layers, hgrn_lb, fnw, consts,
                                       width=x_prompt.shape[1], rows=1, want_state=True)
    y_sample, _, _ = _run_pass(x_sample, mod_lat, state_hgrn, state_ssd, layers, hgrn_lb, fnw, consts,
                               width=GRID_W, rows=x_sample.shape[1] // GRID_W, want_state=False)
    return (y_prompt, y_sample, jnp.stack(new_h, axis=1), jnp.stack(new_s, axis=1))
```

```python
import functools
import math

import numpy as np
import jax
import jax.numpy as jnp
from jax import lax
from jax.experimental import pallas as pl
from jax.experimental.pallas import tpu as pltpu

F32 = jnp.float32
BF16 = jnp.bfloat16

D_MODEL = 1024
DEPTH = 2
GRID_W = 64
HGRN_HEADS = 8
HGRN_DK = 128
HGRN_DV = 128
HGRN_KW = HGRN_HEADS * HGRN_DK
HGRN_W = HGRN_HEADS * HGRN_DV
SSD_W = 1024
SSD_HEADDIM = 64
SSD_HEADS = 16
SSD_GROUPS = 4
SSD_STATE = 128
HEADS_PER_GROUP = SSD_HEADS // SSD_GROUPS
GROUP_W = SSD_W // SSD_GROUPS
MIX_W = HGRN_W + SSD_W
CONV_CH = SSD_W + 2 * SSD_GROUPS * SSD_STATE
D_FF = 2816
MAIN_COLS = 3 * HGRN_KW + 2 * HGRN_W + SSD_W + CONV_CH
DT_COLS = 2 * SSD_HEADS
EPS = 1e-6
LOG2E = math.log2(math.e)

LANES = 128
CHUNK = 128
N_LEVELS = 7
VMEM_LIMIT = 56 * 1024 * 1024


def _cparams(sem):
    return pltpu.CompilerParams(dimension_semantics=sem, vmem_limit_bytes=VMEM_LIMIT)


def _sigmoid(x):
    return 1.0 / (1.0 + jnp.exp(-x))


def _silu(x):
    return x * _sigmoid(x)


def _softplus(x):
    return jnp.maximum(x, 0.0) + jnp.log1p(jnp.exp(-jnp.abs(x)))


def _rms(x):
    return x * lax.rsqrt(jnp.mean(x * x, axis=-1, keepdims=True) + EPS)


def _split3(x):
    hi = x.astype(BF16)
    r = x - hi.astype(F32)
    mid = r.astype(BF16)
    lo = (r - mid.astype(F32)).astype(BF16)
    return hi, mid, lo


def _sum_rows_exact(w01, x):
    n = x.shape[1]
    hi, mid, lo = _split3(x)
    p = jnp.dot(w01, jnp.concatenate([hi, mid, lo], axis=1), preferred_element_type=F32)
    return p[:, :n] + p[:, n:2 * n] + p[:, 2 * n:]


def _sum_cols_exact(x, w01):
    m = x.shape[0]
    hi, mid, lo = _split3(x)
    p = jnp.dot(jnp.concatenate([hi, mid, lo], axis=0), w01, preferred_element_type=F32)
    return p[:m] + p[m:2 * m] + p[2 * m:]


def _dot_nt(a, b):
    return lax.dot_general(a, b, (((1,), (1,)), ((), ())), preferred_element_type=F32)


def _dot_tn(a, b):
    return lax.dot_general(a, b, (((0,), (0,)), ((), ())), preferred_element_type=F32)


def _dwconv(x, w_ref, b_ref, width, rows):
    seq, _ = x.shape
    t = lax.broadcasted_iota(jnp.int32, x.shape, 0)
    col = jnp.bitwise_and(t, width - 1)
    xl = jnp.where(col == 0, 0.0, pltpu.roll(x, 1, axis=0))
    xr = jnp.where(col == width - 1, 0.0, pltpu.roll(x, seq - 1, axis=0))

    def hrow(kh, a, b, c):
        return w_ref[3 * kh:3 * kh + 1, :] * a + w_ref[3 * kh + 1:3 * kh + 2, :] * b \
            + w_ref[3 * kh + 2:3 * kh + 3, :] * c

    out = b_ref[...] + hrow(1, xl, x, xr)
    if rows > 1:
        up = hrow(0, pltpu.roll(xl, width, axis=0), pltpu.roll(x, width, axis=0),
                  pltpu.roll(xr, width, axis=0))
        out = out + jnp.where(t >= width, up, 0.0)
        dn = hrow(2, pltpu.roll(xl, seq - width, axis=0), pltpu.roll(x, seq - width, axis=0),
                  pltpu.roll(xr, seq - width, axis=0))
        out = out + jnp.where(t < seq - width, dn, 0.0)
    return out


def _scan_constants():
    c = CHUNK
    t = np.arange(c)[:, None]
    i = np.arange(c)[None, :]
    x = t ^ i
    lv = np.where(x == 0, 0, np.floor(np.log2(np.maximum(x, 1))).astype(np.int64) + 1)
    lv = np.where(i > t, -lv, lv)
    return jnp.asarray(i <= t, BF16), jnp.asarray(i >= t, BF16), jnp.asarray(lv, jnp.int32)


def _level_signs():
    t = np.arange(CHUNK)[:, None]
    sg = [np.where(((t >> lev) & 1) == 1, 1.0, -1.0) * np.ones((1, LANES)) for lev in range(N_LEVELS)]
    return jnp.asarray(np.concatenate(sg, 0), F32)


def _mod_kernel(cond_ref, w_ref, b_ref, o_ref):
    a = _silu(cond_ref[...]).astype(BF16)
    o_ref[...] = jnp.dot(a, w_ref[...].astype(BF16), preferred_element_type=F32) + b_ref[...]


def _mod_call(cond8, w_ada, b_ada):
    n = 6 * D_MODEL
    tn = D_MODEL
    return pl.pallas_call(
        _mod_kernel,
        grid=(DEPTH, n // tn),
        in_specs=[
            pl.BlockSpec((8, D_MODEL), lambda l, j: (0, 0)),
            pl.BlockSpec((None, D_MODEL, tn), lambda l, j: (l, 0, j)),
            pl.BlockSpec((None, 1, tn), lambda l, j: (l, 0, j)),
        ],
        out_specs=pl.BlockSpec((None, 8, tn), lambda l, j: (l, 0, j)),
        out_shape=jax.ShapeDtypeStruct((DEPTH, 8, n), F32),
        compiler_params=_cparams(("arbitrary", "arbitrary")),
        name="mod",
    )(cond8, w_ada, b_ada.reshape(DEPTH, 1, n))


def _norm_kernel(x_ref, sh_ref, sc_ref, nw_ref, hn_ref):
    h = _rms(x_ref[...]) * nw_ref[...] * (1.0 + sc_ref[...]) + sh_ref[...]
    hn_ref[...] = h.astype(BF16)


def _norm_call(x2, mod3, nw, seq_len):
    m = x2.shape[0]
    tm = 512
    bc = mod3.shape[0]
    seq_of = (lambda i: (i * tm) // seq_len) if bc > 1 else (lambda i: 0)
    return pl.pallas_call(
        _norm_kernel,
        grid=(m // tm,),
        in_specs=[
            pl.BlockSpec((tm, D_MODEL), lambda i: (i, 0)),
            pl.BlockSpec((None, 1, D_MODEL), lambda i: (seq_of(i), 0, 0)),
            pl.BlockSpec((None, 1, D_MODEL), lambda i: (seq_of(i), 0, 1)),
            pl.BlockSpec((1, D_MODEL), lambda i: (0, 0)),
        ],
        out_specs=pl.BlockSpec((tm, D_MODEL), lambda i: (i, 0)),
        out_shape=jax.ShapeDtypeStruct((m, D_MODEL), BF16),
        compiler_params=_cparams(("arbitrary",)),
        name="norm",
    )(x2, mod3, mod3, nw)


def _hgrn_gates(f_pre, lbd):
    ea = jnp.exp(-jnp.abs(f_pre))
    log2_sig = jnp.minimum(f_pre, 0.0) * LOG2E - jnp.log2(1.0 + ea)
    sig_neg = jnp.where(f_pre >= 0.0, ea, 1.0) / (1.0 + ea)
    if lbd is None:
        return log2_sig, sig_neg
    a = jnp.log2(lbd)
    b = jnp.log2(1.0 - lbd) + log2_sig
    log2_f = jnp.maximum(a, b) + jnp.log2(1.0 + jnp.exp2(-jnp.abs(a - b)))
    return log2_f, (1.0 - lbd) * sig_neg


def _block_mid(cum, scr, m):
    c = CHUNK
    if m >= 4:
        return jnp.concatenate(
            [jnp.broadcast_to(scr[r0 + m - 1:r0 + m, :], (2 * m, scr.shape[1])) for r0 in range(0, c, 2 * m)],
            axis=0)
    row = lax.broadcasted_iota(jnp.int32, cum.shape, 0)
    if m == 1:
        return jnp.where(jnp.bitwise_and(row, 1) == 1, pltpu.roll(cum, 1, axis=0), cum)
    r = jnp.bitwise_and(row, 3)
    return jnp.where(r == 0, pltpu.roll(cum, c - 1, axis=0),
                     jnp.where(r == 1, cum,
                               jnp.where(r == 2, pltpu.roll(cum, 1, axis=0), pltpu.roll(cum, 2, axis=0))))


def _pick_halves(up_val, low_val, m):
    pieces = []
    for r0 in range(0, CHUNK, 2 * m):
        pieces += [low_val[r0:r0 + m], up_val[r0 + m:r0 + 2 * m]]
    return jnp.concatenate(pieces, axis=0)


def _hgrn_kernel(*refs, layer, seq_len, seqs, has_s0, has_acc, want_state):
    it = iter(refs)
    hn_ref, w_ref, lb_ref, nw_ref, tl_ref, lv_ref, sg_ref = (next(it) for _ in range(7))
    s0_ref = next(it) if has_s0 else None
    if has_acc:
        next(it)
    o_ref = next(it)
    sf_ref = next(it) if want_state else None
    (p_scr, acc_scr, qc_scr, u_scr, dl_scr, st_scr, cumf_scr, cumb_scr) = (next(it) for _ in range(8))

    c = CHUNK
    dk = HGRN_DK
    n_chunks = seqs * seq_len // c
    cps = seq_len // c

    p_scr[...] = jnp.dot(hn_ref[...], w_ref[...], preferred_element_type=F32)

    lb = None
    if layer > 0:
        lbr = lb_ref[...]
        e = jnp.exp(lbr - jnp.max(lbr, axis=0, keepdims=True))
        sm = e / jnp.sum(e, axis=0, keepdims=True)
        lb = sm[1]
        for i in range(2, layer + 1):
            lb = lb + sm[i]

    lvs = lv_ref[...]
    lv = jnp.abs(lvs)
    tl = tl_ref[...]
    zero = jnp.zeros((c, dk), F32)

    def intra(ci, carry):
        r0 = pl.multiple_of(ci * c, c)
        sl = pl.ds(r0, c)
        q = _silu(p_scr[sl, 0:dk]) * (dk ** -0.5)
        lf_f, kf = _hgrn_gates(p_scr[sl, dk:2 * dk], None if lb is None else lb[0:1, :])
        lf_b, kb = _hgrn_gates(p_scr[sl, 2 * dk:3 * dk], None if lb is None else lb[1:2, :])
        vb = p_scr[sl, 3 * dk:4 * dk].astype(BF16)

        cum_f = _sum_rows_exact(tl, lf_f)
        cum_b = _sum_rows_exact(tl, lf_b)
        cx_b = cum_b - lf_b
        cumf_scr[...] = cum_f
        cumb_scr[...] = cum_b
        tot_f = cum_f[c - 1:c, :]
        tot_b = cum_b[c - 1:c, :]

        a_sum = _dot_nt(q.astype(BF16), (kf + kb).astype(BF16))
        for lev in range(1, N_LEVELS + 1):
            m = 1 << (lev - 1)
            sgn = sg_ref[(lev - 1) * c:lev * c, :]
            ef = jnp.exp2((cum_f - _block_mid(cum_f, cumf_scr, m)) * sgn)
            eb = jnp.exp2((cx_b - _block_mid(cum_b, cumb_scr, m)) * sgn)
            if m >= 8:
                qe = q * _pick_halves(ef, eb, m)
                ke = _pick_halves(kb, kf, m) * _pick_halves(eb, ef, m)
                lhs = jnp.concatenate([_pick_halves(qe, zero, m), _pick_halves(zero, qe, m)], axis=1)
                rhs = jnp.concatenate([_pick_halves(zero, ke, m), _pick_halves(ke, zero, m)], axis=1)
                a_sum = jnp.where(lv == lev, _dot_nt(lhs.astype(BF16), rhs.astype(BF16)), a_sum)
            else:
                p_f = _dot_nt((q * ef).astype(BF16), (kf * ef).astype(BF16))
                p_b = _dot_nt((q * eb).astype(BF16), (kb * eb).astype(BF16))
                a_sum = jnp.where(lvs == lev, p_f, jnp.where(lvs == -lev, p_b, a_sum))
        acc_scr[sl, :] = jnp.dot(a_sum.astype(BF16), vb, preferred_element_type=F32)

        qc_scr[sl, :] = jnp.concatenate(
            [q * jnp.exp2(cum_f), q * jnp.exp2(jnp.minimum(tot_b - cx_b, 0.0))], axis=1).astype(BF16)
        u_scr[ci, 0] = _dot_tn(vb, (kf * jnp.exp2(jnp.minimum(tot_f - cum_f, 0.0))).astype(BF16))
        u_scr[ci, 1] = _dot_tn(vb, (kb * jnp.exp2(jnp.minimum(cx_b, 0.0))).astype(BF16))
        dl_scr[ci, 0:1, :] = jnp.exp2(tot_f)
        dl_scr[ci, 1:2, :] = jnp.exp2(tot_b)
        return carry

    lax.fori_loop(0, n_chunks, intra, 0, unroll=2)

    def init(s, d):
        return s0_ref[s, d].T if has_s0 else jnp.zeros((HGRN_DV, dk), F32)

    st = None
    for ci in range(n_chunks):
        s = ci // cps
        if ci % cps == 0:
            st = init(s, 0)
        st_scr[ci, :, 0:dk] = st.astype(BF16)
        st = st * dl_scr[ci, 0:1, :] + u_scr[ci, 0]
        if want_state and ci % cps == cps - 1:
            sf_ref[s, 0] = st.T
    for ci in reversed(range(n_chunks)):
        s = ci // cps
        if ci % cps == cps - 1:
            st = init(s, 1)
        st_scr[ci, :, dk:2 * dk] = st.astype(BF16)
        st = st * dl_scr[ci, 1:2, :] + u_scr[ci, 1]
        if want_state and ci % cps == 0:
            sf_ref[s, 1] = st.T

    nw = nw_ref[...]

    for ci in range(n_chunks):
        sl = slice(ci * c, (ci + 1) * c)
        o = acc_scr[sl, :] + _dot_nt(qc_scr[sl, :], st_scr[ci])
        o_ref[sl, :] = (_rms(o) * nw * _silu(p_scr[sl, 4 * dk:5 * dk])).astype(o_ref.dtype)


def _hgrn_call(hn, w_heads, lb_raw, nw, consts, s0, state_acc, layer, batch, seq_len, want_state):
    tl, _, lv, sg = consts
    has_s0 = s0 is not None
    m = hn.shape[0]
    tokens = 1024
    seqs = tokens // seq_len
    n_chunks = tokens // CHUNK
    full = lambda arr: pl.BlockSpec(arr.shape, lambda h, b: (0,) * arr.ndim)
    state_spec = pl.BlockSpec((seqs, None, 2, None, HGRN_DK, HGRN_DV), lambda h, b: (b, layer, 0, h, 0, 0))
    in_specs = [pl.BlockSpec((tokens, D_MODEL), lambda h, b: (b, 0)),
                pl.BlockSpec((None, D_MODEL, 5 * LANES), lambda h, b: (h, 0, 0)),
                pl.BlockSpec((DEPTH, 2, LANES), lambda h, b: (0, 0, h)),
                pl.BlockSpec((1, HGRN_DV), lambda h, b: (0, 0)),
                full(tl), full(lv), full(sg)]
    args = [hn, w_heads, lb_raw, nw, tl, lv, sg]
    if has_s0:
        in_specs.append(state_spec)
        args.append(s0)
    out_specs = [pl.BlockSpec((tokens, LANES), lambda h, b: (b, h))]
    out_shape = [jax.ShapeDtypeStruct((m, HGRN_W), BF16)]
    aliases = {}
    if want_state:
        out_specs.append(state_spec)
        out_shape.append(jax.ShapeDtypeStruct((batch, DEPTH, 2, HGRN_HEADS, HGRN_DK, HGRN_DV), F32))
        if state_acc is not None:
            aliases = {len(args): 1}
            in_specs.append(pl.BlockSpec(memory_space=pl.ANY))
            args.append(state_acc)
    scratch = [pltpu.VMEM((tokens, 5 * LANES), F32),
               pltpu.VMEM((tokens, HGRN_DV), F32),
               pltpu.VMEM((tokens, 2 * HGRN_DK), BF16),
               pltpu.VMEM((n_chunks, 2, HGRN_DV, HGRN_DK), F32),
               pltpu.VMEM((n_chunks, 8, HGRN_DK), F32),
               pltpu.VMEM((n_chunks, HGRN_DV, 2 * HGRN_DK), BF16),
               pltpu.VMEM((CHUNK, HGRN_DK), F32),
               pltpu.VMEM((CHUNK, HGRN_DK), F32)]
    res = pl.pallas_call(
        functools.partial(_hgrn_kernel, layer=layer, seq_len=seq_len, seqs=seqs, has_s0=has_s0,
                          has_acc=bool(aliases), want_state=want_state),
        grid=(HGRN_HEADS, m // tokens),
        in_specs=in_specs, out_specs=out_specs, out_shape=out_shape,
        scratch_shapes=scratch,
        input_output_aliases=aliases,
        compiler_params=_cparams(("arbitrary", "arbitrary")),
        name="hgrn",
    )(*args)
    return (res[0], res[1]) if want_state else (res[0], None)


def _ssd_kernel(*refs, seq_len, seqs, width, rows, has_s0, has_acc, want_state):
    it = iter(refs)
    (hn_ref, w_ref, cwx_ref, cwb_ref, cwc_ref, cbx_ref, cbb_ref, cbc_ref,
     alog_ref, acol_ref, dtb_ref, dsk_ref, nw_ref, tl_ref, tu_ref) = (next(it) for _ in range(15))
    s0_ref = next(it) if has_s0 else None
    if has_acc:
        next(it)
    o_ref = next(it)
    sf_ref = next(it) if want_state else None
    (p_scr, xc_scr, bc_scr, cc_scr, dtg_scr, acc_scr, ecum_scr, u_scr, dl_scr, st_scr) = (
        next(it) for _ in range(10))

    q = CHUNK
    n_chunks = seqs * seq_len // q
    cps = seq_len // q
    hp = HEADS_PER_GROUP
    p = SSD_HEADDIM
    gw = GROUP_W
    n = SSD_STATE

    p_scr[...] = jnp.dot(hn_ref[...], w_ref[...], preferred_element_type=F32)

    xc_scr[...] = _silu(_dwconv(p_scr[:, gw:2 * gw], cwx_ref, cbx_ref, width, rows))
    bc_scr[...] = _silu(_dwconv(p_scr[:, 2 * gw:2 * gw + n], cwb_ref, cbb_ref, width, rows))
    cc_scr[...] = _silu(_dwconv(p_scr[:, 2 * gw + n:2 * gw + 2 * n], cwc_ref, cbc_ref, width, rows)).astype(BF16)
    dtg_scr[...] = _softplus(p_scr[:, 2 * gw + 2 * n:2 * gw + 2 * n + LANES] + dtb_ref[...])
    neg_a = -jnp.exp(alog_ref[...])
    neg_a_t = -jnp.exp(acol_ref[...])

    tl = tl_ref[...]
    tu = tu_ref[...]
    ti = lax.broadcasted_iota(jnp.int32, (q, q), 0)
    si = lax.broadcasted_iota(jnp.int32, (q, q), 1)
    lower = si <= ti
    upper = si >= ti

    def intra(ci, carry):
        r0 = pl.multiple_of(ci * q, q)
        sl = pl.ds(r0, q)
        dt = dtg_scr[sl, :]
        da = dt * neg_a
        cum_f = _sum_rows_exact(tl, da)
        cum_b = _sum_rows_exact(tu, da)
        dt_t = dt.T[0:2 * hp, :]
        da_t = dt_t * neg_a_t
        row_f = _sum_cols_exact(da_t, tu)
        row_b = _sum_cols_exact(da_t, tl)
        wr_f = jnp.exp(row_f[:, q - 1:q] - row_f) * dt_t
        wr_b = jnp.exp(row_b[:, 0:1] - row_b) * dt_t
        xb = xc_scr[sl, :].astype(BF16)
        bm = bc_scr[sl, :]
        bt = bm.T
        gmat = _dot_nt(cc_scr[sl, :], bm.astype(BF16))
        ecum_scr[sl, 0:LANES] = jnp.exp(cum_f)
        ecum_scr[sl, LANES:2 * LANES] = jnp.exp(cum_b)
        ys, u_f, u_b = [], [], []
        for j in range(hp):
            jf, jb = j, hp + j
            dec_f = jnp.exp(jnp.minimum(cum_f[:, jf:jf + 1] - row_f[jf:jf + 1, :], 0.0)) * dt_t[jf:jf + 1, :]
            dec_b = jnp.exp(jnp.minimum(cum_b[:, jb:jb + 1] - row_b[jb:jb + 1, :], 0.0)) * dt_t[jb:jb + 1, :]
            mh = gmat * (jnp.where(lower, dec_f, 0.0) + jnp.where(upper, dec_b, 0.0))
            xh = xb[:, j * p:(j + 1) * p]
            ys.append(jnp.dot(mh.astype(BF16), xh, preferred_element_type=F32))
            u_f.append(jnp.dot((bt * wr_f[jf:jf + 1, :]).astype(BF16), xh, preferred_element_type=F32))
            u_b.append(jnp.dot((bt * wr_b[jb:jb + 1, :]).astype(BF16), xh, preferred_element_type=F32))
        acc_scr[sl, :] = jnp.concatenate(ys, axis=1)
        u_scr[ci, 0] = jnp.concatenate(u_f, axis=1)
        u_scr[ci, 1] = jnp.concatenate(u_b, axis=1)
        dl_scr[ci, 0:1, :] = jnp.exp(cum_f[q - 1:q, :])
        dl_scr[ci, 1:2, :] = jnp.exp(cum_b[0:1, :])
        return carry

    lax.fori_loop(0, n_chunks, intra, 0)

    def head_scale(vec, lane0):
        return jnp.concatenate(
            [jnp.broadcast_to(vec[0:1, lane0 + j:lane0 + j + 1], (1, p)) for j in range(hp)], axis=1)

    def init(s, d):
        return s0_ref[s, d].reshape(hp * p, n).T if has_s0 else jnp.zeros((n, hp * p), F32)

    st = None
    for ci in range(n_chunks):
        s = ci // cps
        if ci % cps == 0:
            st = init(s, 0)
        st_scr[ci, :, 0:hp * p] = st.astype(BF16)
        st = st * head_scale(dl_scr[ci, 0:1, :], 0) + u_scr[ci, 0]
        if want_state and ci % cps == cps - 1:
            sf_ref[s, 0] = st.T.reshape(hp, p, n)
    for ci in reversed(range(n_chunks)):
        s = ci // cps
        if ci % cps == cps - 1:
            st = init(s, 1)
        st_scr[ci, :, hp * p:2 * hp * p] = st.astype(BF16)
        st = st * head_scale(dl_scr[ci, 1:2, :], hp) + u_scr[ci, 1]
        if want_state and ci % cps == 0:
            sf_ref[s, 1] = st.T.reshape(hp, p, n)

    dsk = dsk_ref[...]
    nw = nw_ref[...]
    for ci in range(n_chunks):
        sl = slice(ci * q, (ci + 1) * q)
        yi = jnp.dot(cc_scr[sl, :], st_scr[ci], preferred_element_type=F32)
        ec = ecum_scr[sl, :]
        parts = []
        for j in range(hp):
            parts.append(yi[:, j * p:(j + 1) * p] * ec[:, j:j + 1]
                         + yi[:, (hp + j) * p:(hp + j + 1) * p] * ec[:, LANES + hp + j:LANES + hp + j + 1])
        y = acc_scr[sl, :] + jnp.concatenate(parts, axis=1) + dsk * xc_scr[sl, :]
        y = y * _silu(p_scr[sl, 0:gw])
        o_ref[sl, :] = (_rms(y) * nw).astype(o_ref.dtype)


def _ssd_call(hn, w_groups, p, consts, s0, state_acc, layer, batch, seq_len, width, rows, want_state):
    tl, tu = consts
    has_s0 = s0 is not None
    m = hn.shape[0]
    tokens = 1024
    seqs = tokens // seq_len
    assert seqs == 1 or rows == 1
    n_chunks = tokens // CHUNK
    wcols = w_groups.shape[2]
    xb = SSD_W // LANES
    full = lambda arr: pl.BlockSpec(arr.shape, lambda g, b: (0,) * arr.ndim)
    in_specs = [
        pl.BlockSpec((tokens, D_MODEL), lambda g, b: (b, 0)),
        pl.BlockSpec((None, D_MODEL, wcols), lambda g, b: (g, 0, 0)),
        pl.BlockSpec((9, GROUP_W), lambda g, b: (0, g)),
        pl.BlockSpec((9, LANES), lambda g, b: (0, xb + g)),
        pl.BlockSpec((9, LANES), lambda g, b: (0, xb + SSD_GROUPS + g)),
        pl.BlockSpec((1, GROUP_W), lambda g, b: (0, g)),
        pl.BlockSpec((1, LANES), lambda g, b: (0, xb + g)),
        pl.BlockSpec((1, LANES), lambda g, b: (0, xb + SSD_GROUPS + g)),
        pl.BlockSpec((None, 1, LANES), lambda g, b: (g, 0, 0)),
        pl.BlockSpec((None, 2 * HEADS_PER_GROUP, LANES), lambda g, b: (g, 0, 0)),
        pl.BlockSpec((None, 1, LANES), lambda g, b: (g, 0, 0)),
        pl.BlockSpec((1, GROUP_W), lambda g, b: (0, g)),
        pl.BlockSpec((1, GROUP_W), lambda g, b: (0, g)),
        full(tl), full(tu),
    ]
    args = [hn, w_groups, p['conv_w'], p['conv_w'], p['conv_w'], p['conv_b'], p['conv_b'], p['conv_b'],
            p['a_log_rows'], p['a_log_cols'], p['dt_bias_rows'], p['d_rows'], p['norm_w'], tl, tu]
    state_spec = pl.BlockSpec((seqs, None, 2, HEADS_PER_GROUP, SSD_HEADDIM, SSD_STATE),
                              lambda g, b: (b, layer, 0, g, 0, 0))
    if has_s0:
        in_specs.append(state_spec)
        args.append(s0)
    out_specs = [pl.BlockSpec((tokens, GROUP_W), lambda g, b: (b, g))]
    out_shape = [jax.ShapeDtypeStruct((m, SSD_W), BF16)]
    aliases = {}
    if want_state:
        out_specs.append(state_spec)
        out_shape.append(jax.ShapeDtypeStruct((batch, DEPTH, 2, SSD_HEADS, SSD_HEADDIM, SSD_STATE), F32))
        if state_acc is not None:
            aliases = {len(args): 1}
            in_specs.append(pl.BlockSpec(memory_space=pl.ANY))
            args.append(state_acc)
    hpp = HEADS_PER_GROUP * SSD_HEADDIM
    scratch = [pltpu.VMEM((tokens, wcols), F32),
               pltpu.VMEM((tokens, GROUP_W), F32),
               pltpu.VMEM((tokens, SSD_STATE), F32),
               pltpu.VMEM((tokens, SSD_STATE), BF16),
               pltpu.VMEM((tokens, LANES), F32),
               pltpu.VMEM((tokens, GROUP_W), F32),
               pltpu.VMEM((tokens, 2 * LANES), F32),
               pltpu.VMEM((n_chunks, 2, SSD_STATE, hpp), F32),
               pltpu.VMEM((n_chunks, 8, LANES), F32),
               pltpu.VMEM((n_chunks, SSD_STATE, 2 * hpp), BF16)]
    res = pl.pallas_call(
        functools.partial(_ssd_kernel, seq_len=seq_len, seqs=seqs, width=width, rows=rows,
                          has_s0=has_s0, has_acc=bool(aliases), want_state=want_state),
        grid=(SSD_GROUPS, m // tokens),
        in_specs=in_specs, out_specs=out_specs, out_shape=out_shape,
        scratch_shapes=scratch,
        input_output_aliases=aliases,
        compiler_params=_cparams(("arbitrary", "arbitrary")),
        name="ssd",
    )(*args)
    return (res[0], res[1]) if want_state else (res[0], None)


def _outup_kernel(x_ref, oh_ref, os_ref, g1_ref, sh_ref, sc_ref, nw_ref, wo_ref, wu_ref,
                  x1_ref, u_ref, h_scr):
    @pl.when(pl.program_id(1) == 0)
    def _():
        mix = jnp.dot(oh_ref[...], wo_ref[0:HGRN_W, :], preferred_element_type=F32)
        mix = mix + jnp.dot(os_ref[...], wo_ref[HGRN_W:MIX_W, :], preferred_element_type=F32)
        x1 = x_ref[...] + g1_ref[...] * mix
        x1_ref[...] = x1
        h = _rms(x1) * nw_ref[...] * (1.0 + sc_ref[...]) + sh_ref[...]
        h_scr[...] = h.astype(BF16)

    u_ref[...] = jnp.dot(h_scr[...], wu_ref[...], preferred_element_type=F32).astype(u_ref.dtype)


def _outup_call(x2, o_h, o_s, mod3, nw, w_out, w_up, seq_len):
    m = x2.shape[0]
    tm, tn = 1024, 512
    bc = mod3.shape[0]
    seq_of = (lambda i: (i * tm) // seq_len) if bc > 1 else (lambda i: 0)
    modspec = lambda part: pl.BlockSpec((None, 1, D_MODEL), lambda i, j, part=part: (seq_of(i), 0, part))
    return pl.pallas_call(
        _outup_kernel,
        grid=(m // tm, 2 * D_FF // tn),
        in_specs=[
            pl.BlockSpec((tm, D_MODEL), lambda i, j: (i, 0)),
            pl.BlockSpec((tm, HGRN_W), lambda i, j: (i, 0)),
            pl.BlockSpec((tm, SSD_W), lambda i, j: (i, 0)),
            modspec(2), modspec(3), modspec(4),
            pl.BlockSpec((1, D_MODEL), lambda i, j: (0, 0)),
            pl.BlockSpec((MIX_W, D_MODEL), lambda i, j: (0, 0)),
            pl.BlockSpec((D_MODEL, tn), lambda i, j: (0, j)),
        ],
        out_specs=[
            pl.BlockSpec((tm, D_MODEL), lambda i, j: (i, 0)),
            pl.BlockSpec((tm, tn), lambda i, j: (i, j)),
        ],
        out_shape=[jax.ShapeDtypeStruct((m, D_MODEL), F32),
                   jax.ShapeDtypeStruct((m, 2 * D_FF), BF16)],
        scratch_shapes=[pltpu.VMEM((tm, D_MODEL), BF16)],
        compiler_params=_cparams(("arbitrary", "arbitrary")),
        name="outup",
    )(x2, o_h, o_s, mod3, mod3, mod3, nw, w_out, w_up)


def _ffndown_kernel(x1_ref, ug_ref, uv_ref, g2_ref, cwg_ref, cwv_ref, cbg_ref, cbv_ref, wd_ref, fw_ref,
                    o_ref, acc_scr, *, width, rows, final):
    k = pl.program_id(1)

    @pl.when(k == 0)
    def _():
        acc_scr[...] = jnp.zeros_like(acc_scr)

    gate = _dwconv(ug_ref[...].astype(F32), cwg_ref, cbg_ref, width, rows)
    val = _dwconv(uv_ref[...].astype(F32), cwv_ref, cbv_ref, width, rows)
    act = (_silu(gate) * val).astype(BF16)
    acc_scr[...] += jnp.dot(act, wd_ref[...], preferred_element_type=F32)

    @pl.when(k == pl.num_programs(1) - 1)
    def _():
        x2 = x1_ref[...] + g2_ref[...] * acc_scr[...]
        if final:
            x2 = _rms(x2) * fw_ref[...]
        o_ref[...] = x2


def _ffndown_call(x1, u, mod3, conv_w, conv_b, w_down, final_w, batch, seq_len, width, rows, final):
    m = x1.shape[0]
    tk = 256
    nk = D_FF // tk
    bc = mod3.shape[0]
    seq_of = (lambda b: b) if bc > 1 else (lambda b: 0)
    return pl.pallas_call(
        functools.partial(_ffndown_kernel, width=width, rows=rows, final=final),
        grid=(batch, nk),
        in_specs=[
            pl.BlockSpec((seq_len, D_MODEL), lambda b, k: (b, 0)),
            pl.BlockSpec((seq_len, tk), lambda b, k: (b, k)),
            pl.BlockSpec((seq_len, tk), lambda b, k: (b, nk + k)),
            pl.BlockSpec((None, 1, D_MODEL), lambda b, k: (seq_of(b), 0, 5)),
            pl.BlockSpec((9, tk), lambda b, k: (0, k)),
            pl.BlockSpec((9, tk), lambda b, k: (0, nk + k)),
            pl.BlockSpec((1, tk), lambda b, k: (0, k)),
            pl.BlockSpec((1, tk), lambda b, k: (0, nk + k)),
            pl.BlockSpec((tk, D_MODEL), lambda b, k: (k, 0)),
            pl.BlockSpec((1, D_MODEL), lambda b, k: (0, 0)),
        ],
        out_specs=pl.BlockSpec((seq_len, D_MODEL), lambda b, k: (b, 0)),
        out_shape=jax.ShapeDtypeStruct((m, D_MODEL), F32),
        scratch_shapes=[pltpu.VMEM((seq_len, D_MODEL), F32)],
        compiler_params=_cparams(("arbitrary", "arbitrary")),
        name="ffndown",
    )(x1, u, u, mod3, conv_w, conv_w, conv_b, conv_b, w_down, final_w)


def _ssd_param_rows(a_log, dt_bias, d_skip):
    def per_group(v):
        return v.reshape(2, SSD_GROUPS, HEADS_PER_GROUP).transpose(1, 0, 2).reshape(SSD_GROUPS, 2 * HEADS_PER_GROUP)

    def rows(v):
        return jnp.pad(per_group(v), ((0, 0), (0, LANES - 2 * HEADS_PER_GROUP))).reshape(SSD_GROUPS, 1, LANES)
    a_cols = jnp.broadcast_to(per_group(a_log)[:, :, None], (SSD_GROUPS, 2 * HEADS_PER_GROUP, LANES))
    return rows(a_log), a_cols, rows(dt_bias), jnp.repeat(d_skip, SSD_HEADDIM).reshape(1, SSD_W)


def _group_ssd_weight(w_ssd, w_dt):
    g, gw, n = SSD_GROUPS, GROUP_W, SSD_STATE
    z = w_ssd[:, :SSD_W].reshape(D_MODEL, g, gw)
    x = w_ssd[:, SSD_W:2 * SSD_W].reshape(D_MODEL, g, gw)
    b = w_ssd[:, 2 * SSD_W:2 * SSD_W + g * n].reshape(D_MODEL, g, n)
    c = w_ssd[:, 2 * SSD_W + g * n:].reshape(D_MODEL, g, n)
    dt = w_dt.reshape(D_MODEL, 2, g, HEADS_PER_GROUP).transpose(0, 2, 1, 3).reshape(D_MODEL, g, 2 * HEADS_PER_GROUP)
    dt = jnp.pad(dt, ((0, 0), (0, 0), (0, LANES - 2 * HEADS_PER_GROUP)))
    return jnp.concatenate([z, x, b, c, dt], axis=2).transpose(1, 0, 2)


def _run_pass(x, mod_all_rows, s_h0, s_s0, layers, hgrn_lb, final_norm_w, consts, width, rows, want_state):
    batch, seq_len, _ = x.shape
    m = batch * seq_len
    x2 = x.reshape(m, D_MODEL)
    tl, tu = consts[0], consts[1]
    new_h, new_s = None, None
    for l, p in enumerate(layers):
        mod3 = mod_all_rows[l]
        hn = _norm_call(x2, mod3, p['norm_w1'], seq_len)
        o_h, new_h = _hgrn_call(hn, p['w_in_hgrn'], hgrn_lb, p['hgrn_norm_w'], consts,
                                s_h0, new_h, l, batch, seq_len, want_state)
        o_s, new_s = _ssd_call(hn, p['w_in_ssd'], p['ssd'], (tl, tu),
                               s_s0, new_s, l, batch, seq_len, width, rows, want_state)
        x1, u = _outup_call(x2, o_h, o_s, mod3, p['norm_w2'], p['w_out'], p['ffn_up'], seq_len)
        x2 = _ffndown_call(x1, u, mod3, p['ffn_conv_w'], p['ffn_conv_b'], p['ffn_down'], final_norm_w,
                           batch, seq_len, width, rows, final=(l == DEPTH - 1))
    return x2.reshape(batch, seq_len, D_MODEL), new_h, new_s


def kernel(x_prompt, x_sample, c, state_hgrn, state_ssd, c_ctx, norm_w, final_norm_w, w_ada, b_ada,
           w_in, w_out, hgrn_lb, hgrn_norm_w, ssd_conv_w, ssd_conv_b, ssd_a_log, ssd_dt_bias, ssd_d,
           ssd_norm_w, ffn_up, ffn_conv_w, ffn_conv_b, ffn_down):
    dec_batch = c.shape[0]
    consts = _scan_constants() + (_level_signs(),)

    cond8 = jnp.concatenate([c_ctx[None], c, jnp.zeros((8 - 1 - dec_batch, D_MODEL), F32)], axis=0)
    mod_all = _mod_call(cond8, w_ada, b_ada)
    mod_ctx = [mod_all[l, 0:1].reshape(1, 1, 6 * D_MODEL) for l in range(DEPTH)]
    mod_lat = [mod_all[l, 1:1 + dec_batch].reshape(dec_batch, 1, 6 * D_MODEL) for l in range(DEPTH)]

    layers = []
    for l in range(DEPTH):
        a_rows, a_cols, b_rows, d_rows = _ssd_param_rows(ssd_a_log[l], ssd_dt_bias[l], ssd_d[l])
        layers.append(dict(
            norm_w1=norm_w[l, 0].reshape(1, D_MODEL), norm_w2=norm_w[l, 1].reshape(1, D_MODEL),
            w_in_hgrn=w_in[l, :, :5 * HGRN_KW].reshape(D_MODEL, 5, HGRN_HEADS, LANES)
            .transpose(2, 0, 1, 3).reshape(HGRN_HEADS, D_MODEL, 5 * LANES).astype(BF16),
            w_in_ssd=_group_ssd_weight(w_in[l, :, 5 * HGRN_KW:MAIN_COLS], w_in[l, :, MAIN_COLS:]).astype(BF16),
            w_out=w_out[l].astype(BF16),
            hgrn_norm_w=hgrn_norm_w[l].reshape(1, HGRN_DV),
            ssd=dict(conv_w=ssd_conv_w[l].reshape(9, CONV_CH), conv_b=ssd_conv_b[l].reshape(1, CONV_CH),
                     a_log_rows=a_rows, a_log_cols=a_cols, dt_bias_rows=b_rows, d_rows=d_rows,
                     norm_w=ssd_norm_w[l].reshape(1, SSD_W)),
            ffn_up=ffn_up[l].astype(BF16),
            ffn_conv_w=ffn_conv_w[l].reshape(9, 2 * D_FF), ffn_conv_b=ffn_conv_b[l].reshape(1, 2 * D_FF),
            ffn_down=ffn_down[l].astype(BF16),
        ))
    fnw = final_norm_w.reshape(1, D_MODEL)

    y_prompt, new_h, new_s = _run_pass(x_prompt, mod_ctx, None, None, layers, hgrn_lb, fnw, consts,
                                       width=x_prompt.shape[1], rows=1, want_state=True)
    y_sample, _, _ = _run_pass(x_sample, mod_lat, state_hgrn, state_ssd, layers, hgrn_lb, fnw, consts,
                               width=GRID_W, rows=x_sample.shape[1] // GRID_W, want_state=False)
    return (y_prompt, y_sample, new_h, new_s)
```

```python
import functools
import math

import numpy as np
import jax
import jax.numpy as jnp
from jax import lax
from jax.experimental import pallas as pl
from jax.experimental.pallas import tpu as pltpu

F32 = jnp.float32
BF16 = jnp.bfloat16

D_MODEL = 1024
DEPTH = 2
GRID_W = 64
HGRN_HEADS = 8
HGRN_DK = 128
HGRN_DV = 128
HGRN_KW = HGRN_HEADS * HGRN_DK
HGRN_W = HGRN_HEADS * HGRN_DV
SSD_W = 1024
SSD_HEADDIM = 64
SSD_HEADS = 16
SSD_GROUPS = 4
SSD_STATE = 128
HEADS_PER_GROUP = SSD_HEADS // SSD_GROUPS
GROUP_W = SSD_W // SSD_GROUPS
MIX_W = HGRN_W + SSD_W
CONV_CH = SSD_W + 2 * SSD_GROUPS * SSD_STATE
D_FF = 2816
MAIN_COLS = 3 * HGRN_KW + 2 * HGRN_W + SSD_W + CONV_CH
DT_COLS = 2 * SSD_HEADS
EPS = 1e-6
LOG2E = math.log2(math.e)

LANES = 128
CHUNK = 128
N_LEVELS = 7
VMEM_LIMIT = 56 * 1024 * 1024


def _cparams(sem):
    return pltpu.CompilerParams(dimension_semantics=sem, vmem_limit_bytes=VMEM_LIMIT)


def _sigmoid(x):
    return 1.0 / (1.0 + jnp.exp(-x))


def _silu(x):
    return x * _sigmoid(x)


def _softplus(x):
    return jnp.maximum(x, 0.0) + jnp.log1p(jnp.exp(-jnp.abs(x)))


def _rms(x):
    return x * lax.rsqrt(jnp.mean(x * x, axis=-1, keepdims=True) + EPS)


def _split3(x):
    hi = x.astype(BF16)
    r = x - hi.astype(F32)
    mid = r.astype(BF16)
    lo = (r - mid.astype(F32)).astype(BF16)
    return hi, mid, lo


def _sum_rows_exact(w01, x):
    n = x.shape[1]
    hi, mid, lo = _split3(x)
    p = jnp.dot(w01, jnp.concatenate([hi, mid, lo], axis=1), preferred_element_type=F32)
    return p[:, :n] + p[:, n:2 * n] + p[:, 2 * n:]


def _sum_cols_exact(x, w01):
    m = x.shape[0]
    hi, mid, lo = _split3(x)
    p = jnp.dot(jnp.concatenate([hi, mid, lo], axis=0), w01, preferred_element_type=F32)
    return p[:m] + p[m:2 * m] + p[2 * m:]


def _dot_nt(a, b):
    return lax.dot_general(a, b, (((1,), (1,)), ((), ())), preferred_element_type=F32)


def _dot_tn(a, b):
    return lax.dot_general(a, b, (((0,), (0,)), ((), ())), preferred_element_type=F32)


def _dwconv(x, w_ref, b_ref, width, rows):
    seq, _ = x.shape
    t = lax.broadcasted_iota(jnp.int32, x.shape, 0)
    col = jnp.bitwise_and(t, width - 1)
    xl = jnp.where(col == 0, 0.0, pltpu.roll(x, 1, axis=0))
    xr = jnp.where(col == width - 1, 0.0, pltpu.roll(x, seq - 1, axis=0))

    def hrow(kh, a, b, c):
        return w_ref[3 * kh:3 * kh + 1, :] * a + w_ref[3 * kh + 1:3 * kh + 2, :] * b \
            + w_ref[3 * kh + 2:3 * kh + 3, :] * c

    out = b_ref[0:1, :] + hrow(1, xl, x, xr)
    if rows > 1:
        up = hrow(0, pltpu.roll(xl, width, axis=0), pltpu.roll(x, width, axis=0),
                  pltpu.roll(xr, width, axis=0))
        out = out + jnp.where(t >= width, up, 0.0)
        dn = hrow(2, pltpu.roll(xl, seq - width, axis=0), pltpu.roll(x, seq - width, axis=0),
                  pltpu.roll(xr, seq - width, axis=0))
        out = out + jnp.where(t < seq - width, dn, 0.0)
    return out


def _scan_constants():
    c = CHUNK
    t = np.arange(c)[:, None]
    i = np.arange(c)[None, :]
    x = t ^ i
    lv = np.where(x == 0, 0, np.floor(np.log2(np.maximum(x, 1))).astype(np.int64) + 1)
    lv = np.where(i > t, -lv, lv)
    return jnp.asarray(i <= t, BF16), jnp.asarray(i >= t, BF16), jnp.asarray(lv, jnp.int32)


def _level_signs():
    t = np.arange(CHUNK)[:, None]
    sg = [np.where(((t >> lev) & 1) == 1, 1.0, -1.0) * np.ones((1, LANES)) for lev in range(N_LEVELS)]
    return jnp.asarray(np.concatenate(sg, 0), F32)


def _mod_kernel(cond_ref, w_ref, b_ref, o_ref):
    a = _silu(cond_ref[...]).astype(BF16)
    o_ref[...] = jnp.dot(a, w_ref[...].astype(BF16), preferred_element_type=F32) + b_ref[...]


def _mod_call(cond8, w_ada, b_ada):
    n = 6 * D_MODEL
    tn = D_MODEL
    return pl.pallas_call(
        _mod_kernel,
        grid=(DEPTH, n // tn),
        in_specs=[
            pl.BlockSpec((8, D_MODEL), lambda l, j: (0, 0)),
            pl.BlockSpec((None, D_MODEL, tn), lambda l, j: (l, 0, j)),
            pl.BlockSpec((None, 1, tn), lambda l, j: (l, 0, j)),
        ],
        out_specs=pl.BlockSpec((None, 8, tn), lambda l, j: (l, 0, j)),
        out_shape=jax.ShapeDtypeStruct((DEPTH, 8, n), F32),
        compiler_params=_cparams(("arbitrary", "arbitrary")),
        name="mod",
    )(cond8, w_ada, b_ada.reshape(DEPTH, 1, n))


def _norm_kernel(x_ref, sh_ref, sc_ref, nw_ref, hn_ref):
    h = _rms(x_ref[...]) * nw_ref[...] * (1.0 + sc_ref[...]) + sh_ref[...]
    hn_ref[...] = h.astype(BF16)


def _norm_call(x2, mod3, nw, seq_len):
    m = x2.shape[0]
    tm = 512
    bc = mod3.shape[0]
    seq_of = (lambda i: (i * tm) // seq_len) if bc > 1 else (lambda i: 0)
    return pl.pallas_call(
        _norm_kernel,
        grid=(m // tm,),
        in_specs=[
            pl.BlockSpec((tm, D_MODEL), lambda i: (i, 0)),
            pl.BlockSpec((None, 1, D_MODEL), lambda i: (seq_of(i), 0, 0)),
            pl.BlockSpec((None, 1, D_MODEL), lambda i: (seq_of(i), 0, 1)),
            pl.BlockSpec((1, D_MODEL), lambda i: (0, 0)),
        ],
        out_specs=pl.BlockSpec((tm, D_MODEL), lambda i: (i, 0)),
        out_shape=jax.ShapeDtypeStruct((m, D_MODEL), BF16),
        compiler_params=_cparams(("arbitrary",)),
        name="norm",
    )(x2, mod3, mod3, nw)


def _hgrn_gates(f_pre, lbd):
    ea = jnp.exp(-jnp.abs(f_pre))
    log2_sig = jnp.minimum(f_pre, 0.0) * LOG2E - jnp.log2(1.0 + ea)
    sig_neg = jnp.where(f_pre >= 0.0, ea, 1.0) / (1.0 + ea)
    if lbd is None:
        return log2_sig, sig_neg
    a = jnp.log2(lbd)
    b = jnp.log2(1.0 - lbd) + log2_sig
    log2_f = jnp.maximum(a, b) + jnp.log2(1.0 + jnp.exp2(-jnp.abs(a - b)))
    return log2_f, (1.0 - lbd) * sig_neg


def _block_mid(scr, m):
    return jnp.concatenate(
        [jnp.broadcast_to(scr[r0 + m - 1:r0 + m, :], (2 * m, scr.shape[1])) for r0 in range(0, CHUNK, 2 * m)],
        axis=0)


def _mid_distance(c, scr, m):
    pieces = []
    for r0 in range(0, CHUNK, 2 * m):
        mid = jnp.broadcast_to(scr[r0 + m - 1:r0 + m, :], (m, scr.shape[1]))
        pieces += [mid - c[r0:r0 + m], c[r0 + m:r0 + 2 * m] - mid]
    return jnp.concatenate(pieces, axis=0)


def _pick_halves(up_val, low_val, m):
    pieces = []
    for r0 in range(0, CHUNK, 2 * m):
        pieces += [low_val[r0:r0 + m], up_val[r0 + m:r0 + 2 * m]]
    return jnp.concatenate(pieces, axis=0)


def _hgrn_kernel(*refs, layer, seq_len, seqs, has_s0, has_acc, want_state):
    it = iter(refs)
    hn_ref = next(it)
    w_refs = [next(it) for _ in range(5)]
    lb_ref, nw_ref, tl_ref, lv_ref, sg_ref = (next(it) for _ in range(5))
    s0_ref = next(it) if has_s0 else None
    if has_acc:
        next(it)
    o_ref = next(it)
    sf_ref = next(it) if want_state else None
    (w_scr, p_scr, acc_scr, qc_scr, u_scr, dl_scr, st_scr, cumf_scr, cumb_scr) = (next(it) for _ in range(9))

    c = CHUNK
    dk = HGRN_DK
    n_chunks = seqs * seq_len // c
    cps = seq_len // c

    @pl.when(pl.program_id(1) == 0)
    def _():
        for i, w_ref in enumerate(w_refs):
            w_scr[:, i * LANES:(i + 1) * LANES] = w_ref[...].astype(BF16)

    p_scr[...] = jnp.dot(hn_ref[...], w_scr[...], preferred_element_type=F32)

    lb = None
    if layer > 0:
        lbr = lb_ref[...]
        e = jnp.exp(lbr - jnp.max(lbr, axis=0, keepdims=True))
        sm = e / jnp.sum(e, axis=0, keepdims=True)
        lb = sm[1]
        for i in range(2, layer + 1):
            lb = lb + sm[i]

    lvs = lv_ref[...]
    lv = jnp.abs(lvs)
    tl = tl_ref[...]
    zero = jnp.zeros((c, dk), F32)
    odd = jnp.bitwise_and(lax.broadcasted_iota(jnp.int32, (c, dk), 0), 1) == 1

    def intra(ci, carry):
        r0 = pl.multiple_of(ci * c, c)
        sl = pl.ds(r0, c)
        q = _silu(p_scr[sl, 0:dk]) * (dk ** -0.5)
        lf_f, kf = _hgrn_gates(p_scr[sl, dk:2 * dk], None if lb is None else lb[0:1, :])
        lf_b, kb = _hgrn_gates(p_scr[sl, 2 * dk:3 * dk], None if lb is None else lb[1:2, :])
        vb = p_scr[sl, 3 * dk:4 * dk].astype(BF16)

        cum_f = _sum_rows_exact(tl, lf_f)
        cum_b = _sum_rows_exact(tl, lf_b)
        cx_b = cum_b - lf_b
        cumf_scr[...] = cum_f
        cumb_scr[...] = cum_b
        tot_f = cum_f[c - 1:c, :]
        tot_b = cum_b[c - 1:c, :]

        a_sum = _dot_nt(q.astype(BF16), (kf + kb).astype(BF16))

        f_f = jnp.exp2(lf_f)
        f_b = jnp.exp2(lf_b)
        qf = q * f_f
        qb = q * f_b
        p_f = _dot_nt(qf.astype(BF16), kf.astype(BF16))
        p_b = _dot_nt(qb.astype(BF16), kb.astype(BF16))
        a_sum = jnp.where(lvs == 1, p_f, jnp.where(lvs == -1, p_b, a_sum))
        p_f = _dot_nt(jnp.where(odd, qf * pltpu.roll(f_f, 1, axis=0), qf).astype(BF16),
                      jnp.where(odd, kf, kf * pltpu.roll(f_f, c - 1, axis=0)).astype(BF16))
        p_b = _dot_nt(jnp.where(odd, qb, qb * pltpu.roll(f_b, c - 1, axis=0)).astype(BF16),
                      jnp.where(odd, kb * pltpu.roll(f_b, 1, axis=0), kb).astype(BF16))
        a_sum = jnp.where(lvs == 2, p_f, jnp.where(lvs == -2, p_b, a_sum))

        sgn = sg_ref[2 * c:3 * c, :]
        ef = jnp.exp2((cum_f - _block_mid(cumf_scr, 4)) * sgn)
        eb = jnp.exp2((cx_b - _block_mid(cumb_scr, 4)) * sgn)
        p_f = _dot_nt((q * ef).astype(BF16), (kf * ef).astype(BF16))
        p_b = _dot_nt((q * eb).astype(BF16), (kb * eb).astype(BF16))
        a_sum = jnp.where(lvs == 3, p_f, jnp.where(lvs == -3, p_b, a_sum))

        for lev in range(4, N_LEVELS + 1):
            m = 1 << (lev - 1)
            ef = jnp.exp2(_mid_distance(cum_f, cumf_scr, m))
            eb = jnp.exp2(_mid_distance(cx_b, cumb_scr, m))
            qe = q * _pick_halves(ef, eb, m)
            ke = _pick_halves(kb, kf, m) * _pick_halves(eb, ef, m)
            lhs = jnp.concatenate([_pick_halves(qe, zero, m), _pick_halves(zero, qe, m)], axis=1)
            rhs = jnp.concatenate([_pick_halves(zero, ke, m), _pick_halves(ke, zero, m)], axis=1)
            a_sum = jnp.where(lv == lev, _dot_nt(lhs.astype(BF16), rhs.astype(BF16)), a_sum)
        acc_scr[sl, :] = jnp.dot(a_sum.astype(BF16), vb, preferred_element_type=F32)

        qc_scr[sl, :] = jnp.concatenate(
            [q * jnp.exp2(cum_f), q * jnp.exp2(jnp.minimum(tot_b - cx_b, 0.0))], axis=1).astype(BF16)
        u_scr[ci, 0] = _dot_tn(vb, (kf * jnp.exp2(jnp.minimum(tot_f - cum_f, 0.0))).astype(BF16))
        u_scr[ci, 1] = _dot_tn(vb, (kb * jnp.exp2(jnp.minimum(cx_b, 0.0))).astype(BF16))
        dl_scr[ci, 0:1, :] = jnp.exp2(tot_f)
        dl_scr[ci, 1:2, :] = jnp.exp2(tot_b)
        return carry

    lax.fori_loop(0, n_chunks, intra, 0, unroll=2)

    def init(s, d):
        return s0_ref[s, d].T if has_s0 else jnp.zeros((HGRN_DV, dk), F32)

    st = None
    for ci in range(n_chunks):
        s = ci // cps
        if ci % cps == 0:
            st = init(s, 0)
        st_scr[ci, :, 0:dk] = st.astype(BF16)
        st = st * dl_scr[ci, 0:1, :] + u_scr[ci, 0]
        if want_state and ci % cps == cps - 1:
            sf_ref[s, 0] = st.T
    for ci in reversed(range(n_chunks)):
        s = ci // cps
        if ci % cps == cps - 1:
            st = init(s, 1)
        st_scr[ci, :, dk:2 * dk] = st.astype(BF16)
        st = st * dl_scr[ci, 1:2, :] + u_scr[ci, 1]
        if want_state and ci % cps == 0:
            sf_ref[s, 1] = st.T

    nw = nw_ref[...]

    for ci in range(n_chunks):
        sl = slice(ci * c, (ci + 1) * c)
        o = acc_scr[sl, :] + _dot_nt(qc_scr[sl, :], st_scr[ci])
        o_ref[sl, :] = (_rms(o) * nw * _silu(p_scr[sl, 4 * dk:5 * dk])).astype(o_ref.dtype)


def _hgrn_call(hn, w_in, lb_raw, nw, consts, s0, state_acc, layer, batch, seq_len, want_state):
    tl, _, lv, sg = consts
    has_s0 = s0 is not None
    m = hn.shape[0]
    tokens = 1024
    seqs = tokens // seq_len
    n_chunks = tokens // CHUNK
    full = lambda arr: pl.BlockSpec(arr.shape, lambda h, b: (0,) * arr.ndim)
    state_spec = pl.BlockSpec((seqs, None, 2, None, HGRN_DK, HGRN_DV), lambda h, b: (b, layer, 0, h, 0, 0))
    wcol = lambda part: pl.BlockSpec((None, D_MODEL, LANES),
                                     lambda h, b, part=part: (layer, 0, part * HGRN_HEADS + h))
    in_specs = [pl.BlockSpec((tokens, D_MODEL), lambda h, b: (b, 0)),
                wcol(0), wcol(1), wcol(2), wcol(3), wcol(4),
                pl.BlockSpec((DEPTH, 2, LANES), lambda h, b: (0, 0, h)),
                pl.BlockSpec((1, HGRN_DV), lambda h, b: (0, 0)),
                full(tl), full(lv), full(sg)]
    args = [hn, w_in, w_in, w_in, w_in, w_in, lb_raw, nw, tl, lv, sg]
    if has_s0:
        in_specs.append(state_spec)
        args.append(s0)
    out_specs = [pl.BlockSpec((tokens, LANES), lambda h, b: (b, h))]
    out_shape = [jax.ShapeDtypeStruct((m, HGRN_W), BF16)]
    aliases = {}
    if want_state:
        out_specs.append(state_spec)
        out_shape.append(jax.ShapeDtypeStruct((batch, DEPTH, 2, HGRN_HEADS, HGRN_DK, HGRN_DV), F32))
        if state_acc is not None:
            aliases = {len(args): 1}
            in_specs.append(pl.BlockSpec(memory_space=pl.ANY))
            args.append(state_acc)
    scratch = [pltpu.VMEM((D_MODEL, 5 * LANES), BF16),
               pltpu.VMEM((tokens, 5 * LANES), F32),
               pltpu.VMEM((tokens, HGRN_DV), F32),
               pltpu.VMEM((tokens, 2 * HGRN_DK), BF16),
               pltpu.VMEM((n_chunks, 2, HGRN_DV, HGRN_DK), F32),
               pltpu.VMEM((n_chunks, 8, HGRN_DK), F32),
               pltpu.VMEM((n_chunks, HGRN_DV, 2 * HGRN_DK), BF16),
               pltpu.VMEM((CHUNK, HGRN_DK), F32),
               pltpu.VMEM((CHUNK, HGRN_DK), F32)]
    res = pl.pallas_call(
        functools.partial(_hgrn_kernel, layer=layer, seq_len=seq_len, seqs=seqs, has_s0=has_s0,
                          has_acc=bool(aliases), want_state=want_state),
        grid=(HGRN_HEADS, m // tokens),
        in_specs=in_specs, out_specs=out_specs, out_shape=out_shape,
        scratch_shapes=scratch,
        input_output_aliases=aliases,
        compiler_params=_cparams(("arbitrary", "arbitrary")),
        name="hgrn",
    )(*args)
    return (res[0], res[1]) if want_state else (res[0], None)


def _ssd_kernel(*refs, seq_len, seqs, width, rows, has_s0, has_acc, want_state):
    it = iter(refs)
    (hn_ref, wz_ref, wx_ref, wb_ref, wc_ref, wdt_ref, cwx_ref, cwb_ref, cwc_ref, cbx_ref, cbb_ref, cbc_ref,
     alog_ref, acol_ref, dtb_ref, dsk_ref, nw_ref, tl_ref, tu_ref) = (next(it) for _ in range(19))
    s0_ref = next(it) if has_s0 else None
    if has_acc:
        next(it)
    o_ref = next(it)
    sf_ref = next(it) if want_state else None
    (w_scr, p_scr, xc_scr, bc_scr, cc_scr, dtg_scr, acc_scr, ecum_scr, u_scr, dl_scr, st_scr) = (
        next(it) for _ in range(11))

    q = CHUNK
    n_chunks = seqs * seq_len // q
    cps = seq_len // q
    hp = HEADS_PER_GROUP
    p = SSD_HEADDIM
    gw = GROUP_W
    n = SSD_STATE

    @pl.when(pl.program_id(1) == 0)
    def _():
        w_scr[:, 0:gw] = wz_ref[...].astype(BF16)
        w_scr[:, gw:2 * gw] = wx_ref[...].astype(BF16)
        w_scr[:, 2 * gw:2 * gw + n] = wb_ref[...].astype(BF16)
        w_scr[:, 2 * gw + n:2 * gw + 2 * n] = wc_ref[...].astype(BF16)
        w_scr[:, 2 * gw + 2 * n:2 * gw + 2 * n + LANES] = wdt_ref[...]

    p_scr[...] = jnp.dot(hn_ref[...], w_scr[...], preferred_element_type=F32)

    xc_scr[...] = _silu(_dwconv(p_scr[:, gw:2 * gw], cwx_ref, cbx_ref, width, rows))
    bc_scr[...] = _silu(_dwconv(p_scr[:, 2 * gw:2 * gw + n], cwb_ref, cbb_ref, width, rows))
    cc_scr[...] = _silu(_dwconv(p_scr[:, 2 * gw + n:2 * gw + 2 * n], cwc_ref, cbc_ref, width, rows)).astype(BF16)
    dtg_scr[...] = _softplus(p_scr[:, 2 * gw + 2 * n:2 * gw + 2 * n + LANES] + dtb_ref[...])
    neg_a = -jnp.exp(alog_ref[...])
    neg_a_t = -jnp.exp(acol_ref[...])

    tl = tl_ref[...]
    tu = tu_ref[...]
    ti = lax.broadcasted_iota(jnp.int32, (q, q), 0)
    si = lax.broadcasted_iota(jnp.int32, (q, q), 1)
    lower = si <= ti
    upper = si >= ti

    def intra(ci, carry):
        r0 = pl.multiple_of(ci * q, q)
        sl = pl.ds(r0, q)
        dt = dtg_scr[sl, :]
        da = dt * neg_a
        cum_f = _sum_rows_exact(tl, da)
        cum_b = _sum_rows_exact(tu, da)
        dt_t = dt.T[0:2 * hp, :]
        da_t = dt_t * neg_a_t
        row_f = _sum_cols_exact(da_t, tu)
        row_b = _sum_cols_exact(da_t, tl)
        wr_f = jnp.exp(row_f[:, q - 1:q] - row_f) * dt_t
        wr_b = jnp.exp(row_b[:, 0:1] - row_b) * dt_t
        xb = xc_scr[sl, :].astype(BF16)
        bm = bc_scr[sl, :]
        bt = bm.T
        gmat = _dot_nt(cc_scr[sl, :], bm.astype(BF16))
        ecum_scr[sl, 0:LANES] = jnp.exp(cum_f)
        ecum_scr[sl, LANES:2 * LANES] = jnp.exp(cum_b)
        ys, u_f, u_b = [], [], []
        for j in range(hp):
            jf, jb = j, hp + j
            dec_f = jnp.exp(jnp.minimum(cum_f[:, jf:jf + 1] - row_f[jf:jf + 1, :], 0.0)) * dt_t[jf:jf + 1, :]
            dec_b = jnp.exp(jnp.minimum(cum_b[:, jb:jb + 1] - row_b[jb:jb + 1, :], 0.0)) * dt_t[jb:jb + 1, :]
            mh = gmat * (jnp.where(lower, dec_f, 0.0) + jnp.where(upper, dec_b, 0.0))
            xh = xb[:, j * p:(j + 1) * p]
            ys.append(jnp.dot(mh.astype(BF16), xh, preferred_element_type=F32))
            u_f.append(jnp.dot((bt * wr_f[jf:jf + 1, :]).astype(BF16), xh, preferred_element_type=F32))
            u_b.append(jnp.dot((bt * wr_b[jb:jb + 1, :]).astype(BF16), xh, preferred_element_type=F32))
        acc_scr[sl, :] = jnp.concatenate(ys, axis=1)
        u_scr[ci, 0] = jnp.concatenate(u_f, axis=1)
        u_scr[ci, 1] = jnp.concatenate(u_b, axis=1)
        dl_scr[ci, 0:1, :] = jnp.exp(cum_f[q - 1:q, :])
        dl_scr[ci, 1:2, :] = jnp.exp(cum_b[0:1, :])
        return carry

    lax.fori_loop(0, n_chunks, intra, 0)

    def head_scale(vec, lane0):
        return jnp.concatenate(
            [jnp.broadcast_to(vec[0:1, lane0 + j:lane0 + j + 1], (1, p)) for j in range(hp)], axis=1)

    def init(s, d):
        return s0_ref[s, d].reshape(hp * p, n).T if has_s0 else jnp.zeros((n, hp * p), F32)

    st = None
    for ci in range(n_chunks):
        s = ci // cps
        if ci % cps == 0:
            st = init(s, 0)
        st_scr[ci, :, 0:hp * p] = st.astype(BF16)
        st = st * head_scale(dl_scr[ci, 0:1, :], 0) + u_scr[ci, 0]
        if want_state and ci % cps == cps - 1:
            sf_ref[s, 0] = st.T.reshape(hp, p, n)
    for ci in reversed(range(n_chunks)):
        s = ci // cps
        if ci % cps == cps - 1:
            st = init(s, 1)
        st_scr[ci, :, hp * p:2 * hp * p] = st.astype(BF16)
        st = st * head_scale(dl_scr[ci, 1:2, :], hp) + u_scr[ci, 1]
        if want_state and ci % cps == 0:
            sf_ref[s, 1] = st.T.reshape(hp, p, n)

    dsk = dsk_ref[...]
    nw = nw_ref[...]
    for ci in range(n_chunks):
        sl = slice(ci * q, (ci + 1) * q)
        yi = jnp.dot(cc_scr[sl, :], st_scr[ci], preferred_element_type=F32)
        ec = ecum_scr[sl, :]
        parts = []
        for j in range(hp):
            parts.append(yi[:, j * p:(j + 1) * p] * ec[:, j:j + 1]
                         + yi[:, (hp + j) * p:(hp + j + 1) * p] * ec[:, LANES + hp + j:LANES + hp + j + 1])
        y = acc_scr[sl, :] + jnp.concatenate(parts, axis=1) + dsk * xc_scr[sl, :]
        y = y * _silu(p_scr[sl, 0:gw])
        o_ref[sl, :] = (_rms(y) * nw).astype(o_ref.dtype)


def _ssd_call(hn, w_in, p, consts, s0, state_acc, layer, batch, seq_len, width, rows, want_state):
    tl, tu = consts
    has_s0 = s0 is not None
    m = hn.shape[0]
    tokens = 1024
    seqs = tokens // seq_len
    assert seqs == 1 or rows == 1
    n_chunks = tokens // CHUNK
    wcols = 2 * GROUP_W + 2 * SSD_STATE + LANES
    xb = SSD_W // LANES
    z0 = 5 * HGRN_KW // GROUP_W
    x0 = z0 + SSD_W // GROUP_W
    b0 = (5 * HGRN_KW + 2 * SSD_W) // LANES
    c0 = b0 + SSD_GROUPS
    full = lambda arr: pl.BlockSpec(arr.shape, lambda g, b: (0,) * arr.ndim)
    in_specs = [
        pl.BlockSpec((tokens, D_MODEL), lambda g, b: (b, 0)),
        pl.BlockSpec((None, D_MODEL, GROUP_W), lambda g, b: (layer, 0, z0 + g)),
        pl.BlockSpec((None, D_MODEL, GROUP_W), lambda g, b: (layer, 0, x0 + g)),
        pl.BlockSpec((None, D_MODEL, LANES), lambda g, b: (layer, 0, b0 + g)),
        pl.BlockSpec((None, D_MODEL, LANES), lambda g, b: (layer, 0, c0 + g)),
        pl.BlockSpec((None, D_MODEL, LANES), lambda g, b: (g, 0, 0)),
        pl.BlockSpec((9, GROUP_W), lambda g, b: (0, g)),
        pl.BlockSpec((9, LANES), lambda g, b: (0, xb + g)),
        pl.BlockSpec((9, LANES), lambda g, b: (0, xb + SSD_GROUPS + g)),
        pl.BlockSpec((1, GROUP_W), lambda g, b: (0, g)),
        pl.BlockSpec((1, LANES), lambda g, b: (0, xb + g)),
        pl.BlockSpec((1, LANES), lambda g, b: (0, xb + SSD_GROUPS + g)),
        pl.BlockSpec((None, 1, LANES), lambda g, b: (g, 0, 0)),
        pl.BlockSpec((None, 2 * HEADS_PER_GROUP, LANES), lambda g, b: (g, 0, 0)),
        pl.BlockSpec((None, 1, LANES), lambda g, b: (g, 0, 0)),
        pl.BlockSpec((1, GROUP_W), lambda g, b: (0, g)),
        pl.BlockSpec((1, GROUP_W), lambda g, b: (0, g)),
        full(tl), full(tu),
    ]
    args = [hn, w_in, w_in, w_in, w_in, p['w_dt'], p['conv_w'], p['conv_w'], p['conv_w'],
            p['conv_b'], p['conv_b'], p['conv_b'],
            p['a_log_rows'], p['a_log_cols'], p['dt_bias_rows'], p['d_rows'], p['norm_w'], tl, tu]
    state_spec = pl.BlockSpec((seqs, None, 2, HEADS_PER_GROUP, SSD_HEADDIM, SSD_STATE),
                              lambda g, b: (b, layer, 0, g, 0, 0))
    if has_s0:
        in_specs.append(state_spec)
        args.append(s0)
    out_specs = [pl.BlockSpec((tokens, GROUP_W), lambda g, b: (b, g))]
    out_shape = [jax.ShapeDtypeStruct((m, SSD_W), BF16)]
    aliases = {}
    if want_state:
        out_specs.append(state_spec)
        out_shape.append(jax.ShapeDtypeStruct((batch, DEPTH, 2, SSD_HEADS, SSD_HEADDIM, SSD_STATE), F32))
        if state_acc is not None:
            aliases = {len(args): 1}
            in_specs.append(pl.BlockSpec(memory_space=pl.ANY))
            args.append(state_acc)
    hpp = HEADS_PER_GROUP * SSD_HEADDIM
    scratch = [pltpu.VMEM((D_MODEL, wcols), BF16),
               pltpu.VMEM((tokens, wcols), F32),
               pltpu.VMEM((tokens, GROUP_W), F32),
               pltpu.VMEM((tokens, SSD_STATE), F32),
               pltpu.VMEM((tokens, SSD_STATE), BF16),
               pltpu.VMEM((tokens, LANES), F32),
               pltpu.VMEM((tokens, GROUP_W), F32),
               pltpu.VMEM((tokens, 2 * LANES), F32),
               pltpu.VMEM((n_chunks, 2, SSD_STATE, hpp), F32),
               pltpu.VMEM((n_chunks, 8, LANES), F32),
               pltpu.VMEM((n_chunks, SSD_STATE, 2 * hpp), BF16)]
    res = pl.pallas_call(
        functools.partial(_ssd_kernel, seq_len=seq_len, seqs=seqs, width=width, rows=rows,
                          has_s0=has_s0, has_acc=bool(aliases), want_state=want_state),
        grid=(SSD_GROUPS, m // tokens),
        in_specs=in_specs, out_specs=out_specs, out_shape=out_shape,
        scratch_shapes=scratch,
        input_output_aliases=aliases,
        compiler_params=_cparams(("arbitrary", "arbitrary")),
        name="ssd",
    )(*args)
    return (res[0], res[1]) if want_state else (res[0], None)


def _outup_kernel(x_ref, oh_ref, os_ref, g1_ref, sh_ref, sc_ref, nw_ref, wo_ref, wu_ref,
                  x1_ref, u_ref, h_scr):
    @pl.when(pl.program_id(1) == 0)
    def _():
        mix = jnp.dot(oh_ref[...], wo_ref[0:HGRN_W, :], preferred_element_type=F32)
        mix = mix + jnp.dot(os_ref[...], wo_ref[HGRN_W:MIX_W, :], preferred_element_type=F32)
        x1 = x_ref[...] + g1_ref[...] * mix
        x1_ref[...] = x1
        h = _rms(x1) * nw_ref[...] * (1.0 + sc_ref[...]) + sh_ref[...]
        h_scr[...] = h.astype(BF16)

    u_ref[...] = jnp.dot(h_scr[...], wu_ref[...], preferred_element_type=F32).astype(u_ref.dtype)


def _outup_call(x2, o_h, o_s, mod3, nw, w_out, w_up, seq_len):
    m = x2.shape[0]
    tm, tn = 1024, 512
    bc = mod3.shape[0]
    seq_of = (lambda i: (i * tm) // seq_len) if bc > 1 else (lambda i: 0)
    modspec = lambda part: pl.BlockSpec((None, 1, D_MODEL), lambda i, j, part=part: (seq_of(i), 0, part))
    return pl.pallas_call(
        _outup_kernel,
        grid=(m // tm, 2 * D_FF // tn),
        in_specs=[
            pl.BlockSpec((tm, D_MODEL), lambda i, j: (i, 0)),
            pl.BlockSpec((tm, HGRN_W), lambda i, j: (i, 0)),
            pl.BlockSpec((tm, SSD_W), lambda i, j: (i, 0)),
            modspec(2), modspec(3), modspec(4),
            pl.BlockSpec((1, D_MODEL), lambda i, j: (0, 0)),
            pl.BlockSpec((MIX_W, D_MODEL), lambda i, j: (0, 0)),
            pl.BlockSpec((D_MODEL, tn), lambda i, j: (0, j)),
        ],
        out_specs=[
            pl.BlockSpec((tm, D_MODEL), lambda i, j: (i, 0)),
            pl.BlockSpec((tm, tn), lambda i, j: (i, j)),
        ],
        out_shape=[jax.ShapeDtypeStruct((m, D_MODEL), F32),
                   jax.ShapeDtypeStruct((m, 2 * D_FF), BF16)],
        scratch_shapes=[pltpu.VMEM((tm, D_MODEL), BF16)],
        compiler_params=_cparams(("arbitrary", "arbitrary")),
        name="outup",
    )(x2, o_h, o_s, mod3, mod3, mod3, nw, w_out, w_up)


def _ffndown_kernel(x1_ref, ug_ref, uv_ref, g2_ref, cwg_ref, cwv_ref, cbg_ref, cbv_ref, wd_ref, fw_ref,
                    o_ref, acc_scr, act_scr, *, width, rows, final):
    k = pl.program_id(1)

    for s in range(ug_ref.shape[1] // LANES):
        cs = slice(s * LANES, (s + 1) * LANES)
        gate = _dwconv(ug_ref[:, cs].astype(F32), cwg_ref[:, cs], cbg_ref[:, cs], width, rows)
        val = _dwconv(uv_ref[:, cs].astype(F32), cwv_ref[:, cs], cbv_ref[:, cs], width, rows)
        act_scr[:, cs] = (_silu(gate) * val).astype(BF16)
    part = jnp.dot(act_scr[...], wd_ref[...], preferred_element_type=F32)

    @pl.when(k == 0)
    def _():
        acc_scr[...] = part

    @pl.when(k > 0)
    def _():
        acc_scr[...] += part

    @pl.when(k == pl.num_programs(1) - 1)
    def _():
        x2 = x1_ref[...] + g2_ref[...] * acc_scr[...]
        if final:
            x2 = _rms(x2) * fw_ref[...]
        o_ref[...] = x2


def _ffndown_call(x1, u, mod3, conv_w, conv_b, w_down, final_w, seq_len, width, rows, final):
    m = x1.shape[0]
    tokens = 1024
    assert tokens == seq_len or rows == 1
    nk = 2
    tk = D_FF // nk
    bc = mod3.shape[0]
    seq_of = (lambda b: (b * tokens) // seq_len) if bc > 1 else (lambda b: 0)
    return pl.pallas_call(
        functools.partial(_ffndown_kernel, width=width, rows=rows, final=final),
        grid=(m // tokens, nk),
        in_specs=[
            pl.BlockSpec((tokens, D_MODEL), lambda b, k: (b, 0)),
            pl.BlockSpec((tokens, tk), lambda b, k: (b, k)),
            pl.BlockSpec((tokens, tk), lambda b, k: (b, nk + k)),
            pl.BlockSpec((None, 1, D_MODEL), lambda b, k: (seq_of(b), 0, 5)),
            pl.BlockSpec((9, tk), lambda b, k: (0, k)),
            pl.BlockSpec((9, tk), lambda b, k: (0, nk + k)),
            pl.BlockSpec((1, tk), lambda b, k: (0, k)),
            pl.BlockSpec((1, tk), lambda b, k: (0, nk + k)),
            pl.BlockSpec((tk, D_MODEL), lambda b, k: (k, 0)),
            pl.BlockSpec((1, D_MODEL), lambda b, k: (0, 0)),
        ],
        out_specs=pl.BlockSpec((tokens, D_MODEL), lambda b, k: (b, 0)),
        out_shape=jax.ShapeDtypeStruct((m, D_MODEL), F32),
        scratch_shapes=[pltpu.VMEM((tokens, D_MODEL), F32), pltpu.VMEM((tokens, tk), BF16)],
        compiler_params=_cparams(("arbitrary", "arbitrary")),
        name="ffndown",
    )(x1, u, u, mod3, conv_w, conv_w, conv_b, conv_b, w_down, final_w)


def _ssd_param_rows(a_log, dt_bias, d_skip):
    def per_group(v):
        return v.reshape(2, SSD_GROUPS, HEADS_PER_GROUP).transpose(1, 0, 2).reshape(SSD_GROUPS, 2 * HEADS_PER_GROUP)

    def rows(v):
        return jnp.pad(per_group(v), ((0, 0), (0, LANES - 2 * HEADS_PER_GROUP))).reshape(SSD_GROUPS, 1, LANES)
    a_cols = jnp.broadcast_to(per_group(a_log)[:, :, None], (SSD_GROUPS, 2 * HEADS_PER_GROUP, LANES))
    return rows(a_log), a_cols, rows(dt_bias), jnp.repeat(d_skip, SSD_HEADDIM).reshape(1, SSD_W)


def _group_dt_weight(w_dt):
    g = SSD_GROUPS
    dt = w_dt.reshape(D_MODEL, 2, g, HEADS_PER_GROUP).transpose(2, 0, 1, 3).reshape(g, D_MODEL, 2 * HEADS_PER_GROUP)
    return jnp.pad(dt, ((0, 0), (0, 0), (0, LANES - 2 * HEADS_PER_GROUP))).astype(BF16)


def _run_pass(x, mod_all_rows, s_h0, s_s0, layers, hgrn_lb, final_norm_w, consts, width, rows, want_state):
    batch, seq_len, _ = x.shape
    m = batch * seq_len
    x2 = x.reshape(m, D_MODEL)
    tl, tu = consts[0], consts[1]
    new_h, new_s = None, None
    for l, p in enumerate(layers):
        mod3 = mod_all_rows[l]
        hn = _norm_call(x2, mod3, p['norm_w1'], seq_len)
        o_h, new_h = _hgrn_call(hn, p['w_in'], hgrn_lb, p['hgrn_norm_w'], consts,
                                s_h0, new_h, l, batch, seq_len, want_state)
        o_s, new_s = _ssd_call(hn, p['w_in'], p['ssd'], (tl, tu),
                               s_s0, new_s, l, batch, seq_len, width, rows, want_state)
        x1, u = _outup_call(x2, o_h, o_s, mod3, p['norm_w2'], p['w_out'], p['ffn_up'], seq_len)
        x2 = _ffndown_call(x1, u, mod3, p['ffn_conv_w'], p['ffn_conv_b'], p['ffn_down'], final_norm_w,
                           seq_len, width, rows, final=(l == DEPTH - 1))
    return x2.reshape(batch, seq_len, D_MODEL), new_h, new_s


def kernel(x_prompt, x_sample, c, state_hgrn, state_ssd, c_ctx, norm_w, final_norm_w, w_ada, b_ada,
           w_in, w_out, hgrn_lb, hgrn_norm_w, ssd_conv_w, ssd_conv_b, ssd_a_log, ssd_dt_bias, ssd_d,
           ssd_norm_w, ffn_up, ffn_conv_w, ffn_conv_b, ffn_down):
    dec_batch = c.shape[0]
    consts = _scan_constants() + (_level_signs(),)

    cond8 = jnp.concatenate([c_ctx[None], c, jnp.zeros((8 - 1 - dec_batch, D_MODEL), F32)], axis=0)
    mod_all = _mod_call(cond8, w_ada, b_ada)
    mod_ctx = [mod_all[l, 0:1].reshape(1, 1, 6 * D_MODEL) for l in range(DEPTH)]
    mod_lat = [mod_all[l, 1:1 + dec_batch].reshape(dec_batch, 1, 6 * D_MODEL) for l in range(DEPTH)]

    layers = []
    for l in range(DEPTH):
        a_rows, a_cols, b_rows, d_rows = _ssd_param_rows(ssd_a_log[l], ssd_dt_bias[l], ssd_d[l])
        layers.append(dict(
            norm_w1=norm_w[l, 0].reshape(1, D_MODEL), norm_w2=norm_w[l, 1].reshape(1, D_MODEL),
            w_in=w_in,
            w_out=w_out[l].astype(BF16),
            hgrn_norm_w=hgrn_norm_w[l].reshape(1, HGRN_DV),
            ssd=dict(w_dt=_group_dt_weight(w_in[l, :, MAIN_COLS:]),
                     conv_w=ssd_conv_w[l].reshape(9, CONV_CH), conv_b=ssd_conv_b[l].reshape(1, CONV_CH),
                     a_log_rows=a_rows, a_log_cols=a_cols, dt_bias_rows=b_rows, d_rows=d_rows,
                     norm_w=ssd_norm_w[l].reshape(1, SSD_W)),
            ffn_up=ffn_up[l].astype(BF16),
            ffn_conv_w=ffn_conv_w[l].reshape(9, 2 * D_FF), ffn_conv_b=ffn_conv_b[l].reshape(1, 2 * D_FF),
            ffn_down=ffn_down[l].astype(BF16),
        ))
    fnw = final_norm_w.reshape(1, D_MODEL)

    y_prompt, new_h, new_s = _run_pass(x_prompt, mod_ctx, None, None, layers, hgrn_lb, fnw, consts,
                                       width=x_prompt.shape[1], rows=1, want_state=True)
    y_sample, _, _ = _run_pass(x_sample, mod_lat, state_hgrn, state_ssd, layers, hgrn_lb, fnw, consts,
                               width=GRID_W, rows=x_sample.shape[1] // GRID_W, want_state=False)
    return (y_prompt, y_sample, new_h, new_s)
```

```python
import functools
import math

import numpy as np
import jax
import jax.numpy as jnp
from jax import lax
from jax.experimental import pallas as pl
from jax.experimental.pallas import tpu as pltpu

F32 = jnp.float32
BF16 = jnp.bfloat16

D_MODEL = 1024
DEPTH = 2
GRID_W = 64
HGRN_HEADS = 8
HGRN_DK = 128
HGRN_DV = 128
HGRN_KW = HGRN_HEADS * HGRN_DK
HGRN_W = HGRN_HEADS * HGRN_DV
SSD_W = 1024
SSD_HEADDIM = 64
SSD_HEADS = 16
SSD_GROUPS = 4
SSD_STATE = 128
HEADS_PER_GROUP = SSD_HEADS // SSD_GROUPS
GROUP_W = SSD_W // SSD_GROUPS
MIX_W = HGRN_W + SSD_W
CONV_CH = SSD_W + 2 * SSD_GROUPS * SSD_STATE
D_FF = 2816
MAIN_COLS = 3 * HGRN_KW + 2 * HGRN_W + SSD_W + CONV_CH
DT_COLS = 2 * SSD_HEADS
EPS = 1e-6
LOG2E = math.log2(math.e)

LANES = 128
CHUNK = 128
N_LEVELS = 7
VMEM_LIMIT = 56 * 1024 * 1024


def _cparams(sem):
    return pltpu.CompilerParams(dimension_semantics=sem, vmem_limit_bytes=VMEM_LIMIT)


def _sigmoid(x):
    return 1.0 / (1.0 + jnp.exp(-x))


def _silu(x):
    return x * _sigmoid(x)


def _softplus(x):
    return jnp.maximum(x, 0.0) + jnp.log1p(jnp.exp(-jnp.abs(x)))


def _rms(x):
    return x * lax.rsqrt(jnp.mean(x * x, axis=-1, keepdims=True) + EPS)


def _split3(x):
    hi = x.astype(BF16)
    r = x - hi.astype(F32)
    mid = r.astype(BF16)
    lo = (r - mid.astype(F32)).astype(BF16)
    return hi, mid, lo


def _sum_rows_exact(w01, x):
    n = x.shape[1]
    hi, mid, lo = _split3(x)
    p = jnp.dot(w01, jnp.concatenate([hi, mid, lo], axis=1), preferred_element_type=F32)
    return p[:, :n] + p[:, n:2 * n] + p[:, 2 * n:]


def _sum_cols_exact(x, w01):
    m = x.shape[0]
    hi, mid, lo = _split3(x)
    p = jnp.dot(jnp.concatenate([hi, mid, lo], axis=0), w01, preferred_element_type=F32)
    return p[:m] + p[m:2 * m] + p[2 * m:]


def _dot_nt(a, b):
    return lax.dot_general(a, b, (((1,), (1,)), ((), ())), preferred_element_type=F32)


def _dot_tn(a, b):
    return lax.dot_general(a, b, (((0,), (0,)), ((), ())), preferred_element_type=F32)


def _dwconv(x, w_ref, b_ref, width, rows):
    seq, _ = x.shape
    t = lax.broadcasted_iota(jnp.int32, x.shape, 0)
    col = jnp.bitwise_and(t, width - 1)
    xl = jnp.where(col == 0, 0.0, pltpu.roll(x, 1, axis=0))
    xr = jnp.where(col == width - 1, 0.0, pltpu.roll(x, seq - 1, axis=0))

    def hrow(kh, a, b, c):
        return w_ref[3 * kh:3 * kh + 1, :] * a + w_ref[3 * kh + 1:3 * kh + 2, :] * b \
            + w_ref[3 * kh + 2:3 * kh + 3, :] * c

    out = b_ref[0:1, :] + hrow(1, xl, x, xr)
    if rows > 1:
        inner = seq - width
        up = hrow(0, xl[:inner], x[:inner], xr[:inner])
        out = jnp.concatenate([out[:width], out[width:] + up], axis=0)
        dn = hrow(2, xl[width:], x[width:], xr[width:])
        out = jnp.concatenate([out[:inner] + dn, out[inner:]], axis=0)
    return out


def _scan_constants():
    c = CHUNK
    t = np.arange(c)[:, None]
    i = np.arange(c)[None, :]
    x = t ^ i
    lv = np.where(x == 0, 0, np.floor(np.log2(np.maximum(x, 1))).astype(np.int64) + 1)
    lv = np.where(i > t, -lv, lv)
    return jnp.asarray(i <= t, BF16), jnp.asarray(i >= t, BF16), jnp.asarray(lv, jnp.int32)


def _level_signs():
    t = np.arange(CHUNK)[:, None]
    sg = [np.where(((t >> lev) & 1) == 1, 1.0, -1.0) * np.ones((1, LANES)) for lev in range(N_LEVELS)]
    return jnp.asarray(np.concatenate(sg, 0), F32)


def _mod_kernel(cond_ref, w_ref, b_ref, o_ref):
    a = _silu(cond_ref[...]).astype(BF16)
    o_ref[...] = jnp.dot(a, w_ref[...].astype(BF16), preferred_element_type=F32) + b_ref[...]


def _mod_call(cond8, w_ada, b_ada):
    n = 6 * D_MODEL
    tn = D_MODEL
    return pl.pallas_call(
        _mod_kernel,
        grid=(DEPTH, n // tn),
        in_specs=[
            pl.BlockSpec((8, D_MODEL), lambda l, j: (0, 0)),
            pl.BlockSpec((None, D_MODEL, tn), lambda l, j: (l, 0, j)),
            pl.BlockSpec((None, 1, tn), lambda l, j: (l, 0, j)),
        ],
        out_specs=pl.BlockSpec((None, 8, tn), lambda l, j: (l, 0, j)),
        out_shape=jax.ShapeDtypeStruct((DEPTH, 8, n), F32),
        compiler_params=_cparams(("arbitrary", "arbitrary")),
        name="mod",
    )(cond8, w_ada, b_ada.reshape(DEPTH, 1, n))


def _norm_kernel(x_ref, sh_ref, sc_ref, nw_ref, hn_ref):
    h = _rms(x_ref[...]) * nw_ref[...] * (1.0 + sc_ref[...]) + sh_ref[...]
    hn_ref[...] = h.astype(BF16)


def _norm_call(x2, mod3, nw, seq_len):
    m = x2.shape[0]
    tm = 512
    bc = mod3.shape[0]
    seq_of = (lambda i: (i * tm) // seq_len) if bc > 1 else (lambda i: 0)
    return pl.pallas_call(
        _norm_kernel,
        grid=(m // tm,),
        in_specs=[
            pl.BlockSpec((tm, D_MODEL), lambda i: (i, 0)),
            pl.BlockSpec((None, 1, D_MODEL), lambda i: (seq_of(i), 0, 0)),
            pl.BlockSpec((None, 1, D_MODEL), lambda i: (seq_of(i), 0, 1)),
            pl.BlockSpec((1, D_MODEL), lambda i: (0, 0)),
        ],
        out_specs=pl.BlockSpec((tm, D_MODEL), lambda i: (i, 0)),
        out_shape=jax.ShapeDtypeStruct((m, D_MODEL), BF16),
        compiler_params=_cparams(("arbitrary",)),
        name="norm",
    )(x2, mod3, mod3, nw)


def _hgrn_gates(f_pre, lbd):
    ea = jnp.exp(-jnp.abs(f_pre))
    log2_sig = jnp.minimum(f_pre, 0.0) * LOG2E - jnp.log2(1.0 + ea)
    sig_neg = jnp.where(f_pre >= 0.0, ea, 1.0) / (1.0 + ea)
    if lbd is None:
        return log2_sig, sig_neg
    a = jnp.log2(lbd)
    b = jnp.log2(1.0 - lbd) + log2_sig
    log2_f = jnp.maximum(a, b) + jnp.log2(1.0 + jnp.exp2(-jnp.abs(a - b)))
    return log2_f, (1.0 - lbd) * sig_neg


def _block_mid(scr, m):
    return jnp.concatenate(
        [jnp.broadcast_to(scr[r0 + m - 1:r0 + m, :], (2 * m, scr.shape[1])) for r0 in range(0, CHUNK, 2 * m)],
        axis=0)


def _mid_distance(c, scr, m):
    pieces = []
    for r0 in range(0, CHUNK, 2 * m):
        mid = jnp.broadcast_to(scr[r0 + m - 1:r0 + m, :], (m, scr.shape[1]))
        pieces += [mid - c[r0:r0 + m], c[r0 + m:r0 + 2 * m] - mid]
    return jnp.concatenate(pieces, axis=0)


def _pick_halves(up_val, low_val, m):
    pieces = []
    for r0 in range(0, CHUNK, 2 * m):
        pieces += [low_val[r0:r0 + m], up_val[r0 + m:r0 + 2 * m]]
    return jnp.concatenate(pieces, axis=0)


def _hgrn_kernel(*refs, layer, seq_len, seqs, has_s0, has_acc, want_state):
    it = iter(refs)
    hn_ref = next(it)
    w_refs = [next(it) for _ in range(5)]
    lb_ref, nw_ref, tl_ref, lv_ref, sg_ref = (next(it) for _ in range(5))
    s0_ref = next(it) if has_s0 else None
    if has_acc:
        next(it)
    o_ref = next(it)
    sf_ref = next(it) if want_state else None
    (w_scr, p_scr, acc_scr, qc_scr, u_scr, dl_scr, st_scr, cumf_scr, cumb_scr) = (next(it) for _ in range(9))

    c = CHUNK
    dk = HGRN_DK
    n_chunks = seqs * seq_len // c
    cps = seq_len // c

    @pl.when(pl.program_id(1) == 0)
    def _():
        for i, w_ref in enumerate(w_refs):
            w_scr[i * LANES:(i + 1) * LANES, :] = w_ref[...].astype(BF16)

    p_scr[...] = _dot_nt(hn_ref[...], w_scr[...])

    lb = None
    if layer > 0:
        lbr = lb_ref[...]
        e = jnp.exp(lbr - jnp.max(lbr, axis=0, keepdims=True))
        sm = e / jnp.sum(e, axis=0, keepdims=True)
        lb = sm[1]
        for i in range(2, layer + 1):
            lb = lb + sm[i]

    lvs = lv_ref[...]
    lv = jnp.abs(lvs)
    tl = tl_ref[...]
    zero = jnp.zeros((c, dk), F32)
    odd = jnp.bitwise_and(lax.broadcasted_iota(jnp.int32, (c, dk), 0), 1) == 1

    def intra(ci, carry):
        r0 = pl.multiple_of(ci * c, c)
        sl = pl.ds(r0, c)
        q = _silu(p_scr[sl, 0:dk]) * (dk ** -0.5)
        lf_f, kf = _hgrn_gates(p_scr[sl, dk:2 * dk], None if lb is None else lb[0:1, :])
        lf_b, kb = _hgrn_gates(p_scr[sl, 2 * dk:3 * dk], None if lb is None else lb[1:2, :])
        vb = p_scr[sl, 3 * dk:4 * dk].astype(BF16)

        cum_f = _sum_rows_exact(tl, lf_f)
        cum_b = _sum_rows_exact(tl, lf_b)
        cx_b = cum_b - lf_b
        cumf_scr[...] = cum_f
        cumb_scr[...] = cum_b
        tot_f = cum_f[c - 1:c, :]
        tot_b = cum_b[c - 1:c, :]

        a_sum = _dot_nt(q.astype(BF16), (kf + kb).astype(BF16))

        f_f = jnp.exp2(lf_f)
        f_b = jnp.exp2(lf_b)
        qf = q * f_f
        qb = q * f_b
        p_f = _dot_nt(qf.astype(BF16), kf.astype(BF16))
        p_b = _dot_nt(qb.astype(BF16), kb.astype(BF16))
        a_sum = jnp.where(lvs == 1, p_f, jnp.where(lvs == -1, p_b, a_sum))
        p_f = _dot_nt(jnp.where(odd, qf * pltpu.roll(f_f, 1, axis=0), qf).astype(BF16),
                      jnp.where(odd, kf, kf * pltpu.roll(f_f, c - 1, axis=0)).astype(BF16))
        p_b = _dot_nt(jnp.where(odd, qb, qb * pltpu.roll(f_b, c - 1, axis=0)).astype(BF16),
                      jnp.where(odd, kb * pltpu.roll(f_b, 1, axis=0), kb).astype(BF16))
        a_sum = jnp.where(lvs == 2, p_f, jnp.where(lvs == -2, p_b, a_sum))

        sgn = sg_ref[2 * c:3 * c, :]
        ef = jnp.exp2((cum_f - _block_mid(cumf_scr, 4)) * sgn)
        eb = jnp.exp2((cx_b - _block_mid(cumb_scr, 4)) * sgn)
        p_f = _dot_nt((q * ef).astype(BF16), (kf * ef).astype(BF16))
        p_b = _dot_nt((q * eb).astype(BF16), (kb * eb).astype(BF16))
        a_sum = jnp.where(lvs == 3, p_f, jnp.where(lvs == -3, p_b, a_sum))

        for lev in range(4, N_LEVELS + 1):
            m = 1 << (lev - 1)
            ef = jnp.exp2(_mid_distance(cum_f, cumf_scr, m))
            eb = jnp.exp2(_mid_distance(cx_b, cumb_scr, m))
            qe = q * _pick_halves(ef, eb, m)
            ke = _pick_halves(kb, kf, m) * _pick_halves(eb, ef, m)
            lhs = jnp.concatenate([_pick_halves(qe, zero, m), _pick_halves(zero, qe, m)], axis=1)
            rhs = jnp.concatenate([_pick_halves(zero, ke, m), _pick_halves(ke, zero, m)], axis=1)
            a_sum = jnp.where(lv == lev, _dot_nt(lhs.astype(BF16), rhs.astype(BF16)), a_sum)
        acc_scr[sl, :] = jnp.dot(a_sum.astype(BF16), vb, preferred_element_type=F32)

        qc_scr[sl, :] = jnp.concatenate(
            [q * jnp.exp2(cum_f), q * jnp.exp2(jnp.minimum(tot_b - cx_b, 0.0))], axis=1).astype(BF16)
        u_scr[ci, 0] = _dot_tn(vb, (kf * jnp.exp2(jnp.minimum(tot_f - cum_f, 0.0))).astype(BF16))
        u_scr[ci, 1] = _dot_tn(vb, (kb * jnp.exp2(jnp.minimum(cx_b, 0.0))).astype(BF16))
        dl_scr[ci, 0:1, :] = jnp.exp2(tot_f)
        dl_scr[ci, 1:2, :] = jnp.exp2(tot_b)
        return carry

    lax.fori_loop(0, n_chunks, intra, 0, unroll=2)

    def init(s, d):
        return s0_ref[s, d].T if has_s0 else jnp.zeros((HGRN_DV, dk), F32)

    st = None
    for ci in range(n_chunks):
        s = ci // cps
        if ci % cps == 0:
            st = init(s, 0)
        st_scr[ci, :, 0:dk] = st.astype(BF16)
        st = st * dl_scr[ci, 0:1, :] + u_scr[ci, 0]
        if want_state and ci % cps == cps - 1:
            sf_ref[s, 0] = st.T
    for ci in reversed(range(n_chunks)):
        s = ci // cps
        if ci % cps == cps - 1:
            st = init(s, 1)
        st_scr[ci, :, dk:2 * dk] = st.astype(BF16)
        st = st * dl_scr[ci, 1:2, :] + u_scr[ci, 1]
        if want_state and ci % cps == 0:
            sf_ref[s, 1] = st.T

    nw = nw_ref[...]

    for ci in range(n_chunks):
        sl = slice(ci * c, (ci + 1) * c)
        o = acc_scr[sl, :] + _dot_nt(qc_scr[sl, :], st_scr[ci])
        o_ref[sl, :] = (_rms(o) * nw * _silu(p_scr[sl, 4 * dk:5 * dk])).astype(o_ref.dtype)


def _hgrn_call(hn, w_in_t, lb_raw, nw, consts, s0, state_acc, layer, batch, seq_len, want_state):
    tl, _, lv, sg = consts
    has_s0 = s0 is not None
    m = hn.shape[0]
    tokens = 1024
    seqs = tokens // seq_len
    n_chunks = tokens // CHUNK
    full = lambda arr: pl.BlockSpec(arr.shape, lambda h, b: (0,) * arr.ndim)
    state_spec = pl.BlockSpec((seqs, None, 2, None, HGRN_DK, HGRN_DV), lambda h, b: (b, layer, 0, h, 0, 0))
    wcol = lambda part: pl.BlockSpec((None, LANES, D_MODEL),
                                     lambda h, b, part=part: (layer, part * HGRN_HEADS + h, 0))
    in_specs = [pl.BlockSpec((tokens, D_MODEL), lambda h, b: (b, 0)),
                wcol(0), wcol(1), wcol(2), wcol(3), wcol(4),
                pl.BlockSpec((DEPTH, 2, LANES), lambda h, b: (0, 0, h)),
                pl.BlockSpec((1, HGRN_DV), lambda h, b: (0, 0)),
                full(tl), full(lv), full(sg)]
    args = [hn, w_in_t, w_in_t, w_in_t, w_in_t, w_in_t, lb_raw, nw, tl, lv, sg]
    if has_s0:
        in_specs.append(state_spec)
        args.append(s0)
    out_specs = [pl.BlockSpec((tokens, LANES), lambda h, b: (b, h))]
    out_shape = [jax.ShapeDtypeStruct((m, HGRN_W), BF16)]
    aliases = {}
    if want_state:
        out_specs.append(state_spec)
        out_shape.append(jax.ShapeDtypeStruct((batch, DEPTH, 2, HGRN_HEADS, HGRN_DK, HGRN_DV), F32))
        if state_acc is not None:
            aliases = {len(args): 1}
            in_specs.append(pl.BlockSpec(memory_space=pl.ANY))
            args.append(state_acc)
    scratch = [pltpu.VMEM((5 * LANES, D_MODEL), BF16),
               pltpu.VMEM((tokens, 5 * LANES), F32),
               pltpu.VMEM((tokens, HGRN_DV), F32),
               pltpu.VMEM((tokens, 2 * HGRN_DK), BF16),
               pltpu.VMEM((n_chunks, 2, HGRN_DV, HGRN_DK), F32),
               pltpu.VMEM((n_chunks, 8, HGRN_DK), F32),
               pltpu.VMEM((n_chunks, HGRN_DV, 2 * HGRN_DK), BF16),
               pltpu.VMEM((CHUNK, HGRN_DK), F32),
               pltpu.VMEM((CHUNK, HGRN_DK), F32)]
    res = pl.pallas_call(
        functools.partial(_hgrn_kernel, layer=layer, seq_len=seq_len, seqs=seqs, has_s0=has_s0,
                          has_acc=bool(aliases), want_state=want_state),
        grid=(HGRN_HEADS, m // tokens),
        in_specs=in_specs, out_specs=out_specs, out_shape=out_shape,
        scratch_shapes=scratch,
        input_output_aliases=aliases,
        compiler_params=_cparams(("arbitrary", "arbitrary")),
        name="hgrn",
    )(*args)
    return (res[0], res[1]) if want_state else (res[0], None)


def _ssd_kernel(*refs, seq_len, seqs, width, rows, has_s0, has_acc, want_state):
    it = iter(refs)
    (hn_ref, wz_ref, wx_ref, wb_ref, wc_ref, wdt_ref, cwx_ref, cwb_ref, cwc_ref, cbx_ref, cbb_ref, cbc_ref,
     alog_ref, acol_ref, dtb_ref, dsk_ref, nw_ref, tl_ref, tu_ref) = (next(it) for _ in range(19))
    s0_ref = next(it) if has_s0 else None
    if has_acc:
        next(it)
    o_ref = next(it)
    sf_ref = next(it) if want_state else None
    (w_scr, p_scr, xc_scr, bc_scr, cc_scr, dtg_scr, acc_scr, ecum_scr, u_scr, dl_scr, st_scr) = (
        next(it) for _ in range(11))

    q = CHUNK
    n_chunks = seqs * seq_len // q
    cps = seq_len // q
    hp = HEADS_PER_GROUP
    p = SSD_HEADDIM
    gw = GROUP_W
    n = SSD_STATE
    dt0 = 2 * gw + 2 * n

    @pl.when(pl.program_id(1) == 0)
    def _():
        w_scr[0:gw, :] = wz_ref[...].astype(BF16)
        w_scr[gw:2 * gw, :] = wx_ref[...].astype(BF16)
        w_scr[2 * gw:2 * gw + n, :] = wb_ref[...].astype(BF16)
        w_scr[2 * gw + n:2 * gw + 2 * n, :] = wc_ref[...].astype(BF16)
        w_scr[dt0:dt0 + DT_COLS, :] = wdt_ref[...].astype(BF16)
        w_scr[dt0 + DT_COLS:dt0 + LANES, :] = jnp.zeros((LANES - DT_COLS, D_MODEL), BF16)

    p_scr[...] = _dot_nt(hn_ref[...], w_scr[...])

    xc_scr[...] = _silu(_dwconv(p_scr[:, gw:2 * gw], cwx_ref, cbx_ref, width, rows))
    bc_scr[...] = _silu(_dwconv(p_scr[:, 2 * gw:2 * gw + n], cwb_ref, cbb_ref, width, rows))
    cc_scr[...] = _silu(_dwconv(p_scr[:, 2 * gw + n:2 * gw + 2 * n], cwc_ref, cbc_ref, width, rows)).astype(BF16)
    g = pl.program_id(0)
    dt_all = p_scr[:, dt0:dt0 + LANES]
    lane = lax.broadcasted_iota(jnp.int32, dt_all.shape, 1)
    dt_fw = pltpu.roll(dt_all, jnp.bitwise_and(LANES - hp * g, LANES - 1), axis=1)
    dt_bw = pltpu.roll(dt_all, jnp.bitwise_and(LANES - (SSD_HEADS - hp) - hp * g, LANES - 1), axis=1)
    dtg_scr[...] = _softplus(jnp.where(lane < hp, dt_fw, dt_bw) + dtb_ref[...])
    neg_a = -jnp.exp(alog_ref[...])
    neg_a_t = -jnp.exp(acol_ref[...])

    tl = tl_ref[...]
    tu = tu_ref[...]
    ti = lax.broadcasted_iota(jnp.int32, (q, q), 0)
    si = lax.broadcasted_iota(jnp.int32, (q, q), 1)
    lower = si <= ti
    upper = si >= ti

    def intra(ci, carry):
        r0 = pl.multiple_of(ci * q, q)
        sl = pl.ds(r0, q)
        dt = dtg_scr[sl, :]
        da = dt * neg_a
        cum_f = _sum_rows_exact(tl, da)
        cum_b = _sum_rows_exact(tu, da)
        dt_t = dt.T[0:2 * hp, :]
        da_t = dt_t * neg_a_t
        row_f = _sum_cols_exact(da_t, tu)
        row_b = _sum_cols_exact(da_t, tl)
        wr_f = jnp.exp(row_f[:, q - 1:q] - row_f) * dt_t
        wr_b = jnp.exp(row_b[:, 0:1] - row_b) * dt_t
        xb = xc_scr[sl, :].astype(BF16)
        bm = bc_scr[sl, :]
        bt = bm.T
        gmat = _dot_nt(cc_scr[sl, :], bm.astype(BF16))
        ecum_scr[sl, 0:LANES] = jnp.exp(cum_f)
        ecum_scr[sl, LANES:2 * LANES] = jnp.exp(cum_b)
        ys, u_f, u_b = [], [], []
        for j in range(hp):
            jf, jb = j, hp + j
            dec_f = jnp.exp(jnp.minimum(cum_f[:, jf:jf + 1] - row_f[jf:jf + 1, :], 0.0)) * dt_t[jf:jf + 1, :]
            dec_b = jnp.exp(jnp.minimum(cum_b[:, jb:jb + 1] - row_b[jb:jb + 1, :], 0.0)) * dt_t[jb:jb + 1, :]
            mh = gmat * (jnp.where(lower, dec_f, 0.0) + jnp.where(upper, dec_b, 0.0))
            xh = xb[:, j * p:(j + 1) * p]
            ys.append(jnp.dot(mh.astype(BF16), xh, preferred_element_type=F32))
            u_f.append(jnp.dot((bt * wr_f[jf:jf + 1, :]).astype(BF16), xh, preferred_element_type=F32))
            u_b.append(jnp.dot((bt * wr_b[jb:jb + 1, :]).astype(BF16), xh, preferred_element_type=F32))
        acc_scr[sl, :] = jnp.concatenate(ys, axis=1)
        u_scr[ci, 0] = jnp.concatenate(u_f, axis=1)
        u_scr[ci, 1] = jnp.concatenate(u_b, axis=1)
        dl_scr[ci, 0:1, :] = jnp.exp(cum_f[q - 1:q, :])
        dl_scr[ci, 1:2, :] = jnp.exp(cum_b[0:1, :])
        return carry

    lax.fori_loop(0, n_chunks, intra, 0, unroll=2)

    def head_scale(vec, lane0):
        return jnp.concatenate(
            [jnp.broadcast_to(vec[0:1, lane0 + j:lane0 + j + 1], (1, p)) for j in range(hp)], axis=1)

    def init(s, d):
        return s0_ref[s, d].reshape(hp * p, n).T if has_s0 else jnp.zeros((n, hp * p), F32)

    st = None
    for ci in range(n_chunks):
        s = ci // cps
        if ci % cps == 0:
            st = init(s, 0)
        st_scr[ci, :, 0:hp * p] = st.astype(BF16)
        st = st * head_scale(dl_scr[ci, 0:1, :], 0) + u_scr[ci, 0]
        if want_state and ci % cps == cps - 1:
            sf_ref[s, 0] = st.T.reshape(hp, p, n)
    for ci in reversed(range(n_chunks)):
        s = ci // cps
        if ci % cps == cps - 1:
            st = init(s, 1)
        st_scr[ci, :, hp * p:2 * hp * p] = st.astype(BF16)
        st = st * head_scale(dl_scr[ci, 1:2, :], hp) + u_scr[ci, 1]
        if want_state and ci % cps == 0:
            sf_ref[s, 1] = st.T.reshape(hp, p, n)

    dsk = dsk_ref[...]
    nw = nw_ref[...]
    for ci in range(n_chunks):
        sl = slice(ci * q, (ci + 1) * q)
        yi = jnp.dot(cc_scr[sl, :], st_scr[ci], preferred_element_type=F32)
        ec = ecum_scr[sl, :]
        parts = []
        for j in range(hp):
            parts.append(yi[:, j * p:(j + 1) * p] * ec[:, j:j + 1]
                         + yi[:, (hp + j) * p:(hp + j + 1) * p] * ec[:, LANES + hp + j:LANES + hp + j + 1])
        y = acc_scr[sl, :] + jnp.concatenate(parts, axis=1) + dsk * xc_scr[sl, :]
        y = y * _silu(p_scr[sl, 0:gw])
        o_ref[sl, :] = (_rms(y) * nw).astype(o_ref.dtype)


def _ssd_call(hn, w_in_t, p, consts, s0, state_acc, layer, batch, seq_len, width, rows, want_state):
    tl, tu = consts
    has_s0 = s0 is not None
    m = hn.shape[0]
    tokens = 1024
    seqs = tokens // seq_len
    assert seqs == 1 or rows == 1
    n_chunks = tokens // CHUNK
    wcols = 2 * GROUP_W + 2 * SSD_STATE + LANES
    xb = SSD_W // LANES
    z0 = 5 * HGRN_KW // GROUP_W
    x0 = z0 + SSD_W // GROUP_W
    b0 = (5 * HGRN_KW + 2 * SSD_W) // LANES
    c0 = b0 + SSD_GROUPS
    d0 = MAIN_COLS // DT_COLS
    full = lambda arr: pl.BlockSpec(arr.shape, lambda g, b: (0,) * arr.ndim)
    in_specs = [
        pl.BlockSpec((tokens, D_MODEL), lambda g, b: (b, 0)),
        pl.BlockSpec((None, GROUP_W, D_MODEL), lambda g, b: (layer, z0 + g, 0)),
        pl.BlockSpec((None, GROUP_W, D_MODEL), lambda g, b: (layer, x0 + g, 0)),
        pl.BlockSpec((None, LANES, D_MODEL), lambda g, b: (layer, b0 + g, 0)),
        pl.BlockSpec((None, LANES, D_MODEL), lambda g, b: (layer, c0 + g, 0)),
        pl.BlockSpec((None, DT_COLS, D_MODEL), lambda g, b: (layer, d0, 0)),
        pl.BlockSpec((9, GROUP_W), lambda g, b: (0, g)),
        pl.BlockSpec((9, LANES), lambda g, b: (0, xb + g)),
        pl.BlockSpec((9, LANES), lambda g, b: (0, xb + SSD_GROUPS + g)),
        pl.BlockSpec((1, GROUP_W), lambda g, b: (0, g)),
        pl.BlockSpec((1, LANES), lambda g, b: (0, xb + g)),
        pl.BlockSpec((1, LANES), lambda g, b: (0, xb + SSD_GROUPS + g)),
        pl.BlockSpec((None, 1, LANES), lambda g, b: (g, 0, 0)),
        pl.BlockSpec((None, 2 * HEADS_PER_GROUP, LANES), lambda g, b: (g, 0, 0)),
        pl.BlockSpec((None, 1, LANES), lambda g, b: (g, 0, 0)),
        pl.BlockSpec((1, GROUP_W), lambda g, b: (0, g)),
        pl.BlockSpec((1, GROUP_W), lambda g, b: (0, g)),
        full(tl), full(tu),
    ]
    args = [hn, w_in_t, w_in_t, w_in_t, w_in_t, w_in_t, p['conv_w'], p['conv_w'], p['conv_w'],
            p['conv_b'], p['conv_b'], p['conv_b'],
            p['a_log_rows'], p['a_log_cols'], p['dt_bias_rows'], p['d_rows'], p['norm_w'], tl, tu]
    state_spec = pl.BlockSpec((seqs, None, 2, HEADS_PER_GROUP, SSD_HEADDIM, SSD_STATE),
                              lambda g, b: (b, layer, 0, g, 0, 0))
    if has_s0:
        in_specs.append(state_spec)
        args.append(s0)
    out_specs = [pl.BlockSpec((tokens, GROUP_W), lambda g, b: (b, g))]
    out_shape = [jax.ShapeDtypeStruct((m, SSD_W), BF16)]
    aliases = {}
    if want_state:
        out_specs.append(state_spec)
        out_shape.append(jax.ShapeDtypeStruct((batch, DEPTH, 2, SSD_HEADS, SSD_HEADDIM, SSD_STATE), F32))
        if state_acc is not None:
            aliases = {len(args): 1}
            in_specs.append(pl.BlockSpec(memory_space=pl.ANY))
            args.append(state_acc)
    hpp = HEADS_PER_GROUP * SSD_HEADDIM
    scratch = [pltpu.VMEM((wcols, D_MODEL), BF16),
               pltpu.VMEM((tokens, wcols), F32),
               pltpu.VMEM((tokens, GROUP_W), F32),
               pltpu.VMEM((tokens, SSD_STATE), F32),
               pltpu.VMEM((tokens, SSD_STATE), BF16),
               pltpu.VMEM((tokens, LANES), F32),
               pltpu.VMEM((tokens, GROUP_W), F32),
               pltpu.VMEM((tokens, 2 * LANES), F32),
               pltpu.VMEM((n_chunks, 2, SSD_STATE, hpp), F32),
               pltpu.VMEM((n_chunks, 8, LANES), F32),
               pltpu.VMEM((n_chunks, SSD_STATE, 2 * hpp), BF16)]
    res = pl.pallas_call(
        functools.partial(_ssd_kernel, seq_len=seq_len, seqs=seqs, width=width, rows=rows,
                          has_s0=has_s0, has_acc=bool(aliases), want_state=want_state),
        grid=(SSD_GROUPS, m // tokens),
        in_specs=in_specs, out_specs=out_specs, out_shape=out_shape,
        scratch_shapes=scratch,
        input_output_aliases=aliases,
        compiler_params=_cparams(("arbitrary", "arbitrary")),
        name="ssd",
    )(*args)
    return (res[0], res[1]) if want_state else (res[0], None)


def _outup_kernel(x_ref, oh_ref, os_ref, g1_ref, sh_ref, sc_ref, nw_ref, wo_ref, wu_ref,
                  x1_ref, u_ref, h_scr):
    @pl.when(pl.program_id(1) == 0)
    def _():
        mix = jnp.dot(oh_ref[...], wo_ref[0:HGRN_W, :], preferred_element_type=F32)
        mix = mix + jnp.dot(os_ref[...], wo_ref[HGRN_W:MIX_W, :], preferred_element_type=F32)
        x1 = x_ref[...] + g1_ref[...] * mix
        x1_ref[...] = x1
        h = _rms(x1) * nw_ref[...] * (1.0 + sc_ref[...]) + sh_ref[...]
        h_scr[...] = h.astype(BF16)

    u_ref[...] = jnp.dot(h_scr[...], wu_ref[...], preferred_element_type=F32).astype(u_ref.dtype)


def _outup_call(x2, o_h, o_s, mod3, nw, w_out, w_up, layer, seq_len):
    m = x2.shape[0]
    tm, tn = 1024, 512
    bc = mod3.shape[0]
    seq_of = (lambda i: (i * tm) // seq_len) if bc > 1 else (lambda i: 0)
    modspec = lambda part: pl.BlockSpec((None, 1, D_MODEL), lambda i, j, part=part: (seq_of(i), 0, part))
    return pl.pallas_call(
        _outup_kernel,
        grid=(m // tm, 2 * D_FF // tn),
        in_specs=[
            pl.BlockSpec((tm, D_MODEL), lambda i, j: (i, 0)),
            pl.BlockSpec((tm, HGRN_W), lambda i, j: (i, 0)),
            pl.BlockSpec((tm, SSD_W), lambda i, j: (i, 0)),
            modspec(2), modspec(3), modspec(4),
            pl.BlockSpec((1, D_MODEL), lambda i, j: (0, 0)),
            pl.BlockSpec((None, MIX_W, D_MODEL), lambda i, j: (layer, 0, 0)),
            pl.BlockSpec((None, D_MODEL, tn), lambda i, j: (layer, 0, j)),
        ],
        out_specs=[
            pl.BlockSpec((tm, D_MODEL), lambda i, j: (i, 0)),
            pl.BlockSpec((tm, tn), lambda i, j: (i, j)),
        ],
        out_shape=[jax.ShapeDtypeStruct((m, D_MODEL), F32),
                   jax.ShapeDtypeStruct((m, 2 * D_FF), BF16)],
        scratch_shapes=[pltpu.VMEM((tm, D_MODEL), BF16)],
        compiler_params=_cparams(("arbitrary", "arbitrary")),
        name="outup",
    )(x2, o_h, o_s, mod3, mod3, mod3, nw, w_out, w_up)


def _ffndown_kernel(*refs, width, rows, final):
    it = iter(refs)
    x1_ref, ug_ref, uv_ref, g2_ref, cwg_ref, cwv_ref, cbg_ref, cbv_ref, wd_ref, nw_ref = (
        next(it) for _ in range(10))
    sh_ref, sc_ref = (None, None) if final else (next(it), next(it))
    o_ref = next(it)
    hn_ref = None if final else next(it)
    acc_scr, act_scr = next(it), next(it)
    k = pl.program_id(1)

    for s in range(ug_ref.shape[1] // LANES):
        cs = slice(s * LANES, (s + 1) * LANES)
        gate = _dwconv(ug_ref[:, cs].astype(F32), cwg_ref[:, cs], cbg_ref[:, cs], width, rows)
        val = _dwconv(uv_ref[:, cs].astype(F32), cwv_ref[:, cs], cbv_ref[:, cs], width, rows)
        act_scr[:, cs] = (_silu(gate) * val).astype(BF16)
    part = jnp.dot(act_scr[...], wd_ref[...], preferred_element_type=F32)

    @pl.when(k == 0)
    def _():
        acc_scr[...] = part

    @pl.when(k > 0)
    def _():
        acc_scr[...] += part

    @pl.when(k == pl.num_programs(1) - 1)
    def _():
        x2 = x1_ref[...] + g2_ref[...] * acc_scr[...]
        if final:
            o_ref[...] = _rms(x2) * nw_ref[...]
        else:
            o_ref[...] = x2
            hn_ref[...] = (_rms(x2) * nw_ref[...] * (1.0 + sc_ref[...]) + sh_ref[...]).astype(BF16)


def _ffndown_call(x1, u, mod3, conv_w, conv_b, w_down, layer, nw, mod3_next, seq_len, width, rows):
    final = mod3_next is None
    m = x1.shape[0]
    tokens = 1024
    assert tokens == seq_len or rows == 1
    nk = 2
    tk = D_FF // nk
    bc = mod3.shape[0]
    seq_of = (lambda b: (b * tokens) // seq_len) if bc > 1 else (lambda b: 0)
    row_spec = pl.BlockSpec((tokens, D_MODEL), lambda b, k: (b, 0))
    modspec = lambda part: pl.BlockSpec((None, 1, D_MODEL), lambda b, k, part=part: (seq_of(b), 0, part))
    in_specs = [
        row_spec,
        pl.BlockSpec((tokens, tk), lambda b, k: (b, k)),
        pl.BlockSpec((tokens, tk), lambda b, k: (b, nk + k)),
        modspec(5),
        pl.BlockSpec((9, tk), lambda b, k: (0, k)),
        pl.BlockSpec((9, tk), lambda b, k: (0, nk + k)),
        pl.BlockSpec((1, tk), lambda b, k: (0, k)),
        pl.BlockSpec((1, tk), lambda b, k: (0, nk + k)),
        pl.BlockSpec((None, tk, D_MODEL), lambda b, k: (layer, k, 0)),
        pl.BlockSpec((1, D_MODEL), lambda b, k: (0, 0)),
    ]
    args = [x1, u, u, mod3, conv_w, conv_w, conv_b, conv_b, w_down, nw]
    out_specs = [row_spec]
    out_shape = [jax.ShapeDtypeStruct((m, D_MODEL), F32)]
    if not final:
        in_specs += [modspec(0), modspec(1)]
        args += [mod3_next, mod3_next]
        out_specs.append(row_spec)
        out_shape.append(jax.ShapeDtypeStruct((m, D_MODEL), BF16))
    res = pl.pallas_call(
        functools.partial(_ffndown_kernel, width=width, rows=rows, final=final),
        grid=(m // tokens, nk),
        in_specs=in_specs, out_specs=out_specs, out_shape=out_shape,
        scratch_shapes=[pltpu.VMEM((tokens, D_MODEL), F32), pltpu.VMEM((tokens, tk), BF16)],
        compiler_params=_cparams(("arbitrary", "arbitrary")),
        name="ffndown",
    )(*args)
    return (res[0], None) if final else (res[0], res[1])


def _ssd_param_rows(a_log, dt_bias, d_skip):
    def per_group(v):
        return v.reshape(2, SSD_GROUPS, HEADS_PER_GROUP).transpose(1, 0, 2).reshape(SSD_GROUPS, 2 * HEADS_PER_GROUP)

    def rows(v):
        return jnp.pad(per_group(v), ((0, 0), (0, LANES - 2 * HEADS_PER_GROUP))).reshape(SSD_GROUPS, 1, LANES)
    a_cols = jnp.broadcast_to(per_group(a_log)[:, :, None], (SSD_GROUPS, 2 * HEADS_PER_GROUP, LANES))
    return rows(a_log), a_cols, rows(dt_bias), jnp.repeat(d_skip, SSD_HEADDIM).reshape(1, SSD_W)


def _run_pass(x, mod_all_rows, s_h0, s_s0, layers, hgrn_lb, final_norm_w, consts, width, rows, want_state):
    batch, seq_len, _ = x.shape
    m = batch * seq_len
    x2 = x.reshape(m, D_MODEL)
    tl, tu = consts[0], consts[1]
    new_h, new_s = None, None
    hn = _norm_call(x2, mod_all_rows[0], layers[0]['norm_w1'], seq_len)
    for l, p in enumerate(layers):
        mod3 = mod_all_rows[l]
        last = l == DEPTH - 1
        o_h, new_h = _hgrn_call(hn, p['w_in'], hgrn_lb, p['hgrn_norm_w'], consts,
                                s_h0, new_h, l, batch, seq_len, want_state)
        o_s, new_s = _ssd_call(hn, p['w_in'], p['ssd'], (tl, tu),
                               s_s0, new_s, l, batch, seq_len, width, rows, want_state)
        x1, u = _outup_call(x2, o_h, o_s, mod3, p['norm_w2'], p['w_out'], p['ffn_up'], l, seq_len)
        x2, hn = _ffndown_call(x1, u, mod3, p['ffn_conv_w'], p['ffn_conv_b'], p['ffn_down'], l,
                               final_norm_w if last else layers[l + 1]['norm_w1'],
                               None if last else mod_all_rows[l + 1], seq_len, width, rows)
    return x2.reshape(batch, seq_len, D_MODEL), new_h, new_s


def kernel(x_prompt, x_sample, c, state_hgrn, state_ssd, c_ctx, norm_w, final_norm_w, w_ada, b_ada,
           w_in, w_out, hgrn_lb, hgrn_norm_w, ssd_conv_w, ssd_conv_b, ssd_a_log, ssd_dt_bias, ssd_d,
           ssd_norm_w, ffn_up, ffn_conv_w, ffn_conv_b, ffn_down):
    dec_batch = c.shape[0]
    consts = _scan_constants() + (_level_signs(),)

    cond8 = jnp.concatenate([c_ctx[None], c, jnp.zeros((8 - 1 - dec_batch, D_MODEL), F32)], axis=0)
    mod_all = _mod_call(cond8, w_ada, b_ada)
    mod_ctx = [mod_all[l, 0:1].reshape(1, 1, 6 * D_MODEL) for l in range(DEPTH)]
    mod_lat = [mod_all[l, 1:1 + dec_batch].reshape(dec_batch, 1, 6 * D_MODEL) for l in range(DEPTH)]

    w_in_t = jnp.swapaxes(w_in, 1, 2)
    w_out_b, ffn_up_b, ffn_down_b = w_out.astype(BF16), ffn_up.astype(BF16), ffn_down.astype(BF16)
    layers = []
    for l in range(DEPTH):
        a_rows, a_cols, b_rows, d_rows = _ssd_param_rows(ssd_a_log[l], ssd_dt_bias[l], ssd_d[l])
        layers.append(dict(
            norm_w1=norm_w[l, 0].reshape(1, D_MODEL), norm_w2=norm_w[l, 1].reshape(1, D_MODEL),
            w_in=w_in_t,
            w_out=w_out_b,
            hgrn_norm_w=hgrn_norm_w[l].reshape(1, HGRN_DV),
            ssd=dict(conv_w=ssd_conv_w[l].reshape(9, CONV_CH), conv_b=ssd_conv_b[l].reshape(1, CONV_CH),
                     a_log_rows=a_rows, a_log_cols=a_cols, dt_bias_rows=b_rows, d_rows=d_rows,
                     norm_w=ssd_norm_w[l].reshape(1, SSD_W)),
            ffn_up=ffn_up_b,
            ffn_conv_w=ffn_conv_w[l].reshape(9, 2 * D_FF), ffn_conv_b=ffn_conv_b[l].reshape(1, 2 * D_FF),
            ffn_down=ffn_down_b,
        ))
    fnw = final_norm_w.reshape(1, D_MODEL)

    y_prompt, new_h, new_s = _run_pass(x_prompt, mod_ctx, None, None, layers, hgrn_lb, fnw, consts,
                                       width=x_prompt.shape[1], rows=1, want_state=True)
    y_sample, _, _ = _run_pass(x_sample, mod_lat, state_hgrn, state_ssd, layers, hgrn_lb, fnw, consts,
                               width=GRID_W, rows=x_sample.shape[1] // GRID_W, want_state=False)
    return (y_prompt, y_sample, new_h, new_s)
```

```python
import functools
import math

import numpy as np
import jax
import jax.numpy as jnp
from jax import lax
from jax.experimental import pallas as pl
from jax.experimental.pallas import tpu as pltpu

F32 = jnp.float32
BF16 = jnp.bfloat16

D_MODEL = 1024
DEPTH = 2
GRID_W = 64
HGRN_HEADS = 8
HGRN_DK = 128
HGRN_DV = 128
HGRN_KW = HGRN_HEADS * HGRN_DK
HGRN_W = HGRN_HEADS * HGRN_DV
SSD_W = 1024
SSD_HEADDIM = 64
SSD_HEADS = 16
SSD_GROUPS = 4
SSD_STATE = 128
HEADS_PER_GROUP = SSD_HEADS // SSD_GROUPS
GROUP_W = SSD_W // SSD_GROUPS
MIX_W = HGRN_W + SSD_W
CONV_CH = SSD_W + 2 * SSD_GROUPS * SSD_STATE
D_FF = 2816
MAIN_COLS = 3 * HGRN_KW + 2 * HGRN_W + SSD_W + CONV_CH
DT_COLS = 2 * SSD_HEADS
EPS = 1e-6
LOG2E = math.log2(math.e)

LANES = 128
CHUNK = 128
N_LEVELS = 7
VMEM_LIMIT = 56 * 1024 * 1024


def _cparams(sem):
    return pltpu.CompilerParams(dimension_semantics=sem, vmem_limit_bytes=VMEM_LIMIT)


def _sigmoid(x):
    return 1.0 / (1.0 + jnp.exp(-x))


def _silu(x):
    return x * _sigmoid(x)


def _softplus(x):
    return jnp.maximum(x, 0.0) + jnp.log1p(jnp.exp(-jnp.abs(x)))


def _rms(x):
    return x * lax.rsqrt(jnp.mean(x * x, axis=-1, keepdims=True) + EPS)


def _split3(x):
    hi = x.astype(BF16)
    r = x - hi.astype(F32)
    mid = r.astype(BF16)
    lo = (r - mid.astype(F32)).astype(BF16)
    return hi, mid, lo


def _sum_rows_exact(w01, x):
    n = x.shape[1]
    hi, mid, lo = _split3(x)
    p = jnp.dot(w01, jnp.concatenate([hi, mid, lo], axis=1), preferred_element_type=F32)
    return p[:, :n] + p[:, n:2 * n] + p[:, 2 * n:]


def _sum_cols_exact(x, w01):
    m = x.shape[0]
    hi, mid, lo = _split3(x)
    p = jnp.dot(jnp.concatenate([hi, mid, lo], axis=0), w01, preferred_element_type=F32)
    return p[:m] + p[m:2 * m] + p[2 * m:]


def _dot_nt(a, b):
    return lax.dot_general(a, b, (((1,), (1,)), ((), ())), preferred_element_type=F32)


def _dot_tn(a, b):
    return lax.dot_general(a, b, (((0,), (0,)), ((), ())), preferred_element_type=F32)


def _dwconv(x, w_ref, b_ref, width, rows):
    seq, _ = x.shape
    t = lax.broadcasted_iota(jnp.int32, x.shape, 0)
    col = jnp.bitwise_and(t, width - 1)
    xl = jnp.where(col == 0, 0.0, pltpu.roll(x, 1, axis=0))
    xr = jnp.where(col == width - 1, 0.0, pltpu.roll(x, seq - 1, axis=0))

    def hrow(kh, a, b, c):
        return w_ref[3 * kh:3 * kh + 1, :] * a + w_ref[3 * kh + 1:3 * kh + 2, :] * b \
            + w_ref[3 * kh + 2:3 * kh + 3, :] * c

    out = b_ref[0:1, :] + hrow(1, xl, x, xr)
    if rows > 1:
        inner = seq - width
        up = hrow(0, xl[:inner], x[:inner], xr[:inner])
        out = jnp.concatenate([out[:width], out[width:] + up], axis=0)
        dn = hrow(2, xl[width:], x[width:], xr[width:])
        out = jnp.concatenate([out[:inner] + dn, out[inner:]], axis=0)
    return out


def _scan_constants():
    c = CHUNK
    t = np.arange(c)[:, None]
    i = np.arange(c)[None, :]
    x = t ^ i
    lv = np.where(x == 0, 0, np.floor(np.log2(np.maximum(x, 1))).astype(np.int64) + 1)
    lv = np.where(i > t, -lv, lv)
    return jnp.asarray(i <= t, BF16), jnp.asarray(i >= t, BF16), jnp.asarray(lv, jnp.int32)


def _level_signs():
    t = np.arange(CHUNK)[:, None]
    sg = [np.where(((t >> lev) & 1) == 1, 1.0, -1.0) * np.ones((1, LANES)) for lev in range(N_LEVELS)]
    return jnp.asarray(np.concatenate(sg, 0), F32)


def _mod_kernel(cond_ref, w_ref, b_ref, o_ref):
    a = _silu(cond_ref[...]).astype(BF16)
    o_ref[...] = jnp.dot(a, w_ref[...].astype(BF16), preferred_element_type=F32) + b_ref[...]


def _mod_call(cond8, w_ada, b_ada):
    n = 6 * D_MODEL
    tn = D_MODEL
    return pl.pallas_call(
        _mod_kernel,
        grid=(DEPTH, n // tn),
        in_specs=[
            pl.BlockSpec((8, D_MODEL), lambda l, j: (0, 0)),
            pl.BlockSpec((None, D_MODEL, tn), lambda l, j: (l, 0, j)),
            pl.BlockSpec((None, 1, tn), lambda l, j: (l, 0, j)),
        ],
        out_specs=pl.BlockSpec((None, 8, tn), lambda l, j: (l, 0, j)),
        out_shape=jax.ShapeDtypeStruct((DEPTH, 8, n), F32),
        compiler_params=_cparams(("arbitrary", "arbitrary")),
        name="mod",
    )(cond8, w_ada, b_ada.reshape(DEPTH, 1, n))


def _norm_kernel(x_ref, sh_ref, sc_ref, nw_ref, hn_ref):
    h = _rms(x_ref[...]) * nw_ref[...] * (1.0 + sc_ref[...]) + sh_ref[...]
    hn_ref[...] = h.astype(BF16)


def _norm_call(x2, mod3, nw, seq_len):
    m = x2.shape[0]
    tm = 512
    bc = mod3.shape[0]
    seq_of = (lambda i: (i * tm) // seq_len) if bc > 1 else (lambda i: 0)
    return pl.pallas_call(
        _norm_kernel,
        grid=(m // tm,),
        in_specs=[
            pl.BlockSpec((tm, D_MODEL), lambda i: (i, 0)),
            pl.BlockSpec((None, 1, D_MODEL), lambda i: (seq_of(i), 0, 0)),
            pl.BlockSpec((None, 1, D_MODEL), lambda i: (seq_of(i), 0, 1)),
            pl.BlockSpec((1, D_MODEL), lambda i: (0, 0)),
        ],
        out_specs=pl.BlockSpec((tm, D_MODEL), lambda i: (i, 0)),
        out_shape=jax.ShapeDtypeStruct((m, D_MODEL), BF16),
        compiler_params=_cparams(("arbitrary",)),
        name="norm",
    )(x2, mod3, mod3, nw)


def _hgrn_gates(f_pre, lbd):
    ea = jnp.exp(-jnp.abs(f_pre))
    log2_sig = jnp.minimum(f_pre, 0.0) * LOG2E - jnp.log2(1.0 + ea)
    sig_neg = jnp.where(f_pre >= 0.0, ea, 1.0) / (1.0 + ea)
    if lbd is None:
        return log2_sig, sig_neg
    a = jnp.log2(lbd)
    b = jnp.log2(1.0 - lbd) + log2_sig
    log2_f = jnp.maximum(a, b) + jnp.log2(1.0 + jnp.exp2(-jnp.abs(a - b)))
    return log2_f, (1.0 - lbd) * sig_neg


def _block_mid(scr, m):
    return jnp.concatenate(
        [jnp.broadcast_to(scr[r0 + m - 1:r0 + m, :], (2 * m, scr.shape[1])) for r0 in range(0, CHUNK, 2 * m)],
        axis=0)


def _mid_distance(c, scr, m):
    pieces = []
    for r0 in range(0, CHUNK, 2 * m):
        mid = jnp.broadcast_to(scr[r0 + m - 1:r0 + m, :], (m, scr.shape[1]))
        pieces += [mid - c[r0:r0 + m], c[r0 + m:r0 + 2 * m] - mid]
    return jnp.concatenate(pieces, axis=0)


def _pick_halves(up_val, low_val, m):
    pieces = []
    for r0 in range(0, CHUNK, 2 * m):
        pieces += [low_val[r0:r0 + m], up_val[r0 + m:r0 + 2 * m]]
    return jnp.concatenate(pieces, axis=0)


def _hgrn_kernel(*refs, layer, seq_len, seqs, has_s0, has_acc, want_state):
    it = iter(refs)
    hn_ref = next(it)
    w_refs = [next(it) for _ in range(5)]
    lb_ref, nw_ref, tl_ref, lv_ref, sg_ref = (next(it) for _ in range(5))
    s0_ref = next(it) if has_s0 else None
    if has_acc:
        next(it)
    o_ref = next(it)
    sf_ref = next(it) if want_state else None
    (w_scr, p_scr, acc_scr, qc_scr, u_scr, dl_scr, st_scr, cumf_scr, cumb_scr) = (next(it) for _ in range(9))

    c = CHUNK
    dk = HGRN_DK
    n_chunks = seqs * seq_len // c
    cps = seq_len // c

    @pl.when(pl.program_id(1) == 0)
    def _():
        for i, w_ref in enumerate(w_refs):
            w_scr[i * LANES:(i + 1) * LANES, :] = w_ref[...].astype(BF16)

    p_scr[...] = _dot_nt(hn_ref[...], w_scr[...])

    lb = None
    if layer > 0:
        lbr = lb_ref[...]
        e = jnp.exp(lbr - jnp.max(lbr, axis=0, keepdims=True))
        sm = e / jnp.sum(e, axis=0, keepdims=True)
        lb = sm[1]
        for i in range(2, layer + 1):
            lb = lb + sm[i]

    lvs = lv_ref[...]
    lv = jnp.abs(lvs)
    tl = tl_ref[...]
    zero = jnp.zeros((c, dk), F32)
    odd = jnp.bitwise_and(lax.broadcasted_iota(jnp.int32, (c, dk), 0), 1) == 1

    def intra(ci, carry):
        r0 = pl.multiple_of(ci * c, c)
        sl = pl.ds(r0, c)
        q = _silu(p_scr[sl, 0:dk]) * (dk ** -0.5)
        lf_f, kf = _hgrn_gates(p_scr[sl, dk:2 * dk], None if lb is None else lb[0:1, :])
        lf_b, kb = _hgrn_gates(p_scr[sl, 2 * dk:3 * dk], None if lb is None else lb[1:2, :])
        vb = p_scr[sl, 3 * dk:4 * dk].astype(BF16)

        cum_f = _sum_rows_exact(tl, lf_f)
        cum_b = _sum_rows_exact(tl, lf_b)
        cx_b = cum_b - lf_b
        cumf_scr[...] = cum_f
        cumb_scr[...] = cum_b
        tot_f = cum_f[c - 1:c, :]
        tot_b = cum_b[c - 1:c, :]

        a_sum = _dot_nt(q.astype(BF16), (kf + kb).astype(BF16))

        f_f = jnp.exp2(lf_f)
        f_b = jnp.exp2(lf_b)
        qf = q * f_f
        qb = q * f_b
        p_f = _dot_nt(qf.astype(BF16), kf.astype(BF16))
        p_b = _dot_nt(qb.astype(BF16), kb.astype(BF16))
        a_sum = jnp.where(lvs == 1, p_f, jnp.where(lvs == -1, p_b, a_sum))
        p_f = _dot_nt(jnp.where(odd, qf * pltpu.roll(f_f, 1, axis=0), qf).astype(BF16),
                      jnp.where(odd, kf, kf * pltpu.roll(f_f, c - 1, axis=0)).astype(BF16))
        p_b = _dot_nt(jnp.where(odd, qb, qb * pltpu.roll(f_b, c - 1, axis=0)).astype(BF16),
                      jnp.where(odd, kb * pltpu.roll(f_b, 1, axis=0), kb).astype(BF16))
        a_sum = jnp.where(lvs == 2, p_f, jnp.where(lvs == -2, p_b, a_sum))

        sgn = sg_ref[2 * c:3 * c, :]
        ef = jnp.exp2((cum_f - _block_mid(cumf_scr, 4)) * sgn)
        eb = jnp.exp2((cx_b - _block_mid(cumb_scr, 4)) * sgn)
        p_f = _dot_nt((q * ef).astype(BF16), (kf * ef).astype(BF16))
        p_b = _dot_nt((q * eb).astype(BF16), (kb * eb).astype(BF16))
        a_sum = jnp.where(lvs == 3, p_f, jnp.where(lvs == -3, p_b, a_sum))

        for lev in range(4, N_LEVELS + 1):
            m = 1 << (lev - 1)
            ef = jnp.exp2(_mid_distance(cum_f, cumf_scr, m))
            eb = jnp.exp2(_mid_distance(cx_b, cumb_scr, m))
            qe = q * _pick_halves(ef, eb, m)
            ke = _pick_halves(kb, kf, m) * _pick_halves(eb, ef, m)
            lhs = jnp.concatenate([_pick_halves(qe, zero, m), _pick_halves(zero, qe, m)], axis=1)
            rhs = jnp.concatenate([_pick_halves(zero, ke, m), _pick_halves(ke, zero, m)], axis=1)
            a_sum = jnp.where(lv == lev, _dot_nt(lhs.astype(BF16), rhs.astype(BF16)), a_sum)
        acc_scr[sl, :] = jnp.dot(a_sum.astype(BF16), vb, preferred_element_type=F32)

        qc_scr[sl, :] = jnp.concatenate(
            [q * jnp.exp2(cum_f), q * jnp.exp2(jnp.minimum(tot_b - cx_b, 0.0))], axis=1).astype(BF16)
        u_scr[ci, 0] = _dot_tn(vb, (kf * jnp.exp2(jnp.minimum(tot_f - cum_f, 0.0))).astype(BF16))
        u_scr[ci, 1] = _dot_tn(vb, (kb * jnp.exp2(jnp.minimum(cx_b, 0.0))).astype(BF16))
        dl_scr[ci, 0:1, :] = jnp.exp2(tot_f)
        dl_scr[ci, 1:2, :] = jnp.exp2(tot_b)
        return carry

    lax.fori_loop(0, n_chunks, intra, 0, unroll=4)

    def init(s, d):
        return s0_ref[s, d].T if has_s0 else jnp.zeros((HGRN_DV, dk), F32)

    st = None
    for ci in range(n_chunks):
        s = ci // cps
        if ci % cps == 0:
            st = init(s, 0)
        st_scr[ci, :, 0:dk] = st.astype(BF16)
        st = st * dl_scr[ci, 0:1, :] + u_scr[ci, 0]
        if want_state and ci % cps == cps - 1:
            sf_ref[s, 0] = st.T
    for ci in reversed(range(n_chunks)):
        s = ci // cps
        if ci % cps == cps - 1:
            st = init(s, 1)
        st_scr[ci, :, dk:2 * dk] = st.astype(BF16)
        st = st * dl_scr[ci, 1:2, :] + u_scr[ci, 1]
        if want_state and ci % cps == 0:
            sf_ref[s, 1] = st.T

    nw = nw_ref[...]

    for ci in range(n_chunks):
        sl = slice(ci * c, (ci + 1) * c)
        o = acc_scr[sl, :] + _dot_nt(qc_scr[sl, :], st_scr[ci])
        o_ref[sl, :] = (_rms(o) * nw * _silu(p_scr[sl, 4 * dk:5 * dk])).astype(o_ref.dtype)


def _hgrn_call(hn, w_in_t, lb_raw, nw, consts, s0, state_acc, layer, batch, seq_len, want_state):
    tl, _, lv, sg = consts
    has_s0 = s0 is not None
    m = hn.shape[0]
    tokens = 1024
    seqs = tokens // seq_len
    n_chunks = tokens // CHUNK
    full = lambda arr: pl.BlockSpec(arr.shape, lambda h, b: (0,) * arr.ndim)
    state_spec = pl.BlockSpec((seqs, None, 2, None, HGRN_DK, HGRN_DV), lambda h, b: (b, layer, 0, h, 0, 0))
    wcol = lambda part: pl.BlockSpec((None, LANES, D_MODEL),
                                     lambda h, b, part=part: (layer, part * HGRN_HEADS + h, 0))
    in_specs = [pl.BlockSpec((tokens, D_MODEL), lambda h, b: (b, 0)),
                wcol(0), wcol(1), wcol(2), wcol(3), wcol(4),
                pl.BlockSpec((DEPTH, 2, LANES), lambda h, b: (0, 0, h)),
                pl.BlockSpec((1, HGRN_DV), lambda h, b: (0, 0)),
                full(tl), full(lv), full(sg)]
    args = [hn, w_in_t, w_in_t, w_in_t, w_in_t, w_in_t, lb_raw, nw, tl, lv, sg]
    if has_s0:
        in_specs.append(state_spec)
        args.append(s0)
    out_specs = [pl.BlockSpec((tokens, LANES), lambda h, b: (b, h))]
    out_shape = [jax.ShapeDtypeStruct((m, HGRN_W), BF16)]
    aliases = {}
    if want_state:
        out_specs.append(state_spec)
        out_shape.append(jax.ShapeDtypeStruct((batch, DEPTH, 2, HGRN_HEADS, HGRN_DK, HGRN_DV), F32))
        if state_acc is not None:
            aliases = {len(args): 1}
            in_specs.append(pl.BlockSpec(memory_space=pl.ANY))
            args.append(state_acc)
    scratch = [pltpu.VMEM((5 * LANES, D_MODEL), BF16),
               pltpu.VMEM((tokens, 5 * LANES), F32),
               pltpu.VMEM((tokens, HGRN_DV), F32),
               pltpu.VMEM((tokens, 2 * HGRN_DK), BF16),
               pltpu.VMEM((n_chunks, 2, HGRN_DV, HGRN_DK), F32),
               pltpu.VMEM((n_chunks, 8, HGRN_DK), F32),
               pltpu.VMEM((n_chunks, HGRN_DV, 2 * HGRN_DK), BF16),
               pltpu.VMEM((CHUNK, HGRN_DK), F32),
               pltpu.VMEM((CHUNK, HGRN_DK), F32)]
    res = pl.pallas_call(
        functools.partial(_hgrn_kernel, layer=layer, seq_len=seq_len, seqs=seqs, has_s0=has_s0,
                          has_acc=bool(aliases), want_state=want_state),
        grid=(HGRN_HEADS, m // tokens),
        in_specs=in_specs, out_specs=out_specs, out_shape=out_shape,
        scratch_shapes=scratch,
        input_output_aliases=aliases,
        compiler_params=_cparams(("arbitrary", "arbitrary")),
        name="hgrn",
    )(*args)
    return (res[0], res[1]) if want_state else (res[0], None)


def _ssd_kernel(*refs, seq_len, seqs, width, rows, has_s0, has_acc, want_state):
    it = iter(refs)
    (hn_ref, wz_ref, wx_ref, wb_ref, wc_ref, wdt_ref, cwx_ref, cwb_ref, cwc_ref, cbx_ref, cbb_ref, cbc_ref,
     alog_ref, acol_ref, dtb_ref, dsk_ref, nw_ref, tl_ref, tu_ref) = (next(it) for _ in range(19))
    s0_ref = next(it) if has_s0 else None
    if has_acc:
        next(it)
    o_ref = next(it)
    sf_ref = next(it) if want_state else None
    (w_scr, p_scr, xc_scr, bc_scr, cc_scr, dtg_scr, acc_scr, ecum_scr, u_scr, dl_scr, st_scr) = (
        next(it) for _ in range(11))

    q = CHUNK
    n_chunks = seqs * seq_len // q
    cps = seq_len // q
    hp = HEADS_PER_GROUP
    p = SSD_HEADDIM
    gw = GROUP_W
    n = SSD_STATE
    dt0 = 2 * gw + 2 * n

    @pl.when(pl.program_id(1) == 0)
    def _():
        w_scr[0:gw, :] = wz_ref[...].astype(BF16)
        w_scr[gw:2 * gw, :] = wx_ref[...].astype(BF16)
        w_scr[2 * gw:2 * gw + n, :] = wb_ref[...].astype(BF16)
        w_scr[2 * gw + n:2 * gw + 2 * n, :] = wc_ref[...].astype(BF16)
        w_scr[dt0:dt0 + DT_COLS, :] = wdt_ref[...].astype(BF16)
        w_scr[dt0 + DT_COLS:dt0 + LANES, :] = jnp.zeros((LANES - DT_COLS, D_MODEL), BF16)

    hn = hn_ref[...]
    p_scr[...] = _dot_nt(hn, w_scr[0:gw, :])
    px = _dot_nt(hn, w_scr[gw:2 * gw, :])
    for s in range(gw // LANES):
        cs = slice(s * LANES, (s + 1) * LANES)
        xc_scr[:, cs] = _silu(_dwconv(px[:, cs], cwx_ref[:, cs], cbx_ref[:, cs], width, rows))
    pbc = _dot_nt(hn, w_scr[2 * gw:2 * gw + 2 * n, :])
    bc_scr[...] = _silu(_dwconv(pbc[:, 0:n], cwb_ref, cbb_ref, width, rows))
    cc_scr[...] = _silu(_dwconv(pbc[:, n:2 * n], cwc_ref, cbc_ref, width, rows)).astype(BF16)
    g = pl.program_id(0)
    dt_all = _dot_nt(hn, w_scr[dt0:dt0 + LANES, :])
    lane = lax.broadcasted_iota(jnp.int32, dt_all.shape, 1)
    dt_fw = pltpu.roll(dt_all, jnp.bitwise_and(LANES - hp * g, LANES - 1), axis=1)
    dt_bw = pltpu.roll(dt_all, jnp.bitwise_and(LANES - (SSD_HEADS - hp) - hp * g, LANES - 1), axis=1)
    dtg_scr[...] = _softplus(jnp.where(lane < hp, dt_fw, dt_bw) + dtb_ref[...])
    neg_a = -jnp.exp(alog_ref[...])
    neg_a_t = -jnp.exp(acol_ref[...])

    tl = tl_ref[...]
    tu = tu_ref[...]
    ti = lax.broadcasted_iota(jnp.int32, (q, q), 0)
    si = lax.broadcasted_iota(jnp.int32, (q, q), 1)
    lower = si <= ti
    upper = si >= ti

    def intra(ci, carry):
        r0 = pl.multiple_of(ci * q, q)
        sl = pl.ds(r0, q)
        dt = dtg_scr[sl, :]
        da = dt * neg_a
        cum_f = _sum_rows_exact(tl, da)
        cum_b = _sum_rows_exact(tu, da)
        dt_t = dt.T[0:2 * hp, :]
        da_t = dt_t * neg_a_t
        row_f = _sum_cols_exact(da_t, tu)
        row_b = _sum_cols_exact(da_t, tl)
        wr_f = jnp.exp(row_f[:, q - 1:q] - row_f) * dt_t
        wr_b = jnp.exp(row_b[:, 0:1] - row_b) * dt_t
        xb = xc_scr[sl, :].astype(BF16)
        bm = bc_scr[sl, :]
        bt = bm.T
        gmat = _dot_nt(cc_scr[sl, :], bm.astype(BF16))
        ecum_scr[sl, 0:LANES] = jnp.exp(cum_f)
        ecum_scr[sl, LANES:2 * LANES] = jnp.exp(cum_b)
        ys, u_f, u_b = [], [], []
        for j in range(hp):
            jf, jb = j, hp + j
            dec_f = jnp.exp(jnp.minimum(cum_f[:, jf:jf + 1] - row_f[jf:jf + 1, :], 0.0)) * dt_t[jf:jf + 1, :]
            dec_b = jnp.exp(jnp.minimum(cum_b[:, jb:jb + 1] - row_b[jb:jb + 1, :], 0.0)) * dt_t[jb:jb + 1, :]
            mh = gmat * (jnp.where(lower, dec_f, 0.0) + jnp.where(upper, dec_b, 0.0))
            xh = xb[:, j * p:(j + 1) * p]
            ys.append(jnp.dot(mh.astype(BF16), xh, preferred_element_type=F32))
            u_f.append(jnp.dot((bt * wr_f[jf:jf + 1, :]).astype(BF16), xh, preferred_element_type=F32))
            u_b.append(jnp.dot((bt * wr_b[jb:jb + 1, :]).astype(BF16), xh, preferred_element_type=F32))
        acc_scr[sl, :] = jnp.concatenate(ys, axis=1)
        u_scr[ci, 0] = jnp.concatenate(u_f, axis=1)
        u_scr[ci, 1] = jnp.concatenate(u_b, axis=1)
        dl_scr[ci, 0:1, :] = jnp.exp(cum_f[q - 1:q, :])
        dl_scr[ci, 1:2, :] = jnp.exp(cum_b[0:1, :])
        return carry

    lax.fori_loop(0, n_chunks, intra, 0, unroll=4)

    def head_scale(vec, lane0):
        return jnp.concatenate(
            [jnp.broadcast_to(vec[0:1, lane0 + j:lane0 + j + 1], (1, p)) for j in range(hp)], axis=1)

    def init(s, d):
        return s0_ref[s, d].reshape(hp * p, n).T if has_s0 else jnp.zeros((n, hp * p), F32)

    st = None
    for ci in range(n_chunks):
        s = ci // cps
        if ci % cps == 0:
            st = init(s, 0)
        st_scr[ci, :, 0:hp * p] = st.astype(BF16)
        st = st * head_scale(dl_scr[ci, 0:1, :], 0) + u_scr[ci, 0]
        if want_state and ci % cps == cps - 1:
            sf_ref[s, 0] = st.T.reshape(hp, p, n)
    for ci in reversed(range(n_chunks)):
        s = ci // cps
        if ci % cps == cps - 1:
            st = init(s, 1)
        st_scr[ci, :, hp * p:2 * hp * p] = st.astype(BF16)
        st = st * head_scale(dl_scr[ci, 1:2, :], hp) + u_scr[ci, 1]
        if want_state and ci % cps == 0:
            sf_ref[s, 1] = st.T.reshape(hp, p, n)

    dsk = dsk_ref[...]
    nw = nw_ref[...]
    for ci in range(n_chunks):
        sl = slice(ci * q, (ci + 1) * q)
        yi = jnp.dot(cc_scr[sl, :], st_scr[ci], preferred_element_type=F32)
        ec = ecum_scr[sl, :]
        parts = []
        for j in range(hp):
            parts.append(yi[:, j * p:(j + 1) * p] * ec[:, j:j + 1]
                         + yi[:, (hp + j) * p:(hp + j + 1) * p] * ec[:, LANES + hp + j:LANES + hp + j + 1])
        y = acc_scr[sl, :] + jnp.concatenate(parts, axis=1) + dsk * xc_scr[sl, :]
        y = y * _silu(p_scr[sl, 0:gw])
        o_ref[sl, :] = (_rms(y) * nw).astype(o_ref.dtype)


def _ssd_call(hn, w_in_t, p, consts, s0, state_acc, layer, batch, seq_len, width, rows, want_state):
    tl, tu = consts
    has_s0 = s0 is not None
    m = hn.shape[0]
    tokens = 1024
    seqs = tokens // seq_len
    assert seqs == 1 or rows == 1
    n_chunks = tokens // CHUNK
    wcols = 2 * GROUP_W + 2 * SSD_STATE + LANES
    xb = SSD_W // LANES
    z0 = 5 * HGRN_KW // GROUP_W
    x0 = z0 + SSD_W // GROUP_W
    b0 = (5 * HGRN_KW + 2 * SSD_W) // LANES
    c0 = b0 + SSD_GROUPS
    d0 = MAIN_COLS // DT_COLS
    full = lambda arr: pl.BlockSpec(arr.shape, lambda g, b: (0,) * arr.ndim)
    in_specs = [
        pl.BlockSpec((tokens, D_MODEL), lambda g, b: (b, 0)),
        pl.BlockSpec((None, GROUP_W, D_MODEL), lambda g, b: (layer, z0 + g, 0)),
        pl.BlockSpec((None, GROUP_W, D_MODEL), lambda g, b: (layer, x0 + g, 0)),
        pl.BlockSpec((None, LANES, D_MODEL), lambda g, b: (layer, b0 + g, 0)),
        pl.BlockSpec((None, LANES, D_MODEL), lambda g, b: (layer, c0 + g, 0)),
        pl.BlockSpec((None, DT_COLS, D_MODEL), lambda g, b: (layer, d0, 0)),
        pl.BlockSpec((9, GROUP_W), lambda g, b: (0, g)),
        pl.BlockSpec((9, LANES), lambda g, b: (0, xb + g)),
        pl.BlockSpec((9, LANES), lambda g, b: (0, xb + SSD_GROUPS + g)),
        pl.BlockSpec((1, GROUP_W), lambda g, b: (0, g)),
        pl.BlockSpec((1, LANES), lambda g, b: (0, xb + g)),
        pl.BlockSpec((1, LANES), lambda g, b: (0, xb + SSD_GROUPS + g)),
        pl.BlockSpec((None, 1, LANES), lambda g, b: (g, 0, 0)),
        pl.BlockSpec((None, 2 * HEADS_PER_GROUP, LANES), lambda g, b: (g, 0, 0)),
        pl.BlockSpec((None, 1, LANES), lambda g, b: (g, 0, 0)),
        pl.BlockSpec((1, GROUP_W), lambda g, b: (0, g)),
        pl.BlockSpec((1, GROUP_W), lambda g, b: (0, g)),
        full(tl), full(tu),
    ]
    args = [hn, w_in_t, w_in_t, w_in_t, w_in_t, w_in_t, p['conv_w'], p['conv_w'], p['conv_w'],
            p['conv_b'], p['conv_b'], p['conv_b'],
            p['a_log_rows'], p['a_log_cols'], p['dt_bias_rows'], p['d_rows'], p['norm_w'], tl, tu]
    state_spec = pl.BlockSpec((seqs, None, 2, HEADS_PER_GROUP, SSD_HEADDIM, SSD_STATE),
                              lambda g, b: (b, layer, 0, g, 0, 0))
    if has_s0:
        in_specs.append(state_spec)
        args.append(s0)
    out_specs = [pl.BlockSpec((tokens, GROUP_W), lambda g, b: (b, g))]
    out_shape = [jax.ShapeDtypeStruct((m, SSD_W), BF16)]
    aliases = {}
    if want_state:
        out_specs.append(state_spec)
        out_shape.append(jax.ShapeDtypeStruct((batch, DEPTH, 2, SSD_HEADS, SSD_HEADDIM, SSD_STATE), F32))
        if state_acc is not None:
            aliases = {len(args): 1}
            in_specs.append(pl.BlockSpec(memory_space=pl.ANY))
            args.append(state_acc)
    hpp = HEADS_PER_GROUP * SSD_HEADDIM
    scratch = [pltpu.VMEM((wcols, D_MODEL), BF16),
               pltpu.VMEM((tokens, GROUP_W), F32),
               pltpu.VMEM((tokens, GROUP_W), F32),
               pltpu.VMEM((tokens, SSD_STATE), F32),
               pltpu.VMEM((tokens, SSD_STATE), BF16),
               pltpu.VMEM((tokens, LANES), F32),
               pltpu.VMEM((tokens, GROUP_W), F32),
               pltpu.VMEM((tokens, 2 * LANES), F32),
               pltpu.VMEM((n_chunks, 2, SSD_STATE, hpp), F32),
               pltpu.VMEM((n_chunks, 8, LANES), F32),
               pltpu.VMEM((n_chunks, SSD_STATE, 2 * hpp), BF16)]
    res = pl.pallas_call(
        functools.partial(_ssd_kernel, seq_len=seq_len, seqs=seqs, width=width, rows=rows,
                          has_s0=has_s0, has_acc=bool(aliases), want_state=want_state),
        grid=(SSD_GROUPS, m // tokens),
        in_specs=in_specs, out_specs=out_specs, out_shape=out_shape,
        scratch_shapes=scratch,
        input_output_aliases=aliases,
        compiler_params=_cparams(("arbitrary", "arbitrary")),
        name="ssd",
    )(*args)
    return (res[0], res[1]) if want_state else (res[0], None)


def _outup_kernel(x_ref, oh_ref, os_ref, g1_ref, sh_ref, sc_ref, nw_ref, wo_ref, wu_ref,
                  x1_ref, u_ref, h_scr):
    @pl.when(pl.program_id(1) == 0)
    def _():
        mix = jnp.dot(oh_ref[...], wo_ref[0:HGRN_W, :], preferred_element_type=F32)
        mix = mix + jnp.dot(os_ref[...], wo_ref[HGRN_W:MIX_W, :], preferred_element_type=F32)
        x1 = x_ref[...] + g1_ref[...] * mix
        x1_ref[...] = x1
        h = _rms(x1) * nw_ref[...] * (1.0 + sc_ref[...]) + sh_ref[...]
        h_scr[...] = h.astype(BF16)

    u_ref[...] = jnp.dot(h_scr[...], wu_ref[...], preferred_element_type=F32).astype(u_ref.dtype)


def _outup_call(x2, o_h, o_s, mod3, nw, w_out, w_up, layer, seq_len):
    m = x2.shape[0]
    tm, tn = 1024, 512
    bc = mod3.shape[0]
    seq_of = (lambda i: (i * tm) // seq_len) if bc > 1 else (lambda i: 0)
    modspec = lambda part: pl.BlockSpec((None, 1, D_MODEL), lambda i, j, part=part: (seq_of(i), 0, part))
    return pl.pallas_call(
        _outup_kernel,
        grid=(m // tm, 2 * D_FF // tn),
        in_specs=[
            pl.BlockSpec((tm, D_MODEL), lambda i, j: (i, 0)),
            pl.BlockSpec((tm, HGRN_W), lambda i, j: (i, 0)),
            pl.BlockSpec((tm, SSD_W), lambda i, j: (i, 0)),
            modspec(2), modspec(3), modspec(4),
            pl.BlockSpec((1, D_MODEL), lambda i, j: (0, 0)),
            pl.BlockSpec((None, MIX_W, D_MODEL), lambda i, j: (layer, 0, 0)),
            pl.BlockSpec((None, D_MODEL, tn), lambda i, j: (layer, 0, j)),
        ],
        out_specs=[
            pl.BlockSpec((tm, D_MODEL), lambda i, j: (i, 0)),
            pl.BlockSpec((tm, tn), lambda i, j: (i, j)),
        ],
        out_shape=[jax.ShapeDtypeStruct((m, D_MODEL), F32),
                   jax.ShapeDtypeStruct((m, 2 * D_FF), BF16)],
        scratch_shapes=[pltpu.VMEM((tm, D_MODEL), BF16)],
        compiler_params=_cparams(("arbitrary", "arbitrary")),
        name="outup",
    )(x2, o_h, o_s, mod3, mod3, mod3, nw, w_out, w_up)


def _ffndown_kernel(*refs, width, rows, final):
    it = iter(refs)
    x1_ref, ug_ref, uv_ref, g2_ref, cwg_ref, cwv_ref, cbg_ref, cbv_ref, wd_ref, nw_ref = (
        next(it) for _ in range(10))
    sh_ref, sc_ref = (None, None) if final else (next(it), next(it))
    o_ref = next(it)
    hn_ref = None if final else next(it)
    acc_scr, act_scr = next(it), next(it)
    k = pl.program_id(1)

    for s in range(ug_ref.shape[1] // LANES):
        cs = slice(s * LANES, (s + 1) * LANES)
        gate = _dwconv(ug_ref[:, cs].astype(F32), cwg_ref[:, cs], cbg_ref[:, cs], width, rows)
        val = _dwconv(uv_ref[:, cs].astype(F32), cwv_ref[:, cs], cbv_ref[:, cs], width, rows)
        act_scr[:, cs] = (_silu(gate) * val).astype(BF16)
    part = jnp.dot(act_scr[...], wd_ref[...], preferred_element_type=F32)

    @pl.when(k == 0)
    def _():
        acc_scr[...] = part

    @pl.when(k > 0)
    def _():
        acc_scr[...] += part

    @pl.when(k == pl.num_programs(1) - 1)
    def _():
        x2 = x1_ref[...] + g2_ref[...] * acc_scr[...]
        if final:
            o_ref[...] = _rms(x2) * nw_ref[...]
        else:
            o_ref[...] = x2
            hn_ref[...] = (_rms(x2) * nw_ref[...] * (1.0 + sc_ref[...]) + sh_ref[...]).astype(BF16)


def _ffndown_call(x1, u, mod3, conv_w, conv_b, w_down, layer, nw, mod3_next, seq_len, width, rows):
    final = mod3_next is None
    m = x1.shape[0]
    tokens = 1024
    assert tokens == seq_len or rows == 1
    nk = 2
    tk = D_FF // nk
    bc = mod3.shape[0]
    seq_of = (lambda b: (b * tokens) // seq_len) if bc > 1 else (lambda b: 0)
    row_spec = pl.BlockSpec((tokens, D_MODEL), lambda b, k: (b, 0))
    modspec = lambda part: pl.BlockSpec((None, 1, D_MODEL), lambda b, k, part=part: (seq_of(b), 0, part))
    in_specs = [
        row_spec,
        pl.BlockSpec((tokens, tk), lambda b, k: (b, k)),
        pl.BlockSpec((tokens, tk), lambda b, k: (b, nk + k)),
        modspec(5),
        pl.BlockSpec((9, tk), lambda b, k: (0, k)),
        pl.BlockSpec((9, tk), lambda b, k: (0, nk + k)),
        pl.BlockSpec((1, tk), lambda b, k: (0, k)),
        pl.BlockSpec((1, tk), lambda b, k: (0, nk + k)),
        pl.BlockSpec((None, tk, D_MODEL), lambda b, k: (layer, k, 0)),
        pl.BlockSpec((1, D_MODEL), lambda b, k: (0, 0)),
    ]
    args = [x1, u, u, mod3, conv_w, conv_w, conv_b, conv_b, w_down, nw]
    out_specs = [row_spec]
    out_shape = [jax.ShapeDtypeStruct((m, D_MODEL), F32)]
    if not final:
        in_specs += [modspec(0), modspec(1)]
        args += [mod3_next, mod3_next]
        out_specs.append(row_spec)
        out_shape.append(jax.ShapeDtypeStruct((m, D_MODEL), BF16))
    res = pl.pallas_call(
        functools.partial(_ffndown_kernel, width=width, rows=rows, final=final),
        grid=(m // tokens, nk),
        in_specs=in_specs, out_specs=out_specs, out_shape=out_shape,
        scratch_shapes=[pltpu.VMEM((tokens, D_MODEL), F32), pltpu.VMEM((tokens, tk), BF16)],
        compiler_params=_cparams(("arbitrary", "arbitrary")),
        name="ffndown",
    )(*args)
    return (res[0], None) if final else (res[0], res[1])


def _ssd_param_rows(a_log, dt_bias, d_skip):
    def per_group(v):
        return v.reshape(2, SSD_GROUPS, HEADS_PER_GROUP).transpose(1, 0, 2).reshape(SSD_GROUPS, 2 * HEADS_PER_GROUP)

    def rows(v):
        return jnp.pad(per_group(v), ((0, 0), (0, LANES - 2 * HEADS_PER_GROUP))).reshape(SSD_GROUPS, 1, LANES)
    a_cols = jnp.broadcast_to(per_group(a_log)[:, :, None], (SSD_GROUPS, 2 * HEADS_PER_GROUP, LANES))
    return rows(a_log), a_cols, rows(dt_bias), jnp.repeat(d_skip, SSD_HEADDIM).reshape(1, SSD_W)


def _run_pass(x, mod_all_rows, s_h0, s_s0, layers, hgrn_lb, final_norm_w, consts, width, rows, want_state):
    batch, seq_len, _ = x.shape
    m = batch * seq_len
    x2 = x.reshape(m, D_MODEL)
    tl, tu = consts[0], consts[1]
    new_h, new_s = None, None
    hn = _norm_call(x2, mod_all_rows[0], layers[0]['norm_w1'], seq_len)
    for l, p in enumerate(layers):
        mod3 = mod_all_rows[l]
        last = l == DEPTH - 1
        o_h, new_h = _hgrn_call(hn, p['w_in'], hgrn_lb, p['hgrn_norm_w'], consts,
                                s_h0, new_h, l, batch, seq_len, want_state)
        o_s, new_s = _ssd_call(hn, p['w_in'], p['ssd'], (tl, tu),
                               s_s0, new_s, l, batch, seq_len, width, rows, want_state)
        x1, u = _outup_call(x2, o_h, o_s, mod3, p['norm_w2'], p['w_out'], p['ffn_up'], l, seq_len)
        x2, hn = _ffndown_call(x1, u, mod3, p['ffn_conv_w'], p['ffn_conv_b'], p['ffn_down'], l,
                               final_norm_w if last else layers[l + 1]['norm_w1'],
                               None if last else mod_all_rows[l + 1], seq_len, width, rows)
    return x2.reshape(batch, seq_len, D_MODEL), new_h, new_s


def kernel(x_prompt, x_sample, c, state_hgrn, state_ssd, c_ctx, norm_w, final_norm_w, w_ada, b_ada,
           w_in, w_out, hgrn_lb, hgrn_norm_w, ssd_conv_w, ssd_conv_b, ssd_a_log, ssd_dt_bias, ssd_d,
           ssd_norm_w, ffn_up, ffn_conv_w, ffn_conv_b, ffn_down):
    dec_batch = c.shape[0]
    consts = _scan_constants() + (_level_signs(),)

    cond8 = jnp.concatenate([c_ctx[None], c, jnp.zeros((8 - 1 - dec_batch, D_MODEL), F32)], axis=0)
    mod_all = _mod_call(cond8, w_ada, b_ada)
    mod_ctx = [mod_all[l, 0:1].reshape(1, 1, 6 * D_MODEL) for l in range(DEPTH)]
    mod_lat = [mod_all[l, 1:1 + dec_batch].reshape(dec_batch, 1, 6 * D_MODEL) for l in range(DEPTH)]

    w_in_t = jnp.swapaxes(w_in, 1, 2)
    w_out_b, ffn_up_b, ffn_down_b = w_out.astype(BF16), ffn_up.astype(BF16), ffn_down.astype(BF16)
    layers = []
    for l in range(DEPTH):
        a_rows, a_cols, b_rows, d_rows = _ssd_param_rows(ssd_a_log[l], ssd_dt_bias[l], ssd_d[l])
        layers.append(dict(
            norm_w1=norm_w[l, 0].reshape(1, D_MODEL), norm_w2=norm_w[l, 1].reshape(1, D_MODEL),
            w_in=w_in_t,
            w_out=w_out_b,
            hgrn_norm_w=hgrn_norm_w[l].reshape(1, HGRN_DV),
            ssd=dict(conv_w=ssd_conv_w[l].reshape(9, CONV_CH), conv_b=ssd_conv_b[l].reshape(1, CONV_CH),
                     a_log_rows=a_rows, a_log_cols=a_cols, dt_bias_rows=b_rows, d_rows=d_rows,
                     norm_w=ssd_norm_w[l].reshape(1, SSD_W)),
            ffn_up=ffn_up_b,
            ffn_conv_w=ffn_conv_w[l].reshape(9, 2 * D_FF), ffn_conv_b=ffn_conv_b[l].reshape(1, 2 * D_FF),
            ffn_down=ffn_down_b,
        ))
    fnw = final_norm_w.reshape(1, D_MODEL)

    y_prompt, new_h, new_s = _run_pass(x_prompt, mod_ctx, None, None, layers, hgrn_lb, fnw, consts,
                                       width=x_prompt.shape[1], rows=1, want_state=True)
    y_sample, _, _ = _run_pass(x_sample, mod_lat, state_hgrn, state_ssd, layers, hgrn_lb, fnw, consts,
                               width=GRID_W, rows=x_sample.shape[1] // GRID_W, want_state=False)
    return (y_prompt, y_sample, new_h, new_s)
```

```python
import functools
import math

import numpy as np
import jax
import jax.numpy as jnp
from jax import lax
from jax.experimental import pallas as pl
from jax.experimental.pallas import tpu as pltpu

F32 = jnp.float32
BF16 = jnp.bfloat16

D_MODEL = 1024
DEPTH = 2
GRID_W = 64
HGRN_HEADS = 8
HGRN_DK = 128
HGRN_DV = 128
HGRN_KW = HGRN_HEADS * HGRN_DK
HGRN_W = HGRN_HEADS * HGRN_DV
SSD_W = 1024
SSD_HEADDIM = 64
SSD_HEADS = 16
SSD_GROUPS = 4
SSD_STATE = 128
HEADS_PER_GROUP = SSD_HEADS // SSD_GROUPS
GROUP_W = SSD_W // SSD_GROUPS
MIX_W = HGRN_W + SSD_W
CONV_CH = SSD_W + 2 * SSD_GROUPS * SSD_STATE
D_FF = 2816
MAIN_COLS = 3 * HGRN_KW + 2 * HGRN_W + SSD_W + CONV_CH
DT_COLS = 2 * SSD_HEADS
EPS = 1e-6
LOG2E = math.log2(math.e)

LANES = 128
CHUNK = 128
N_LEVELS = 7
VMEM_LIMIT = 56 * 1024 * 1024


def _cparams(sem):
    return pltpu.CompilerParams(dimension_semantics=sem, vmem_limit_bytes=VMEM_LIMIT)


def _sigmoid(x):
    return 1.0 / (1.0 + jnp.exp(-x))


def _silu(x):
    return x * _sigmoid(x)


def _softplus(x):
    return jnp.maximum(x, 0.0) + jnp.log1p(jnp.exp(-jnp.abs(x)))


def _rms(x):
    return x * lax.rsqrt(jnp.mean(x * x, axis=-1, keepdims=True) + EPS)


def _split3(x):
    hi = x.astype(BF16)
    r = x - hi.astype(F32)
    mid = r.astype(BF16)
    lo = (r - mid.astype(F32)).astype(BF16)
    return hi, mid, lo


def _sum_rows_exact(w01, x):
    n = x.shape[1]
    hi, mid, lo = _split3(x)
    p = jnp.dot(w01, jnp.concatenate([hi, mid, lo], axis=1), preferred_element_type=F32)
    return p[:, :n] + p[:, n:2 * n] + p[:, 2 * n:]


def _sum_cols_exact(x, w01):
    m = x.shape[0]
    hi, mid, lo = _split3(x)
    p = jnp.dot(jnp.concatenate([hi, mid, lo], axis=0), w01, preferred_element_type=F32)
    return p[:m] + p[m:2 * m] + p[2 * m:]


def _dot_nt(a, b):
    return lax.dot_general(a, b, (((1,), (1,)), ((), ())), preferred_element_type=F32)


def _dot_tn(a, b):
    return lax.dot_general(a, b, (((0,), (0,)), ((), ())), preferred_element_type=F32)


def _dwconv(x, w_ref, b_ref, width, rows):
    seq, _ = x.shape
    t = lax.broadcasted_iota(jnp.int32, x.shape, 0)
    col = jnp.bitwise_and(t, width - 1)
    xl = jnp.where(col == 0, 0.0, pltpu.roll(x, 1, axis=0))
    xr = jnp.where(col == width - 1, 0.0, pltpu.roll(x, seq - 1, axis=0))

    def hrow(kh, a, b, c):
        return w_ref[3 * kh:3 * kh + 1, :] * a + w_ref[3 * kh + 1:3 * kh + 2, :] * b \
            + w_ref[3 * kh + 2:3 * kh + 3, :] * c

    out = b_ref[0:1, :] + hrow(1, xl, x, xr)
    if rows > 1:
        inner = seq - width
        up = hrow(0, xl[:inner], x[:inner], xr[:inner])
        out = jnp.concatenate([out[:width], out[width:] + up], axis=0)
        dn = hrow(2, xl[width:], x[width:], xr[width:])
        out = jnp.concatenate([out[:inner] + dn, out[inner:]], axis=0)
    return out


def _scan_constants():
    c = CHUNK
    t = np.arange(c)[:, None]
    i = np.arange(c)[None, :]
    x = t ^ i
    lv = np.where(x == 0, 0, np.floor(np.log2(np.maximum(x, 1))).astype(np.int64) + 1)
    lv = np.where(i > t, -lv, lv)
    return jnp.asarray(i <= t, BF16), jnp.asarray(i >= t, BF16), jnp.asarray(lv, jnp.int32)


def _level_signs():
    t = np.arange(CHUNK)[:, None]
    sg = [np.where(((t >> lev) & 1) == 1, 1.0, -1.0) * np.ones((1, LANES)) for lev in range(N_LEVELS)]
    return jnp.asarray(np.concatenate(sg, 0), F32)


def _mod_kernel(cond_ref, w_ref, b_ref, o_ref):
    a = _silu(cond_ref[...]).astype(BF16)
    o_ref[...] = jnp.dot(a, w_ref[...].astype(BF16), preferred_element_type=F32) + b_ref[...]


def _mod_call(cond8, w_ada, b_ada):
    n = 6 * D_MODEL
    tn = D_MODEL
    return pl.pallas_call(
        _mod_kernel,
        grid=(DEPTH, n // tn),
        in_specs=[
            pl.BlockSpec((8, D_MODEL), lambda l, j: (0, 0)),
            pl.BlockSpec((None, D_MODEL, tn), lambda l, j: (l, 0, j)),
            pl.BlockSpec((None, 1, tn), lambda l, j: (l, 0, j)),
        ],
        out_specs=pl.BlockSpec((None, 8, tn), lambda l, j: (l, 0, j)),
        out_shape=jax.ShapeDtypeStruct((DEPTH, 8, n), F32),
        compiler_params=_cparams(("arbitrary", "arbitrary")),
        name="mod",
    )(cond8, w_ada, b_ada.reshape(DEPTH, 1, n))


def _norm_kernel(x_ref, sh_ref, sc_ref, nw_ref, hn_ref):
    h = _rms(x_ref[...]) * nw_ref[...] * (1.0 + sc_ref[...]) + sh_ref[...]
    hn_ref[...] = h.astype(BF16)


def _norm_call(x2, mod3, nw, seq_len):
    m = x2.shape[0]
    tm = 512
    bc = mod3.shape[0]
    seq_of = (lambda i: (i * tm) // seq_len) if bc > 1 else (lambda i: 0)
    return pl.pallas_call(
        _norm_kernel,
        grid=(m // tm,),
        in_specs=[
            pl.BlockSpec((tm, D_MODEL), lambda i: (i, 0)),
            pl.BlockSpec((None, 1, D_MODEL), lambda i: (seq_of(i), 0, 0)),
            pl.BlockSpec((None, 1, D_MODEL), lambda i: (seq_of(i), 0, 1)),
            pl.BlockSpec((1, D_MODEL), lambda i: (0, 0)),
        ],
        out_specs=pl.BlockSpec((tm, D_MODEL), lambda i: (i, 0)),
        out_shape=jax.ShapeDtypeStruct((m, D_MODEL), BF16),
        compiler_params=_cparams(("arbitrary",)),
        name="norm",
    )(x2, mod3, mod3, nw)


def _hgrn_gates(f_pre, lbd):
    ea = jnp.exp(-jnp.abs(f_pre))
    log2_sig = jnp.minimum(f_pre, 0.0) * LOG2E - jnp.log2(1.0 + ea)
    sig_neg = jnp.where(f_pre >= 0.0, ea, 1.0) / (1.0 + ea)
    if lbd is None:
        return log2_sig, sig_neg
    a = jnp.log2(lbd)
    b = jnp.log2(1.0 - lbd) + log2_sig
    log2_f = jnp.maximum(a, b) + jnp.log2(1.0 + jnp.exp2(-jnp.abs(a - b)))
    return log2_f, (1.0 - lbd) * sig_neg


def _block_mid(scr, m):
    return jnp.concatenate(
        [jnp.broadcast_to(scr[r0 + m - 1:r0 + m, :], (2 * m, scr.shape[1])) for r0 in range(0, CHUNK, 2 * m)],
        axis=0)


def _mid_distance(c, scr, m):
    pieces = []
    for r0 in range(0, CHUNK, 2 * m):
        mid = jnp.broadcast_to(scr[r0 + m - 1:r0 + m, :], (m, scr.shape[1]))
        pieces += [mid - c[r0:r0 + m], c[r0 + m:r0 + 2 * m] - mid]
    return jnp.concatenate(pieces, axis=0)


def _pick_halves(up_val, low_val, m):
    pieces = []
    for r0 in range(0, CHUNK, 2 * m):
        pieces += [low_val[r0:r0 + m], up_val[r0 + m:r0 + 2 * m]]
    return jnp.concatenate(pieces, axis=0)


def _hgrn_kernel(*refs, layer, seq_len, seqs, has_s0, n_prev, want_state):
    it = iter(refs)
    hn_ref = next(it)
    w_refs = [next(it) for _ in range(5)]
    lb_ref, nw_ref, tl_ref, lv_ref, sg_ref = (next(it) for _ in range(5))
    s0_ref = next(it) if has_s0 else None
    prev_refs = [next(it) for _ in range(n_prev)]
    o_ref = next(it)
    sf_ref = next(it) if want_state else None
    if n_prev:
        for i, pref in enumerate(prev_refs):
            sf_ref[:, i] = pref[...]
        sf_own = sf_ref.at[:, layer]
    else:
        sf_own = sf_ref
    (w_scr, p_scr, acc_scr, qc_scr, u_scr, dl_scr, st_scr, cumf_scr, cumb_scr) = (next(it) for _ in range(9))

    c = CHUNK
    dk = HGRN_DK
    n_chunks = seqs * seq_len // c
    cps = seq_len // c

    @pl.when(pl.program_id(1) == 0)
    def _():
        for i, w_ref in enumerate(w_refs):
            w_scr[i * LANES:(i + 1) * LANES, :] = w_ref[...].astype(BF16)

    p_scr[...] = _dot_nt(hn_ref[...], w_scr[...])

    lb = None
    if layer > 0:
        lbr = lb_ref[...]
        e = jnp.exp(lbr - jnp.max(lbr, axis=0, keepdims=True))
        sm = e / jnp.sum(e, axis=0, keepdims=True)
        lb = sm[1]
        for i in range(2, layer + 1):
            lb = lb + sm[i]

    lvs = lv_ref[...]
    lv = jnp.abs(lvs)
    tl = tl_ref[...]
    zero = jnp.zeros((c, dk), F32)
    odd = jnp.bitwise_and(lax.broadcasted_iota(jnp.int32, (c, dk), 0), 1) == 1

    def intra(ci, carry):
        r0 = pl.multiple_of(ci * c, c)
        sl = pl.ds(r0, c)
        q = _silu(p_scr[sl, 0:dk]) * (dk ** -0.5)
        lf_f, kf = _hgrn_gates(p_scr[sl, dk:2 * dk], None if lb is None else lb[0:1, :])
        lf_b, kb = _hgrn_gates(p_scr[sl, 2 * dk:3 * dk], None if lb is None else lb[1:2, :])
        vb = p_scr[sl, 3 * dk:4 * dk].astype(BF16)

        cum_f = _sum_rows_exact(tl, lf_f)
        cum_b = _sum_rows_exact(tl, lf_b)
        cx_b = cum_b - lf_b
        cumf_scr[...] = cum_f
        cumb_scr[...] = cum_b
        tot_f = cum_f[c - 1:c, :]
        tot_b = cum_b[c - 1:c, :]

        a_sum = _dot_nt(q.astype(BF16), (kf + kb).astype(BF16))

        f_f = jnp.exp2(lf_f)
        f_b = jnp.exp2(lf_b)
        qf = q * f_f
        qb = q * f_b
        p_f = _dot_nt(qf.astype(BF16), kf.astype(BF16))
        p_b = _dot_nt(qb.astype(BF16), kb.astype(BF16))
        a_sum = jnp.where(lvs == 1, p_f, jnp.where(lvs == -1, p_b, a_sum))
        p_f = _dot_nt(jnp.where(odd, qf * pltpu.roll(f_f, 1, axis=0), qf).astype(BF16),
                      jnp.where(odd, kf, kf * pltpu.roll(f_f, c - 1, axis=0)).astype(BF16))
        p_b = _dot_nt(jnp.where(odd, qb, qb * pltpu.roll(f_b, c - 1, axis=0)).astype(BF16),
                      jnp.where(odd, kb * pltpu.roll(f_b, 1, axis=0), kb).astype(BF16))
        a_sum = jnp.where(lvs == 2, p_f, jnp.where(lvs == -2, p_b, a_sum))

        sgn = sg_ref[2 * c:3 * c, :]
        ef = jnp.exp2((cum_f - _block_mid(cumf_scr, 4)) * sgn)
        eb = jnp.exp2((cx_b - _block_mid(cumb_scr, 4)) * sgn)
        p_f = _dot_nt((q * ef).astype(BF16), (kf * ef).astype(BF16))
        p_b = _dot_nt((q * eb).astype(BF16), (kb * eb).astype(BF16))
        a_sum = jnp.where(lvs == 3, p_f, jnp.where(lvs == -3, p_b, a_sum))

        for lev in range(4, N_LEVELS + 1):
            m = 1 << (lev - 1)
            ef = jnp.exp2(_mid_distance(cum_f, cumf_scr, m))
            eb = jnp.exp2(_mid_distance(cx_b, cumb_scr, m))
            qe = q * _pick_halves(ef, eb, m)
            ke = _pick_halves(kb, kf, m) * _pick_halves(eb, ef, m)
            lhs = jnp.concatenate([_pick_halves(qe, zero, m), _pick_halves(zero, qe, m)], axis=1)
            rhs = jnp.concatenate([_pick_halves(zero, ke, m), _pick_halves(ke, zero, m)], axis=1)
            a_sum = jnp.where(lv == lev, _dot_nt(lhs.astype(BF16), rhs.astype(BF16)), a_sum)
        acc_scr[sl, :] = jnp.dot(a_sum.astype(BF16), vb, preferred_element_type=F32)

        qc_scr[sl, :] = jnp.concatenate(
            [q * jnp.exp2(cum_f), q * jnp.exp2(jnp.minimum(tot_b - cx_b, 0.0))], axis=1).astype(BF16)
        u_scr[ci, 0] = _dot_tn(vb, (kf * jnp.exp2(jnp.minimum(tot_f - cum_f, 0.0))).astype(BF16))
        u_scr[ci, 1] = _dot_tn(vb, (kb * jnp.exp2(jnp.minimum(cx_b, 0.0))).astype(BF16))
        dl_scr[ci, 0:1, :] = jnp.exp2(tot_f)
        dl_scr[ci, 1:2, :] = jnp.exp2(tot_b)
        return carry

    lax.fori_loop(0, n_chunks, intra, 0, unroll=4)

    def init(s, d):
        return s0_ref[s, d].T if has_s0 else jnp.zeros((HGRN_DV, dk), F32)

    st = None
    for ci in range(n_chunks):
        s = ci // cps
        if ci % cps == 0:
            st = init(s, 0)
        st_scr[ci, :, 0:dk] = st.astype(BF16)
        st = st * dl_scr[ci, 0:1, :] + u_scr[ci, 0]
        if want_state and ci % cps == cps - 1:
            sf_own[s, 0] = st.T
    for ci in reversed(range(n_chunks)):
        s = ci // cps
        if ci % cps == cps - 1:
            st = init(s, 1)
        st_scr[ci, :, dk:2 * dk] = st.astype(BF16)
        st = st * dl_scr[ci, 1:2, :] + u_scr[ci, 1]
        if want_state and ci % cps == 0:
            sf_own[s, 1] = st.T

    nw = nw_ref[...]

    for ci in range(n_chunks):
        sl = slice(ci * c, (ci + 1) * c)
        o = acc_scr[sl, :] + _dot_nt(qc_scr[sl, :], st_scr[ci])
        o_ref[sl, :] = (_rms(o) * nw * _silu(p_scr[sl, 4 * dk:5 * dk])).astype(o_ref.dtype)


def _hgrn_call(hn, w_in_t, lb_raw, nw, consts, s0, prev_states, layer, batch, seq_len, want_state):
    tl, _, lv, sg = consts
    has_s0 = s0 is not None
    m = hn.shape[0]
    tokens = 1024
    seqs = tokens // seq_len
    n_chunks = tokens // CHUNK
    full = lambda arr: pl.BlockSpec(arr.shape, lambda h, b: (0,) * arr.ndim)
    state_spec = pl.BlockSpec((seqs, None, 2, None, HGRN_DK, HGRN_DV), lambda h, b: (b, layer, 0, h, 0, 0))
    wcol = lambda part: pl.BlockSpec((None, LANES, D_MODEL),
                                     lambda h, b, part=part: (layer, part * HGRN_HEADS + h, 0))
    in_specs = [pl.BlockSpec((tokens, D_MODEL), lambda h, b: (b, 0)),
                wcol(0), wcol(1), wcol(2), wcol(3), wcol(4),
                pl.BlockSpec((DEPTH, 2, LANES), lambda h, b: (0, 0, h)),
                pl.BlockSpec((1, HGRN_DV), lambda h, b: (0, 0)),
                full(tl), full(lv), full(sg)]
    args = [hn, w_in_t, w_in_t, w_in_t, w_in_t, w_in_t, lb_raw, nw, tl, lv, sg]
    if has_s0:
        in_specs.append(state_spec)
        args.append(s0)
    out_specs = [pl.BlockSpec((tokens, LANES), lambda h, b: (b, h))]
    out_shape = [jax.ShapeDtypeStruct((m, HGRN_W), BF16)]
    n_prev = 0
    if want_state:
        own_spec = pl.BlockSpec((seqs, 2, None, HGRN_DK, HGRN_DV), lambda h, b: (b, 0, h, 0, 0))
        own_shape = (batch, 2, HGRN_HEADS, HGRN_DK, HGRN_DV)
        if layer == DEPTH - 1 and prev_states:
            n_prev = len(prev_states)
            in_specs += [own_spec] * n_prev
            args += list(prev_states)
            out_specs.append(pl.BlockSpec((seqs, DEPTH, 2, None, HGRN_DK, HGRN_DV),
                                          lambda h, b: (b, 0, 0, h, 0, 0)))
            out_shape.append(jax.ShapeDtypeStruct((batch, DEPTH) + own_shape[1:], F32))
        else:
            out_specs.append(own_spec)
            out_shape.append(jax.ShapeDtypeStruct(own_shape, F32))
    scratch = [pltpu.VMEM((5 * LANES, D_MODEL), BF16),
               pltpu.VMEM((tokens, 5 * LANES), F32),
               pltpu.VMEM((tokens, HGRN_DV), F32),
               pltpu.VMEM((tokens, 2 * HGRN_DK), BF16),
               pltpu.VMEM((n_chunks, 2, HGRN_DV, HGRN_DK), F32),
               pltpu.VMEM((n_chunks, 8, HGRN_DK), F32),
               pltpu.VMEM((n_chunks, HGRN_DV, 2 * HGRN_DK), BF16),
               pltpu.VMEM((CHUNK, HGRN_DK), F32),
               pltpu.VMEM((CHUNK, HGRN_DK), F32)]
    res = pl.pallas_call(
        functools.partial(_hgrn_kernel, layer=layer, seq_len=seq_len, seqs=seqs, has_s0=has_s0,
                          n_prev=n_prev, want_state=want_state),
        grid=(HGRN_HEADS, m // tokens),
        in_specs=in_specs, out_specs=out_specs, out_shape=out_shape,
        scratch_shapes=scratch,
        compiler_params=_cparams(("arbitrary", "arbitrary")),
        name="hgrn",
    )(*args)
    return (res[0], res[1]) if want_state else (res[0], None)


def _ssd_kernel(*refs, layer, seq_len, seqs, width, rows, has_s0, n_prev, want_state):
    it = iter(refs)
    (hn_ref, wz_ref, wx_ref, wb_ref, wc_ref, wdt_ref, cwx_ref, cwb_ref, cwc_ref, cbx_ref, cbb_ref, cbc_ref,
     alog_ref, acol_ref, dtb_ref, dsk_ref, nw_ref, tl_ref, tu_ref) = (next(it) for _ in range(19))
    s0_ref = next(it) if has_s0 else None
    prev_refs = [next(it) for _ in range(n_prev)]
    o_ref = next(it)
    sf_ref = next(it) if want_state else None
    if n_prev:
        for i, pref in enumerate(prev_refs):
            sf_ref[:, i] = pref[...]
        sf_own = sf_ref.at[:, layer]
    else:
        sf_own = sf_ref
    (w_scr, p_scr, xc_scr, bc_scr, cc_scr, dtg_scr, acc_scr, ecum_scr, u_scr, dl_scr, st_scr) = (
        next(it) for _ in range(11))

    q = CHUNK
    n_chunks = seqs * seq_len // q
    cps = seq_len // q
    hp = HEADS_PER_GROUP
    p = SSD_HEADDIM
    gw = GROUP_W
    n = SSD_STATE
    dt0 = 2 * gw + 2 * n

    @pl.when(pl.program_id(1) == 0)
    def _():
        w_scr[0:gw, :] = wz_ref[...].astype(BF16)
        w_scr[gw:2 * gw, :] = wx_ref[...].astype(BF16)
        w_scr[2 * gw:2 * gw + n, :] = wb_ref[...].astype(BF16)
        w_scr[2 * gw + n:2 * gw + 2 * n, :] = wc_ref[...].astype(BF16)
        w_scr[dt0:dt0 + DT_COLS, :] = wdt_ref[...].astype(BF16)
        w_scr[dt0 + DT_COLS:dt0 + LANES, :] = jnp.zeros((LANES - DT_COLS, D_MODEL), BF16)

    hn = hn_ref[...]
    p_scr[...] = _dot_nt(hn, w_scr[0:gw, :])
    px = _dot_nt(hn, w_scr[gw:2 * gw, :])
    for s in range(gw // LANES):
        cs = slice(s * LANES, (s + 1) * LANES)
        xc_scr[:, cs] = _silu(_dwconv(px[:, cs], cwx_ref[:, cs], cbx_ref[:, cs], width, rows))
    pbc = _dot_nt(hn, w_scr[2 * gw:2 * gw + 2 * n, :])
    bc_scr[...] = _silu(_dwconv(pbc[:, 0:n], cwb_ref, cbb_ref, width, rows))
    cc_scr[...] = _silu(_dwconv(pbc[:, n:2 * n], cwc_ref, cbc_ref, width, rows)).astype(BF16)
    g = pl.program_id(0)
    dt_all = _dot_nt(hn, w_scr[dt0:dt0 + LANES, :])
    lane = lax.broadcasted_iota(jnp.int32, dt_all.shape, 1)
    dt_fw = pltpu.roll(dt_all, jnp.bitwise_and(LANES - hp * g, LANES - 1), axis=1)
    dt_bw = pltpu.roll(dt_all, jnp.bitwise_and(LANES - (SSD_HEADS - hp) - hp * g, LANES - 1), axis=1)
    dtg_scr[...] = _softplus(jnp.where(lane < hp, dt_fw, dt_bw) + dtb_ref[...])
    neg_a = -jnp.exp(alog_ref[...])
    neg_a_t = -jnp.exp(acol_ref[...])

    tl = tl_ref[...]
    tu = tu_ref[...]
    ti = lax.broadcasted_iota(jnp.int32, (q, q), 0)
    si = lax.broadcasted_iota(jnp.int32, (q, q), 1)
    lower = si <= ti
    upper = si >= ti

    def intra(ci, carry):
        r0 = pl.multiple_of(ci * q, q)
        sl = pl.ds(r0, q)
        dt = dtg_scr[sl, :]
        da = dt * neg_a
        cum_f = _sum_rows_exact(tl, da)
        cum_b = _sum_rows_exact(tu, da)
        dt_t = dt.T[0:2 * hp, :]
        da_t = dt_t * neg_a_t
        row_f = _sum_cols_exact(da_t, tu)
        row_b = _sum_cols_exact(da_t, tl)
        wr_f = jnp.exp(row_f[:, q - 1:q] - row_f) * dt_t
        wr_b = jnp.exp(row_b[:, 0:1] - row_b) * dt_t
        xb = xc_scr[sl, :].astype(BF16)
        bm = bc_scr[sl, :]
        bt = bm.T
        gmat = _dot_nt(cc_scr[sl, :], bm.astype(BF16))
        ecum_scr[sl, 0:LANES] = jnp.exp(cum_f)
        ecum_scr[sl, LANES:2 * LANES] = jnp.exp(cum_b)
        ys, u_f, u_b = [], [], []
        for j in range(hp):
            jf, jb = j, hp + j
            dec_f = jnp.exp(jnp.minimum(cum_f[:, jf:jf + 1] - row_f[jf:jf + 1, :], 0.0)) * dt_t[jf:jf + 1, :]
            dec_b = jnp.exp(jnp.minimum(cum_b[:, jb:jb + 1] - row_b[jb:jb + 1, :], 0.0)) * dt_t[jb:jb + 1, :]
            mh = gmat * (jnp.where(lower, dec_f, 0.0) + jnp.where(upper, dec_b, 0.0))
            xh = xb[:, j * p:(j + 1) * p]
            ys.append(jnp.dot(mh.astype(BF16), xh, preferred_element_type=F32))
            u_f.append(jnp.dot((bt * wr_f[jf:jf + 1, :]).astype(BF16), xh, preferred_element_type=F32))
            u_b.append(jnp.dot((bt * wr_b[jb:jb + 1, :]).astype(BF16), xh, preferred_element_type=F32))
        acc_scr[sl, :] = jnp.concatenate(ys, axis=1)
        u_scr[ci, 0] = jnp.concatenate(u_f, axis=1)
        u_scr[ci, 1] = jnp.concatenate(u_b, axis=1)
        dl_scr[ci, 0:1, :] = jnp.exp(cum_f[q - 1:q, :])
        dl_scr[ci, 1:2, :] = jnp.exp(cum_b[0:1, :])
        return carry

    lax.fori_loop(0, n_chunks, intra, 0, unroll=4)

    def head_scale(vec, lane0):
        return jnp.concatenate(
            [jnp.broadcast_to(vec[0:1, lane0 + j:lane0 + j + 1], (1, p)) for j in range(hp)], axis=1)

    def init(s, d):
        return s0_ref[s, d].reshape(hp * p, n).T if has_s0 else jnp.zeros((n, hp * p), F32)

    st = None
    for ci in range(n_chunks):
        s = ci // cps
        if ci % cps == 0:
            st = init(s, 0)
        st_scr[ci, :, 0:hp * p] = st.astype(BF16)
        st = st * head_scale(dl_scr[ci, 0:1, :], 0) + u_scr[ci, 0]
        if want_state and ci % cps == cps - 1:
            sf_own[s, 0] = st.T.reshape(hp, p, n)
    for ci in reversed(range(n_chunks)):
        s = ci // cps
        if ci % cps == cps - 1:
            st = init(s, 1)
        st_scr[ci, :, hp * p:2 * hp * p] = st.astype(BF16)
        st = st * head_scale(dl_scr[ci, 1:2, :], hp) + u_scr[ci, 1]
        if want_state and ci % cps == 0:
            sf_own[s, 1] = st.T.reshape(hp, p, n)

    dsk = dsk_ref[...]
    nw = nw_ref[...]
    for ci in range(n_chunks):
        sl = slice(ci * q, (ci + 1) * q)
        yi = jnp.dot(cc_scr[sl, :], st_scr[ci], preferred_element_type=F32)
        ec = ecum_scr[sl, :]
        parts = []
        for j in range(hp):
            parts.append(yi[:, j * p:(j + 1) * p] * ec[:, j:j + 1]
                         + yi[:, (hp + j) * p:(hp + j + 1) * p] * ec[:, LANES + hp + j:LANES + hp + j + 1])
        y = acc_scr[sl, :] + jnp.concatenate(parts, axis=1) + dsk * xc_scr[sl, :]
        y = y * _silu(p_scr[sl, 0:gw])
        o_ref[sl, :] = (_rms(y) * nw).astype(o_ref.dtype)


def _ssd_call(hn, w_in_t, p, consts, s0, prev_states, layer, batch, seq_len, width, rows, want_state):
    tl, tu = consts
    has_s0 = s0 is not None
    m = hn.shape[0]
    tokens = 1024
    seqs = tokens // seq_len
    assert seqs == 1 or rows == 1
    n_chunks = tokens // CHUNK
    wcols = 2 * GROUP_W + 2 * SSD_STATE + LANES
    xb = SSD_W // LANES
    z0 = 5 * HGRN_KW // GROUP_W
    x0 = z0 + SSD_W // GROUP_W
    b0 = (5 * HGRN_KW + 2 * SSD_W) // LANES
    c0 = b0 + SSD_GROUPS
    d0 = MAIN_COLS // DT_COLS
    full = lambda arr: pl.BlockSpec(arr.shape, lambda g, b: (0,) * arr.ndim)
    in_specs = [
        pl.BlockSpec((tokens, D_MODEL), lambda g, b: (b, 0)),
        pl.BlockSpec((None, GROUP_W, D_MODEL), lambda g, b: (layer, z0 + g, 0)),
        pl.BlockSpec((None, GROUP_W, D_MODEL), lambda g, b: (layer, x0 + g, 0)),
        pl.BlockSpec((None, LANES, D_MODEL), lambda g, b: (layer, b0 + g, 0)),
        pl.BlockSpec((None, LANES, D_MODEL), lambda g, b: (layer, c0 + g, 0)),
        pl.BlockSpec((None, DT_COLS, D_MODEL), lambda g, b: (layer, d0, 0)),
        pl.BlockSpec((9, GROUP_W), lambda g, b: (0, g)),
        pl.BlockSpec((9, LANES), lambda g, b: (0, xb + g)),
        pl.BlockSpec((9, LANES), lambda g, b: (0, xb + SSD_GROUPS + g)),
        pl.BlockSpec((1, GROUP_W), lambda g, b: (0, g)),
        pl.BlockSpec((1, LANES), lambda g, b: (0, xb + g)),
        pl.BlockSpec((1, LANES), lambda g, b: (0, xb + SSD_GROUPS + g)),
        pl.BlockSpec((None, 1, LANES), lambda g, b: (g, 0, 0)),
        pl.BlockSpec((None, 2 * HEADS_PER_GROUP, LANES), lambda g, b: (g, 0, 0)),
        pl.BlockSpec((None, 1, LANES), lambda g, b: (g, 0, 0)),
        pl.BlockSpec((1, GROUP_W), lambda g, b: (0, g)),
        pl.BlockSpec((1, GROUP_W), lambda g, b: (0, g)),
        full(tl), full(tu),
    ]
    args = [hn, w_in_t, w_in_t, w_in_t, w_in_t, w_in_t, p['conv_w'], p['conv_w'], p['conv_w'],
            p['conv_b'], p['conv_b'], p['conv_b'],
            p['a_log_rows'], p['a_log_cols'], p['dt_bias_rows'], p['d_rows'], p['norm_w'], tl, tu]
    state_spec = pl.BlockSpec((seqs, None, 2, HEADS_PER_GROUP, SSD_HEADDIM, SSD_STATE),
                              lambda g, b: (b, layer, 0, g, 0, 0))
    if has_s0:
        in_specs.append(state_spec)
        args.append(s0)
    out_specs = [pl.BlockSpec((tokens, GROUP_W), lambda g, b: (b, g))]
    out_shape = [jax.ShapeDtypeStruct((m, SSD_W), BF16)]
    n_prev = 0
    if want_state:
        own_spec = pl.BlockSpec((seqs, 2, HEADS_PER_GROUP, SSD_HEADDIM, SSD_STATE), lambda g, b: (b, 0, g, 0, 0))
        own_shape = (batch, 2, SSD_HEADS, SSD_HEADDIM, SSD_STATE)
        if layer == DEPTH - 1 and prev_states:
            n_prev = len(prev_states)
            in_specs += [own_spec] * n_prev
            args += list(prev_states)
            out_specs.append(pl.BlockSpec((seqs, DEPTH, 2, HEADS_PER_GROUP, SSD_HEADDIM, SSD_STATE),
                                          lambda g, b: (b, 0, 0, g, 0, 0)))
            out_shape.append(jax.ShapeDtypeStruct((batch, DEPTH) + own_shape[1:], F32))
        else:
            out_specs.append(own_spec)
            out_shape.append(jax.ShapeDtypeStruct(own_shape, F32))
    hpp = HEADS_PER_GROUP * SSD_HEADDIM
    scratch = [pltpu.VMEM((wcols, D_MODEL), BF16),
               pltpu.VMEM((tokens, GROUP_W), F32),
               pltpu.VMEM((tokens, GROUP_W), F32),
               pltpu.VMEM((tokens, SSD_STATE), F32),
               pltpu.VMEM((tokens, SSD_STATE), BF16),
               pltpu.VMEM((tokens, LANES), F32),
               pltpu.VMEM((tokens, GROUP_W), F32),
               pltpu.VMEM((tokens, 2 * LANES), F32),
               pltpu.VMEM((n_chunks, 2, SSD_STATE, hpp), F32),
               pltpu.VMEM((n_chunks, 8, LANES), F32),
               pltpu.VMEM((n_chunks, SSD_STATE, 2 * hpp), BF16)]
    res = pl.pallas_call(
        functools.partial(_ssd_kernel, layer=layer, seq_len=seq_len, seqs=seqs, width=width, rows=rows,
                          has_s0=has_s0, n_prev=n_prev, want_state=want_state),
        grid=(SSD_GROUPS, m // tokens),
        in_specs=in_specs, out_specs=out_specs, out_shape=out_shape,
        scratch_shapes=scratch,
        compiler_params=_cparams(("arbitrary", "arbitrary")),
        name="ssd",
    )(*args)
    return (res[0], res[1]) if want_state else (res[0], None)


def _outup_kernel(x_ref, oh_ref, os_ref, g1_ref, sh_ref, sc_ref, nw_ref, wo_ref, wu_ref,
                  x1_ref, u_ref, h_scr):
    @pl.when(pl.program_id(1) == 0)
    def _():
        mix = jnp.dot(oh_ref[...], wo_ref[0:HGRN_W, :], preferred_element_type=F32)
        mix = mix + jnp.dot(os_ref[...], wo_ref[HGRN_W:MIX_W, :], preferred_element_type=F32)
        x1 = x_ref[...] + g1_ref[...] * mix
        x1_ref[...] = x1
        h = _rms(x1) * nw_ref[...] * (1.0 + sc_ref[...]) + sh_ref[...]
        h_scr[...] = h.astype(BF16)

    u_ref[...] = jnp.dot(h_scr[...], wu_ref[...], preferred_element_type=F32).astype(u_ref.dtype)


def _outup_call(x2, o_h, o_s, mod3, nw, w_out, w_up, layer, seq_len):
    m = x2.shape[0]
    tm, tn = 1024, 512
    bc = mod3.shape[0]
    seq_of = (lambda i: (i * tm) // seq_len) if bc > 1 else (lambda i: 0)
    modspec = lambda part: pl.BlockSpec((None, 1, D_MODEL), lambda i, j, part=part: (seq_of(i), 0, part))
    return pl.pallas_call(
        _outup_kernel,
        grid=(m // tm, 2 * D_FF // tn),
        in_specs=[
            pl.BlockSpec((tm, D_MODEL), lambda i, j: (i, 0)),
            pl.BlockSpec((tm, HGRN_W), lambda i, j: (i, 0)),
            pl.BlockSpec((tm, SSD_W), lambda i, j: (i, 0)),
            modspec(2), modspec(3), modspec(4),
            pl.BlockSpec((1, D_MODEL), lambda i, j: (0, 0)),
            pl.BlockSpec((None, MIX_W, D_MODEL), lambda i, j: (layer, 0, 0)),
            pl.BlockSpec((None, D_MODEL, tn), lambda i, j: (layer, 0, j)),
        ],
        out_specs=[
            pl.BlockSpec((tm, D_MODEL), lambda i, j: (i, 0)),
            pl.BlockSpec((tm, tn), lambda i, j: (i, j)),
        ],
        out_shape=[jax.ShapeDtypeStruct((m, D_MODEL), F32),
                   jax.ShapeDtypeStruct((m, 2 * D_FF), BF16)],
        scratch_shapes=[pltpu.VMEM((tm, D_MODEL), BF16)],
        compiler_params=_cparams(("arbitrary", "arbitrary")),
        name="outup",
    )(x2, o_h, o_s, mod3, mod3, mod3, nw, w_out, w_up)


def _ffndown_kernel(*refs, width, rows, final):
    it = iter(refs)
    x1_ref, ug_ref, uv_ref, g2_ref, cwg_ref, cwv_ref, cbg_ref, cbv_ref, wd_ref, nw_ref = (
        next(it) for _ in range(10))
    sh_ref, sc_ref = (None, None) if final else (next(it), next(it))
    o_ref = next(it)
    hn_ref = None if final else next(it)
    acc_scr, act_scr = next(it), next(it)
    k = pl.program_id(1)

    for s in range(ug_ref.shape[1] // LANES):
        cs = slice(s * LANES, (s + 1) * LANES)
        gate = _dwconv(ug_ref[:, cs].astype(F32), cwg_ref[:, cs], cbg_ref[:, cs], width, rows)
        val = _dwconv(uv_ref[:, cs].astype(F32), cwv_ref[:, cs], cbv_ref[:, cs], width, rows)
        act_scr[:, cs] = (_silu(gate) * val).astype(BF16)
    part = jnp.dot(act_scr[...], wd_ref[...], preferred_element_type=F32)

    @pl.when(k == 0)
    def _():
        acc_scr[...] = part

    @pl.when(k > 0)
    def _():
        acc_scr[...] += part

    @pl.when(k == pl.num_programs(1) - 1)
    def _():
        x2 = x1_ref[...] + g2_ref[...] * acc_scr[...]
        if final:
            o_ref[...] = _rms(x2) * nw_ref[...]
        else:
            o_ref[...] = x2
            hn_ref[...] = (_rms(x2) * nw_ref[...] * (1.0 + sc_ref[...]) + sh_ref[...]).astype(BF16)


def _ffndown_call(x1, u, mod3, conv_w, conv_b, w_down, layer, nw, mod3_next, seq_len, width, rows):
    final = mod3_next is None
    m = x1.shape[0]
    tokens = 1024
    assert tokens == seq_len or rows == 1
    nk = 2
    tk = D_FF // nk
    bc = mod3.shape[0]
    seq_of = (lambda b: (b * tokens) // seq_len) if bc > 1 else (lambda b: 0)
    row_spec = pl.BlockSpec((tokens, D_MODEL), lambda b, k: (b, 0))
    modspec = lambda part: pl.BlockSpec((None, 1, D_MODEL), lambda b, k, part=part: (seq_of(b), 0, part))
    in_specs = [
        row_spec,
        pl.BlockSpec((tokens, tk), lambda b, k: (b, k)),
        pl.BlockSpec((tokens, tk), lambda b, k: (b, nk + k)),
        modspec(5),
        pl.BlockSpec((9, tk), lambda b, k: (0, k)),
        pl.BlockSpec((9, tk), lambda b, k: (0, nk + k)),
        pl.BlockSpec((1, tk), lambda b, k: (0, k)),
        pl.BlockSpec((1, tk), lambda b, k: (0, nk + k)),
        pl.BlockSpec((None, tk, D_MODEL), lambda b, k: (layer, k, 0)),
        pl.BlockSpec((1, D_MODEL), lambda b, k: (0, 0)),
    ]
    args = [x1, u, u, mod3, conv_w, conv_w, conv_b, conv_b, w_down, nw]
    out_specs = [row_spec]
    out_shape = [jax.ShapeDtypeStruct((m, D_MODEL), F32)]
    if not final:
        in_specs += [modspec(0), modspec(1)]
        args += [mod3_next, mod3_next]
        out_specs.append(row_spec)
        out_shape.append(jax.ShapeDtypeStruct((m, D_MODEL), BF16))
    res = pl.pallas_call(
        functools.partial(_ffndown_kernel, width=width, rows=rows, final=final),
        grid=(m // tokens, nk),
        in_specs=in_specs, out_specs=out_specs, out_shape=out_shape,
        scratch_shapes=[pltpu.VMEM((tokens, D_MODEL), F32), pltpu.VMEM((tokens, tk), BF16)],
        compiler_params=_cparams(("arbitrary", "arbitrary")),
        name="ffndown",
    )(*args)
    return (res[0], None) if final else (res[0], res[1])


def _ssd_param_rows(a_log, dt_bias, d_skip):
    def per_group(v):
        return v.reshape(2, SSD_GROUPS, HEADS_PER_GROUP).transpose(1, 0, 2).reshape(SSD_GROUPS, 2 * HEADS_PER_GROUP)

    def rows(v):
        return jnp.pad(per_group(v), ((0, 0), (0, LANES - 2 * HEADS_PER_GROUP))).reshape(SSD_GROUPS, 1, LANES)
    a_cols = jnp.broadcast_to(per_group(a_log)[:, :, None], (SSD_GROUPS, 2 * HEADS_PER_GROUP, LANES))
    return rows(a_log), a_cols, rows(dt_bias), jnp.repeat(d_skip, SSD_HEADDIM).reshape(1, SSD_W)


def _run_pass(x, mod_all_rows, s_h0, s_s0, layers, hgrn_lb, final_norm_w, consts, width, rows, want_state):
    batch, seq_len, _ = x.shape
    m = batch * seq_len
    x2 = x.reshape(m, D_MODEL)
    tl, tu = consts[0], consts[1]
    prev_h, prev_s = [], []
    new_h, new_s = None, None
    hn = _norm_call(x2, mod_all_rows[0], layers[0]['norm_w1'], seq_len)
    for l, p in enumerate(layers):
        mod3 = mod_all_rows[l]
        last = l == DEPTH - 1
        o_h, new_h = _hgrn_call(hn, p['w_in'], hgrn_lb, p['hgrn_norm_w'], consts,
                                s_h0, prev_h, l, batch, seq_len, want_state)
        o_s, new_s = _ssd_call(hn, p['w_in'], p['ssd'], (tl, tu),
                               s_s0, prev_s, l, batch, seq_len, width, rows, want_state)
        prev_h.append(new_h)
        prev_s.append(new_s)
        x1, u = _outup_call(x2, o_h, o_s, mod3, p['norm_w2'], p['w_out'], p['ffn_up'], l, seq_len)
        x2, hn = _ffndown_call(x1, u, mod3, p['ffn_conv_w'], p['ffn_conv_b'], p['ffn_down'], l,
                               final_norm_w if last else layers[l + 1]['norm_w1'],
                               None if last else mod_all_rows[l + 1], seq_len, width, rows)
    if want_state and DEPTH == 1:
        new_h, new_s = new_h[:, None], new_s[:, None]
    return x2.reshape(batch, seq_len, D_MODEL), new_h, new_s


def kernel(x_prompt, x_sample, c, state_hgrn, state_ssd, c_ctx, norm_w, final_norm_w, w_ada, b_ada,
           w_in, w_out, hgrn_lb, hgrn_norm_w, ssd_conv_w, ssd_conv_b, ssd_a_log, ssd_dt_bias, ssd_d,
           ssd_norm_w, ffn_up, ffn_conv_w, ffn_conv_b, ffn_down):
    dec_batch = c.shape[0]
    consts = _scan_constants() + (_level_signs(),)

    cond8 = jnp.concatenate([c_ctx[None], c, jnp.zeros((8 - 1 - dec_batch, D_MODEL), F32)], axis=0)
    mod_all = _mod_call(cond8, w_ada, b_ada)
    mod_ctx = [mod_all[l, 0:1].reshape(1, 1, 6 * D_MODEL) for l in range(DEPTH)]
    mod_lat = [mod_all[l, 1:1 + dec_batch].reshape(dec_batch, 1, 6 * D_MODEL) for l in range(DEPTH)]

    w_in_t = jnp.swapaxes(w_in, 1, 2)
    w_out_b, ffn_up_b, ffn_down_b = w_out.astype(BF16), ffn_up.astype(BF16), ffn_down.astype(BF16)
    layers = []
    for l in range(DEPTH):
        a_rows, a_cols, b_rows, d_rows = _ssd_param_rows(ssd_a_log[l], ssd_dt_bias[l], ssd_d[l])
        layers.append(dict(
            norm_w1=norm_w[l, 0].reshape(1, D_MODEL), norm_w2=norm_w[l, 1].reshape(1, D_MODEL),
            w_in=w_in_t,
            w_out=w_out_b,
            hgrn_norm_w=hgrn_norm_w[l].reshape(1, HGRN_DV),
            ssd=dict(conv_w=ssd_conv_w[l].reshape(9, CONV_CH), conv_b=ssd_conv_b[l].reshape(1, CONV_CH),
                     a_log_rows=a_rows, a_log_cols=a_cols, dt_bias_rows=b_rows, d_rows=d_rows,
                     norm_w=ssd_norm_w[l].reshape(1, SSD_W)),
            ffn_up=ffn_up_b,
            ffn_conv_w=ffn_conv_w[l].reshape(9, 2 * D_FF), ffn_conv_b=ffn_conv_b[l].reshape(1, 2 * D_FF),
            ffn_down=ffn_down_b,
        ))
    fnw = final_norm_w.reshape(1, D_MODEL)

    y_prompt, new_h, new_s = _run_pass(x_prompt, mod_ctx, None, None, layers, hgrn_lb, fnw, consts,
                                       width=x_prompt.shape[1], rows=1, want_state=True)
    y_sample, _, _ = _run_pass(x_sample, mod_lat, state_hgrn, state_ssd, layers, hgrn_lb, fnw, consts,
                               width=GRID_W, rows=x_sample.shape[1] // GRID_W, want_state=False)
    return (y_prompt, y_sample, new_h, new_s)
```

```python
import functools
import math

import numpy as np
import jax
import jax.numpy as jnp
from jax import lax
from jax.experimental import pallas as pl
from jax.experimental.pallas import tpu as pltpu

F32 = jnp.float32
BF16 = jnp.bfloat16

D_MODEL = 1024
DEPTH = 2
GRID_W = 64
HGRN_HEADS = 8
HGRN_DK = 128
HGRN_DV = 128
HGRN_KW = HGRN_HEADS * HGRN_DK
HGRN_W = HGRN_HEADS * HGRN_DV
SSD_W = 1024
SSD_HEADDIM = 64
SSD_HEADS = 16
SSD_GROUPS = 4
SSD_STATE = 128
HEADS_PER_GROUP = SSD_HEADS // SSD_GROUPS
GROUP_W = SSD_W // SSD_GROUPS
MIX_W = HGRN_W + SSD_W
CONV_CH = SSD_W + 2 * SSD_GROUPS * SSD_STATE
D_FF = 2816
MAIN_COLS = 3 * HGRN_KW + 2 * HGRN_W + SSD_W + CONV_CH
DT_COLS = 2 * SSD_HEADS
EPS = 1e-6
LOG2E = math.log2(math.e)

LANES = 128
CHUNK = 128
N_LEVELS = 7
TOKENS_PER_STEP = 1024
VMEM_LIMIT = 56 * 1024 * 1024


def _cparams(sem):
    return pltpu.CompilerParams(dimension_semantics=sem, vmem_limit_bytes=VMEM_LIMIT)


def _sigmoid(x):
    return 1.0 / (1.0 + jnp.exp(-x))


def _silu(x):
    return x * _sigmoid(x)


def _softplus(x):
    return jnp.maximum(x, 0.0) + jnp.log1p(jnp.exp(-jnp.abs(x)))


def _rms(x):
    return x * lax.rsqrt(jnp.mean(x * x, axis=-1, keepdims=True) + EPS)


def _split3(x):
    hi = x.astype(BF16)
    r = x - hi.astype(F32)
    mid = r.astype(BF16)
    lo = (r - mid.astype(F32)).astype(BF16)
    return hi, mid, lo


def _sum_rows_exact(w01, x):
    n = x.shape[1]
    hi, mid, lo = _split3(x)
    p = jnp.dot(w01, jnp.concatenate([hi, mid, lo], axis=1), preferred_element_type=F32)
    return p[:, :n] + p[:, n:2 * n] + p[:, 2 * n:]


def _sum_cols_exact(x, w01):
    m = x.shape[0]
    hi, mid, lo = _split3(x)
    p = jnp.dot(jnp.concatenate([hi, mid, lo], axis=0), w01, preferred_element_type=F32)
    return p[:m] + p[m:2 * m] + p[2 * m:]


def _dot_nt(a, b):
    return lax.dot_general(a, b, (((1,), (1,)), ((), ())), preferred_element_type=F32)


def _dot_tn(a, b):
    return lax.dot_general(a, b, (((0,), (0,)), ((), ())), preferred_element_type=F32)


def _dwconv(x, w_ref, b_ref, width, rows):
    seq, _ = x.shape
    t = lax.broadcasted_iota(jnp.int32, x.shape, 0)
    col = jnp.bitwise_and(t, width - 1)
    xl = jnp.where(col == 0, 0.0, pltpu.roll(x, 1, axis=0))
    xr = jnp.where(col == width - 1, 0.0, pltpu.roll(x, seq - 1, axis=0))

    def hrow(kh, a, b, c):
        return w_ref[3 * kh:3 * kh + 1, :] * a + w_ref[3 * kh + 1:3 * kh + 2, :] * b \
            + w_ref[3 * kh + 2:3 * kh + 3, :] * c

    out = b_ref[0:1, :] + hrow(1, xl, x, xr)
    if rows > 1:
        inner = seq - width
        up = hrow(0, xl[:inner], x[:inner], xr[:inner])
        out = jnp.concatenate([out[:width], out[width:] + up], axis=0)
        dn = hrow(2, xl[width:], x[width:], xr[width:])
        out = jnp.concatenate([out[:inner] + dn, out[inner:]], axis=0)
    return out


def _scan_constants():
    c = CHUNK
    t = np.arange(c)[:, None]
    i = np.arange(c)[None, :]
    x = t ^ i
    lv = np.where(x == 0, 0, np.floor(np.log2(np.maximum(x, 1))).astype(np.int64) + 1)
    lv = np.where(i > t, -lv, lv)
    return jnp.asarray(i <= t, BF16), jnp.asarray(i >= t, BF16), jnp.asarray(lv, jnp.int32)


def _level_signs():
    t = np.arange(CHUNK)[:, None]
    sg = [np.where(((t >> lev) & 1) == 1, 1.0, -1.0) * np.ones((1, LANES)) for lev in range(N_LEVELS)]
    return jnp.asarray(np.concatenate(sg, 0), F32)


def _mod_kernel(cond_ref, w_ref, b_ref, o_ref):
    a = _silu(cond_ref[...]).astype(BF16)
    o_ref[...] = jnp.dot(a, w_ref[...].astype(BF16), preferred_element_type=F32) + b_ref[...]


def _mod_call(cond8, w_ada, b_ada):
    n = 6 * D_MODEL
    tn = D_MODEL
    return pl.pallas_call(
        _mod_kernel,
        grid=(DEPTH, n // tn),
        in_specs=[
            pl.BlockSpec((8, D_MODEL), lambda l, j: (0, 0)),
            pl.BlockSpec((None, D_MODEL, tn), lambda l, j: (l, 0, j)),
            pl.BlockSpec((None, 1, tn), lambda l, j: (l, 0, j)),
        ],
        out_specs=pl.BlockSpec((None, 8, tn), lambda l, j: (l, 0, j)),
        out_shape=jax.ShapeDtypeStruct((DEPTH, 8, n), F32),
        compiler_params=_cparams(("arbitrary", "arbitrary")),
        name="mod",
    )(cond8, w_ada, b_ada.reshape(DEPTH, 1, n))


def _norm_kernel(x_ref, sh_ref, sc_ref, nw_ref, hn_ref):
    h = _rms(x_ref[...]) * nw_ref[...] * (1.0 + sc_ref[...]) + sh_ref[...]
    hn_ref[...] = h.astype(BF16)


def _norm_call(x2, mod3, nw, seq_len):
    m = x2.shape[0]
    tm = 512
    bc = mod3.shape[0]
    seq_of = (lambda i: (i * tm) // seq_len) if bc > 1 else (lambda i: 0)
    return pl.pallas_call(
        _norm_kernel,
        grid=(m // tm,),
        in_specs=[
            pl.BlockSpec((tm, D_MODEL), lambda i: (i, 0)),
            pl.BlockSpec((None, 1, D_MODEL), lambda i: (seq_of(i), 0, 0)),
            pl.BlockSpec((None, 1, D_MODEL), lambda i: (seq_of(i), 0, 1)),
            pl.BlockSpec((1, D_MODEL), lambda i: (0, 0)),
        ],
        out_specs=pl.BlockSpec((tm, D_MODEL), lambda i: (i, 0)),
        out_shape=jax.ShapeDtypeStruct((m, D_MODEL), BF16),
        compiler_params=_cparams(("arbitrary",)),
        name="norm",
    )(x2, mod3, mod3, nw)


def _hgrn_gates(f_pre, lbd):
    ea = jnp.exp(-jnp.abs(f_pre))
    log2_sig = jnp.minimum(f_pre, 0.0) * LOG2E - jnp.log2(1.0 + ea)
    sig_neg = jnp.where(f_pre >= 0.0, ea, 1.0) / (1.0 + ea)
    if lbd is None:
        return log2_sig, sig_neg
    a = jnp.log2(lbd)
    b = jnp.log2(1.0 - lbd) + log2_sig
    log2_f = jnp.maximum(a, b) + jnp.log2(1.0 + jnp.exp2(-jnp.abs(a - b)))
    return log2_f, (1.0 - lbd) * sig_neg


def _block_mid(scr, m):
    return jnp.concatenate(
        [jnp.broadcast_to(scr[r0 + m - 1:r0 + m, :], (2 * m, scr.shape[1])) for r0 in range(0, CHUNK, 2 * m)],
        axis=0)


def _mid_distance(c, scr, m):
    pieces = []
    for r0 in range(0, CHUNK, 2 * m):
        mid = jnp.broadcast_to(scr[r0 + m - 1:r0 + m, :], (m, scr.shape[1]))
        pieces += [mid - c[r0:r0 + m], c[r0 + m:r0 + 2 * m] - mid]
    return jnp.concatenate(pieces, axis=0)


def _pick_halves(up_val, low_val, m):
    pieces = []
    for r0 in range(0, CHUNK, 2 * m):
        pieces += [low_val[r0:r0 + m], up_val[r0 + m:r0 + 2 * m]]
    return jnp.concatenate(pieces, axis=0)


def _hgrn_kernel(*refs, layer, seq_len, seqs, has_s0, n_prev, want_state):
    it = iter(refs)
    hn_ref = next(it)
    w_refs = [next(it) for _ in range(5)]
    lb_ref, nw_ref, tl_ref, lv_ref, sg_ref = (next(it) for _ in range(5))
    s0_ref = next(it) if has_s0 else None
    prev_refs = [next(it) for _ in range(n_prev)]
    o_ref = next(it)
    sf_ref = next(it) if want_state else None
    if n_prev:
        for i, pref in enumerate(prev_refs):
            sf_ref[:, i] = pref[...]
        sf_own = sf_ref.at[:, layer]
    else:
        sf_own = sf_ref
    (w_scr, p_scr, acc_scr, qc_scr, u_scr, dl_scr, st_scr, cumf_scr, cumb_scr) = (next(it) for _ in range(9))

    c = CHUNK
    dk = HGRN_DK
    n_chunks = seqs * seq_len // c
    cps = seq_len // c

    @pl.when(pl.program_id(1) == 0)
    def _():
        for i, w_ref in enumerate(w_refs):
            w_scr[i * LANES:(i + 1) * LANES, :] = w_ref[...].astype(BF16)

    p_scr[...] = _dot_nt(hn_ref[...], w_scr[...])

    lb = None
    if layer > 0:
        lbr = lb_ref[...]
        e = jnp.exp(lbr - jnp.max(lbr, axis=0, keepdims=True))
        sm = e / jnp.sum(e, axis=0, keepdims=True)
        lb = sm[1]
        for i in range(2, layer + 1):
            lb = lb + sm[i]

    lvs = lv_ref[...]
    lv = jnp.abs(lvs)
    tl = tl_ref[...]
    zero = jnp.zeros((c, dk), F32)
    odd = jnp.bitwise_and(lax.broadcasted_iota(jnp.int32, (c, dk), 0), 1) == 1

    def intra(ci, carry):
        r0 = pl.multiple_of(ci * c, c)
        sl = pl.ds(r0, c)
        q = _silu(p_scr[sl, 0:dk]) * (dk ** -0.5)
        lf_f, kf = _hgrn_gates(p_scr[sl, dk:2 * dk], None if lb is None else lb[0:1, :])
        lf_b, kb = _hgrn_gates(p_scr[sl, 2 * dk:3 * dk], None if lb is None else lb[1:2, :])
        vb = p_scr[sl, 3 * dk:4 * dk].astype(BF16)

        cum_f = _sum_rows_exact(tl, lf_f)
        cum_b = _sum_rows_exact(tl, lf_b)
        cx_b = cum_b - lf_b
        cumf_scr[...] = cum_f
        cumb_scr[...] = cum_b
        tot_f = cum_f[c - 1:c, :]
        tot_b = cum_b[c - 1:c, :]

        a_sum = _dot_nt(q.astype(BF16), (kf + kb).astype(BF16))

        f_f = jnp.exp2(lf_f)
        f_b = jnp.exp2(lf_b)
        qf = q * f_f
        qb = q * f_b
        p_f = _dot_nt(qf.astype(BF16), kf.astype(BF16))
        p_b = _dot_nt(qb.astype(BF16), kb.astype(BF16))
        a_sum = jnp.where(lvs == 1, p_f, jnp.where(lvs == -1, p_b, a_sum))
        p_f = _dot_nt(jnp.where(odd, qf * pltpu.roll(f_f, 1, axis=0), qf).astype(BF16),
                      jnp.where(odd, kf, kf * pltpu.roll(f_f, c - 1, axis=0)).astype(BF16))
        p_b = _dot_nt(jnp.where(odd, qb, qb * pltpu.roll(f_b, c - 1, axis=0)).astype(BF16),
                      jnp.where(odd, kb * pltpu.roll(f_b, 1, axis=0), kb).astype(BF16))
        a_sum = jnp.where(lvs == 2, p_f, jnp.where(lvs == -2, p_b, a_sum))

        sgn = sg_ref[2 * c:3 * c, :]
        ef = jnp.exp2((cum_f - _block_mid(cumf_scr, 4)) * sgn)
        eb = jnp.exp2((cx_b - _block_mid(cumb_scr, 4)) * sgn)
        p_f = _dot_nt((q * ef).astype(BF16), (kf * ef).astype(BF16))
        p_b = _dot_nt((q * eb).astype(BF16), (kb * eb).astype(BF16))
        a_sum = jnp.where(lvs == 3, p_f, jnp.where(lvs == -3, p_b, a_sum))

        for lev in range(4, N_LEVELS + 1):
            m = 1 << (lev - 1)
            ef = jnp.exp2(_mid_distance(cum_f, cumf_scr, m))
            eb = jnp.exp2(_mid_distance(cx_b, cumb_scr, m))
            qe = q * _pick_halves(ef, eb, m)
            ke = _pick_halves(kb, kf, m) * _pick_halves(eb, ef, m)
            lhs = jnp.concatenate([_pick_halves(qe, zero, m), _pick_halves(zero, qe, m)], axis=1)
            rhs = jnp.concatenate([_pick_halves(zero, ke, m), _pick_halves(ke, zero, m)], axis=1)
            a_sum = jnp.where(lv == lev, _dot_nt(lhs.astype(BF16), rhs.astype(BF16)), a_sum)
        acc_scr[sl, :] = jnp.dot(a_sum.astype(BF16), vb, preferred_element_type=F32)

        qc_scr[sl, :] = jnp.concatenate(
            [q * jnp.exp2(cum_f), q * jnp.exp2(jnp.minimum(tot_b - cx_b, 0.0))], axis=1).astype(BF16)
        u_scr[ci, 0] = _dot_tn(vb, (kf * jnp.exp2(jnp.minimum(tot_f - cum_f, 0.0))).astype(BF16))
        u_scr[ci, 1] = _dot_tn(vb, (kb * jnp.exp2(jnp.minimum(cx_b, 0.0))).astype(BF16))
        dl_scr[ci, 0:1, :] = jnp.exp2(tot_f)
        dl_scr[ci, 1:2, :] = jnp.exp2(tot_b)
        return carry

    lax.fori_loop(0, n_chunks, intra, 0, unroll=4)

    def init(s, d):
        return s0_ref[s, d].T if has_s0 else jnp.zeros((HGRN_DV, dk), F32)

    st = None
    for ci in range(n_chunks):
        s = ci // cps
        if ci % cps == 0:
            st = init(s, 0)
        st_scr[ci, :, 0:dk] = st.astype(BF16)
        st = st * dl_scr[ci, 0:1, :] + u_scr[ci, 0]
        if want_state and ci % cps == cps - 1:
            sf_own[s, 0] = st.T
    for ci in reversed(range(n_chunks)):
        s = ci // cps
        if ci % cps == cps - 1:
            st = init(s, 1)
        st_scr[ci, :, dk:2 * dk] = st.astype(BF16)
        st = st * dl_scr[ci, 1:2, :] + u_scr[ci, 1]
        if want_state and ci % cps == 0:
            sf_own[s, 1] = st.T

    nw = nw_ref[...]

    for ci in range(n_chunks):
        sl = slice(ci * c, (ci + 1) * c)
        o = acc_scr[sl, :] + _dot_nt(qc_scr[sl, :], st_scr[ci])
        o_ref[sl, :] = (_rms(o) * nw * _silu(p_scr[sl, 4 * dk:5 * dk])).astype(o_ref.dtype)


def _hgrn_call(hn, w_in_t, lb_raw, nw, consts, s0, prev_states, layer, batch, seq_len, want_state):
    tl, _, lv, sg = consts
    has_s0 = s0 is not None
    m = hn.shape[0]
    tokens = TOKENS_PER_STEP
    seqs = tokens // seq_len
    n_chunks = tokens // CHUNK
    full = lambda arr: pl.BlockSpec(arr.shape, lambda h, b: (0,) * arr.ndim)
    state_spec = pl.BlockSpec((seqs, None, 2, None, HGRN_DK, HGRN_DV), lambda h, b: (b, layer, 0, h, 0, 0))
    wcol = lambda part: pl.BlockSpec((None, LANES, D_MODEL),
                                     lambda h, b, part=part: (layer, part * HGRN_HEADS + h, 0))
    in_specs = [pl.BlockSpec((tokens, D_MODEL), lambda h, b: (b, 0)),
                wcol(0), wcol(1), wcol(2), wcol(3), wcol(4),
                pl.BlockSpec((DEPTH, 2, LANES), lambda h, b: (0, 0, h)),
                pl.BlockSpec((1, HGRN_DV), lambda h, b: (0, 0)),
                full(tl), full(lv), full(sg)]
    args = [hn, w_in_t, w_in_t, w_in_t, w_in_t, w_in_t, lb_raw, nw, tl, lv, sg]
    if has_s0:
        in_specs.append(state_spec)
        args.append(s0)
    out_specs = [pl.BlockSpec((tokens, LANES), lambda h, b: (b, h))]
    out_shape = [jax.ShapeDtypeStruct((m, HGRN_W), BF16)]
    n_prev = 0
    if want_state:
        own_spec = pl.BlockSpec((seqs, 2, None, HGRN_DK, HGRN_DV), lambda h, b: (b, 0, h, 0, 0))
        own_shape = (batch, 2, HGRN_HEADS, HGRN_DK, HGRN_DV)
        if layer == DEPTH - 1 and prev_states:
            n_prev = len(prev_states)
            in_specs += [own_spec] * n_prev
            args += list(prev_states)
            out_specs.append(pl.BlockSpec((seqs, DEPTH, 2, None, HGRN_DK, HGRN_DV),
                                          lambda h, b: (b, 0, 0, h, 0, 0)))
            out_shape.append(jax.ShapeDtypeStruct((batch, DEPTH) + own_shape[1:], F32))
        else:
            out_specs.append(own_spec)
            out_shape.append(jax.ShapeDtypeStruct(own_shape, F32))
    scratch = [pltpu.VMEM((5 * LANES, D_MODEL), BF16),
               pltpu.VMEM((tokens, 5 * LANES), F32),
               pltpu.VMEM((tokens, HGRN_DV), F32),
               pltpu.VMEM((tokens, 2 * HGRN_DK), BF16),
               pltpu.VMEM((n_chunks, 2, HGRN_DV, HGRN_DK), F32),
               pltpu.VMEM((n_chunks, 8, HGRN_DK), F32),
               pltpu.VMEM((n_chunks, HGRN_DV, 2 * HGRN_DK), BF16),
               pltpu.VMEM((CHUNK, HGRN_DK), F32),
               pltpu.VMEM((CHUNK, HGRN_DK), F32)]
    res = pl.pallas_call(
        functools.partial(_hgrn_kernel, layer=layer, seq_len=seq_len, seqs=seqs, has_s0=has_s0,
                          n_prev=n_prev, want_state=want_state),
        grid=(HGRN_HEADS, m // tokens),
        in_specs=in_specs, out_specs=out_specs, out_shape=out_shape,
        scratch_shapes=scratch,
        compiler_params=_cparams(("arbitrary", "arbitrary")),
        name="hgrn",
    )(*args)
    return (res[0], res[1]) if want_state else (res[0], None)


def _ssd_kernel(*refs, layer, seq_len, seqs, width, rows, has_s0, n_prev, want_state):
    it = iter(refs)
    (hn_ref, wz_ref, wx_ref, wb_ref, wc_ref, wdt_ref, cwx_ref, cwb_ref, cwc_ref, cbx_ref, cbb_ref, cbc_ref,
     alog_ref, acol_ref, dtb_ref, dsk_ref, nw_ref, tl_ref, tu_ref) = (next(it) for _ in range(19))
    s0_ref = next(it) if has_s0 else None
    prev_refs = [next(it) for _ in range(n_prev)]
    o_ref = next(it)
    sf_ref = next(it) if want_state else None
    if n_prev:
        for i, pref in enumerate(prev_refs):
            sf_ref[:, i] = pref[...]
        sf_own = sf_ref.at[:, layer]
    else:
        sf_own = sf_ref
    (w_scr, p_scr, xc_scr, bc_scr, cc_scr, dtg_scr, acc_scr, ecum_scr, u_scr, dl_scr, st_scr) = (
        next(it) for _ in range(11))

    q = CHUNK
    n_chunks = seqs * seq_len // q
    cps = seq_len // q
    hp = HEADS_PER_GROUP
    p = SSD_HEADDIM
    gw = GROUP_W
    n = SSD_STATE
    dt0 = 2 * gw + 2 * n

    @pl.when(pl.program_id(1) == 0)
    def _():
        w_scr[0:gw, :] = wz_ref[...].astype(BF16)
        w_scr[gw:2 * gw, :] = wx_ref[...].astype(BF16)
        w_scr[2 * gw:2 * gw + n, :] = wb_ref[...].astype(BF16)
        w_scr[2 * gw + n:2 * gw + 2 * n, :] = wc_ref[...].astype(BF16)
        w_scr[dt0:dt0 + DT_COLS, :] = wdt_ref[...].astype(BF16)
        w_scr[dt0 + DT_COLS:dt0 + LANES, :] = jnp.zeros((LANES - DT_COLS, D_MODEL), BF16)

    hn = hn_ref[...]
    p_scr[...] = _dot_nt(hn, w_scr[0:gw, :])
    px = _dot_nt(hn, w_scr[gw:2 * gw, :])
    for s in range(gw // LANES):
        cs = slice(s * LANES, (s + 1) * LANES)
        xc_scr[:, cs] = _silu(_dwconv(px[:, cs], cwx_ref[:, cs], cbx_ref[:, cs], width, rows))
    pbc = _dot_nt(hn, w_scr[2 * gw:2 * gw + 2 * n, :])
    bc_scr[...] = _silu(_dwconv(pbc[:, 0:n], cwb_ref, cbb_ref, width, rows))
    cc_scr[...] = _silu(_dwconv(pbc[:, n:2 * n], cwc_ref, cbc_ref, width, rows)).astype(BF16)
    g = pl.program_id(0)
    dt_all = _dot_nt(hn, w_scr[dt0:dt0 + LANES, :])
    lane = lax.broadcasted_iota(jnp.int32, dt_all.shape, 1)
    dt_fw = pltpu.roll(dt_all, jnp.bitwise_and(LANES - hp * g, LANES - 1), axis=1)
    dt_bw = pltpu.roll(dt_all, jnp.bitwise_and(LANES - (SSD_HEADS - hp) - hp * g, LANES - 1), axis=1)
    dtg_scr[...] = _softplus(jnp.where(lane < hp, dt_fw, dt_bw) + dtb_ref[...])
    neg_a = -jnp.exp(alog_ref[...])
    neg_a_t = -jnp.exp(acol_ref[...])

    tl = tl_ref[...]
    tu = tu_ref[...]
    ti = lax.broadcasted_iota(jnp.int32, (q, q), 0)
    si = lax.broadcasted_iota(jnp.int32, (q, q), 1)
    lower = si <= ti
    upper = si >= ti

    def intra(ci, carry):
        r0 = pl.multiple_of(ci * q, q)
        sl = pl.ds(r0, q)
        dt = dtg_scr[sl, :]
        da = dt * neg_a
        cum_f = _sum_rows_exact(tl, da)
        cum_b = _sum_rows_exact(tu, da)
        dt_t = dt.T[0:2 * hp, :]
        da_t = dt_t * neg_a_t
        row_f = _sum_cols_exact(da_t, tu)
        row_b = _sum_cols_exact(da_t, tl)
        wr_f = jnp.exp(row_f[:, q - 1:q] - row_f) * dt_t
        wr_b = jnp.exp(row_b[:, 0:1] - row_b) * dt_t
        xb = xc_scr[sl, :].astype(BF16)
        bm = bc_scr[sl, :]
        bt = bm.T
        gmat = _dot_nt(cc_scr[sl, :], bm.astype(BF16))
        ecum_scr[sl, 0:LANES] = jnp.exp(cum_f)
        ecum_scr[sl, LANES:2 * LANES] = jnp.exp(cum_b)
        ys, u_f, u_b = [], [], []
        for j in range(hp):
            jf, jb = j, hp + j
            dec_f = jnp.exp(jnp.minimum(cum_f[:, jf:jf + 1] - row_f[jf:jf + 1, :], 0.0)) * dt_t[jf:jf + 1, :]
            dec_b = jnp.exp(jnp.minimum(cum_b[:, jb:jb + 1] - row_b[jb:jb + 1, :], 0.0)) * dt_t[jb:jb + 1, :]
            mh = gmat * (jnp.where(lower, dec_f, 0.0) + jnp.where(upper, dec_b, 0.0))
            xh = xb[:, j * p:(j + 1) * p]
            ys.append(jnp.dot(mh.astype(BF16), xh, preferred_element_type=F32))
            u_f.append(jnp.dot((bt * wr_f[jf:jf + 1, :]).astype(BF16), xh, preferred_element_type=F32))
            u_b.append(jnp.dot((bt * wr_b[jb:jb + 1, :]).astype(BF16), xh, preferred_element_type=F32))
        acc_scr[sl, :] = jnp.concatenate(ys, axis=1)
        u_scr[ci, 0] = jnp.concatenate(u_f, axis=1)
        u_scr[ci, 1] = jnp.concatenate(u_b, axis=1)
        dl_scr[ci, 0:1, :] = jnp.exp(cum_f[q - 1:q, :])
        dl_scr[ci, 1:2, :] = jnp.exp(cum_b[0:1, :])
        return carry

    lax.fori_loop(0, n_chunks, intra, 0, unroll=4)

    def head_scale(vec, lane0):
        return jnp.concatenate(
            [jnp.broadcast_to(vec[0:1, lane0 + j:lane0 + j + 1], (1, p)) for j in range(hp)], axis=1)

    def init(s, d):
        return s0_ref[s, d].reshape(hp * p, n).T if has_s0 else jnp.zeros((n, hp * p), F32)

    st = None
    for ci in range(n_chunks):
        s = ci // cps
        if ci % cps == 0:
            st = init(s, 0)
        st_scr[ci, :, 0:hp * p] = st.astype(BF16)
        st = st * head_scale(dl_scr[ci, 0:1, :], 0) + u_scr[ci, 0]
        if want_state and ci % cps == cps - 1:
            sf_own[s, 0] = st.T.reshape(hp, p, n)
    for ci in reversed(range(n_chunks)):
        s = ci // cps
        if ci % cps == cps - 1:
            st = init(s, 1)
        st_scr[ci, :, hp * p:2 * hp * p] = st.astype(BF16)
        st = st * head_scale(dl_scr[ci, 1:2, :], hp) + u_scr[ci, 1]
        if want_state and ci % cps == 0:
            sf_own[s, 1] = st.T.reshape(hp, p, n)

    dsk = dsk_ref[...]
    nw = nw_ref[...]
    for ci in range(n_chunks):
        sl = slice(ci * q, (ci + 1) * q)
        yi = jnp.dot(cc_scr[sl, :], st_scr[ci], preferred_element_type=F32)
        ec = ecum_scr[sl, :]
        parts = []
        for j in range(hp):
            parts.append(yi[:, j * p:(j + 1) * p] * ec[:, j:j + 1]
                         + yi[:, (hp + j) * p:(hp + j + 1) * p] * ec[:, LANES + hp + j:LANES + hp + j + 1])
        y = acc_scr[sl, :] + jnp.concatenate(parts, axis=1) + dsk * xc_scr[sl, :]
        y = y * _silu(p_scr[sl, 0:gw])
        o_ref[sl, :] = (_rms(y) * nw).astype(o_ref.dtype)


def _ssd_call(hn, w_in_t, p, consts, s0, prev_states, layer, batch, seq_len, width, rows, want_state):
    tl, tu = consts
    has_s0 = s0 is not None
    m = hn.shape[0]
    tokens = TOKENS_PER_STEP
    seqs = tokens // seq_len
    assert seqs == 1 or rows == 1
    n_chunks = tokens // CHUNK
    wcols = 2 * GROUP_W + 2 * SSD_STATE + LANES
    xb = SSD_W // LANES
    z0 = 5 * HGRN_KW // GROUP_W
    x0 = z0 + SSD_W // GROUP_W
    b0 = (5 * HGRN_KW + 2 * SSD_W) // LANES
    c0 = b0 + SSD_GROUPS
    d0 = MAIN_COLS // DT_COLS
    full = lambda arr: pl.BlockSpec(arr.shape, lambda g, b: (0,) * arr.ndim)
    in_specs = [
        pl.BlockSpec((tokens, D_MODEL), lambda g, b: (b, 0)),
        pl.BlockSpec((None, GROUP_W, D_MODEL), lambda g, b: (layer, z0 + g, 0)),
        pl.BlockSpec((None, GROUP_W, D_MODEL), lambda g, b: (layer, x0 + g, 0)),
        pl.BlockSpec((None, LANES, D_MODEL), lambda g, b: (layer, b0 + g, 0)),
        pl.BlockSpec((None, LANES, D_MODEL), lambda g, b: (layer, c0 + g, 0)),
        pl.BlockSpec((None, DT_COLS, D_MODEL), lambda g, b: (layer, d0, 0)),
        pl.BlockSpec((9, GROUP_W), lambda g, b: (0, g)),
        pl.BlockSpec((9, LANES), lambda g, b: (0, xb + g)),
        pl.BlockSpec((9, LANES), lambda g, b: (0, xb + SSD_GROUPS + g)),
        pl.BlockSpec((1, GROUP_W), lambda g, b: (0, g)),
        pl.BlockSpec((1, LANES), lambda g, b: (0, xb + g)),
        pl.BlockSpec((1, LANES), lambda g, b: (0, xb + SSD_GROUPS + g)),
        pl.BlockSpec((None, 1, LANES), lambda g, b: (g, 0, 0)),
        pl.BlockSpec((None, 2 * HEADS_PER_GROUP, LANES), lambda g, b: (g, 0, 0)),
        pl.BlockSpec((None, 1, LANES), lambda g, b: (g, 0, 0)),
        pl.BlockSpec((1, GROUP_W), lambda g, b: (0, g)),
        pl.BlockSpec((1, GROUP_W), lambda g, b: (0, g)),
        full(tl), full(tu),
    ]
    args = [hn, w_in_t, w_in_t, w_in_t, w_in_t, w_in_t, p['conv_w'], p['conv_w'], p['conv_w'],
            p['conv_b'], p['conv_b'], p['conv_b'],
            p['a_log_rows'], p['a_log_cols'], p['dt_bias_rows'], p['d_rows'], p['norm_w'], tl, tu]
    state_spec = pl.BlockSpec((seqs, None, 2, HEADS_PER_GROUP, SSD_HEADDIM, SSD_STATE),
                              lambda g, b: (b, layer, 0, g, 0, 0))
    if has_s0:
        in_specs.append(state_spec)
        args.append(s0)
    out_specs = [pl.BlockSpec((tokens, GROUP_W), lambda g, b: (b, g))]
    out_shape = [jax.ShapeDtypeStruct((m, SSD_W), BF16)]
    n_prev = 0
    if want_state:
        own_spec = pl.BlockSpec((seqs, 2, HEADS_PER_GROUP, SSD_HEADDIM, SSD_STATE), lambda g, b: (b, 0, g, 0, 0))
        own_shape = (batch, 2, SSD_HEADS, SSD_HEADDIM, SSD_STATE)
        if layer == DEPTH - 1 and prev_states:
            n_prev = len(prev_states)
            in_specs += [own_spec] * n_prev
            args += list(prev_states)
            out_specs.append(pl.BlockSpec((seqs, DEPTH, 2, HEADS_PER_GROUP, SSD_HEADDIM, SSD_STATE),
                                          lambda g, b: (b, 0, 0, g, 0, 0)))
            out_shape.append(jax.ShapeDtypeStruct((batch, DEPTH) + own_shape[1:], F32))
        else:
            out_specs.append(own_spec)
            out_shape.append(jax.ShapeDtypeStruct(own_shape, F32))
    hpp = HEADS_PER_GROUP * SSD_HEADDIM
    scratch = [pltpu.VMEM((wcols, D_MODEL), BF16),
               pltpu.VMEM((tokens, GROUP_W), F32),
               pltpu.VMEM((tokens, GROUP_W), F32),
               pltpu.VMEM((tokens, SSD_STATE), F32),
               pltpu.VMEM((tokens, SSD_STATE), BF16),
               pltpu.VMEM((tokens, LANES), F32),
               pltpu.VMEM((tokens, GROUP_W), F32),
               pltpu.VMEM((tokens, 2 * LANES), F32),
               pltpu.VMEM((n_chunks, 2, SSD_STATE, hpp), F32),
               pltpu.VMEM((n_chunks, 8, LANES), F32),
               pltpu.VMEM((n_chunks, SSD_STATE, 2 * hpp), BF16)]
    res = pl.pallas_call(
        functools.partial(_ssd_kernel, layer=layer, seq_len=seq_len, seqs=seqs, width=width, rows=rows,
                          has_s0=has_s0, n_prev=n_prev, want_state=want_state),
        grid=(SSD_GROUPS, m // tokens),
        in_specs=in_specs, out_specs=out_specs, out_shape=out_shape,
        scratch_shapes=scratch,
        compiler_params=_cparams(("arbitrary", "arbitrary")),
        name="ssd",
    )(*args)
    return (res[0], res[1]) if want_state else (res[0], None)


def _outup_kernel(x_ref, oh_ref, os_ref, g1_ref, sh_ref, sc_ref, nw_ref, wo_ref, wu_ref,
                  x1_ref, u_ref, h_scr):
    @pl.when(pl.program_id(1) == 0)
    def _():
        mix = jnp.dot(oh_ref[...], wo_ref[0:HGRN_W, :], preferred_element_type=F32)
        mix = mix + jnp.dot(os_ref[...], wo_ref[HGRN_W:MIX_W, :], preferred_element_type=F32)
        x1 = x_ref[...] + g1_ref[...] * mix
        x1_ref[...] = x1
        h = _rms(x1) * nw_ref[...] * (1.0 + sc_ref[...]) + sh_ref[...]
        h_scr[...] = h.astype(BF16)

    u_ref[...] = jnp.dot(h_scr[...], wu_ref[...], preferred_element_type=F32).astype(u_ref.dtype)


def _outup_call(x2, o_h, o_s, mod3, nw, w_out, w_up, layer, seq_len):
    m = x2.shape[0]
    tm, tn = 512, 1408
    bc = mod3.shape[0]
    seq_of = (lambda i: (i * tm) // seq_len) if bc > 1 else (lambda i: 0)
    modspec = lambda part: pl.BlockSpec((None, 1, D_MODEL), lambda i, j, part=part: (seq_of(i), 0, part))
    return pl.pallas_call(
        _outup_kernel,
        grid=(m // tm, 2 * D_FF // tn),
        in_specs=[
            pl.BlockSpec((tm, D_MODEL), lambda i, j: (i, 0)),
            pl.BlockSpec((tm, HGRN_W), lambda i, j: (i, 0)),
            pl.BlockSpec((tm, SSD_W), lambda i, j: (i, 0)),
            modspec(2), modspec(3), modspec(4),
            pl.BlockSpec((1, D_MODEL), lambda i, j: (0, 0)),
            pl.BlockSpec((None, MIX_W, D_MODEL), lambda i, j: (layer, 0, 0)),
            pl.BlockSpec((None, D_MODEL, tn), lambda i, j: (layer, 0, j)),
        ],
        out_specs=[
            pl.BlockSpec((tm, D_MODEL), lambda i, j: (i, 0)),
            pl.BlockSpec((tm, tn), lambda i, j: (i, j)),
        ],
        out_shape=[jax.ShapeDtypeStruct((m, D_MODEL), F32),
                   jax.ShapeDtypeStruct((m, 2 * D_FF), BF16)],
        scratch_shapes=[pltpu.VMEM((tm, D_MODEL), BF16)],
        compiler_params=_cparams(("arbitrary", "arbitrary")),
        name="outup",
    )(x2, o_h, o_s, mod3, mod3, mod3, nw, w_out, w_up)


def _ffndown_kernel(*refs, width, rows, final):
    it = iter(refs)
    x1_ref, ug_ref, uv_ref, g2_ref, cwg_ref, cwv_ref, cbg_ref, cbv_ref, wd_ref, nw_ref = (
        next(it) for _ in range(10))
    sh_ref, sc_ref = (None, None) if final else (next(it), next(it))
    o_ref = next(it)
    hn_ref = None if final else next(it)
    acc_scr, act_scr = next(it), next(it)
    k = pl.program_id(1)

    for s in range(ug_ref.shape[1] // LANES):
        cs = slice(s * LANES, (s + 1) * LANES)
        gate = _dwconv(ug_ref[:, cs].astype(F32), cwg_ref[:, cs], cbg_ref[:, cs], width, rows)
        val = _dwconv(uv_ref[:, cs].astype(F32), cwv_ref[:, cs], cbv_ref[:, cs], width, rows)
        act_scr[:, cs] = (_silu(gate) * val).astype(BF16)
    part = jnp.dot(act_scr[...], wd_ref[...], preferred_element_type=F32)

    @pl.when(k == 0)
    def _():
        acc_scr[...] = part

    @pl.when(k > 0)
    def _():
        acc_scr[...] += part

    @pl.when(k == pl.num_programs(1) - 1)
    def _():
        x2 = x1_ref[...] + g2_ref[...] * acc_scr[...]
        if final:
            o_ref[...] = _rms(x2) * nw_ref[...]
        else:
            o_ref[...] = x2
            hn_ref[...] = (_rms(x2) * nw_ref[...] * (1.0 + sc_ref[...]) + sh_ref[...]).astype(BF16)


def _ffndown_call(x1, u, mod3, conv_w, conv_b, w_down, layer, nw, mod3_next, seq_len, width, rows):
    final = mod3_next is None
    m = x1.shape[0]
    tokens = TOKENS_PER_STEP
    assert tokens == seq_len or rows == 1
    nk = 2
    tk = D_FF // nk
    bc = mod3.shape[0]
    seq_of = (lambda b: (b * tokens) // seq_len) if bc > 1 else (lambda b: 0)
    row_spec = pl.BlockSpec((tokens, D_MODEL), lambda b, k: (b, 0))
    modspec = lambda part: pl.BlockSpec((None, 1, D_MODEL), lambda b, k, part=part: (seq_of(b), 0, part))
    in_specs = [
        row_spec,
        pl.BlockSpec((tokens, tk), lambda b, k: (b, k)),
        pl.BlockSpec((tokens, tk), lambda b, k: (b, nk + k)),
        modspec(5),
        pl.BlockSpec((9, tk), lambda b, k: (0, k)),
        pl.BlockSpec((9, tk), lambda b, k: (0, nk + k)),
        pl.BlockSpec((1, tk), lambda b, k: (0, k)),
        pl.BlockSpec((1, tk), lambda b, k: (0, nk + k)),
        pl.BlockSpec((None, tk, D_MODEL), lambda b, k: (layer, k, 0)),
        pl.BlockSpec((1, D_MODEL), lambda b, k: (0, 0)),
    ]
    args = [x1, u, u, mod3, conv_w, conv_w, conv_b, conv_b, w_down, nw]
    out_specs = [row_spec]
    out_shape = [jax.ShapeDtypeStruct((m, D_MODEL), F32)]
    if not final:
        in_specs += [modspec(0), modspec(1)]
        args += [mod3_next, mod3_next]
        out_specs.append(row_spec)
        out_shape.append(jax.ShapeDtypeStruct((m, D_MODEL), BF16))
    res = pl.pallas_call(
        functools.partial(_ffndown_kernel, width=width, rows=rows, final=final),
        grid=(m // tokens, nk),
        in_specs=in_specs, out_specs=out_specs, out_shape=out_shape,
        scratch_shapes=[pltpu.VMEM((tokens, D_MODEL), F32), pltpu.VMEM((tokens, tk), BF16)],
        compiler_params=_cparams(("arbitrary", "arbitrary")),
        name="ffndown",
    )(*args)
    return (res[0], None) if final else (res[0], res[1])


def _ssd_param_rows(a_log, dt_bias, d_skip):
    def per_group(v):
        return v.reshape(2, SSD_GROUPS, HEADS_PER_GROUP).transpose(1, 0, 2).reshape(SSD_GROUPS, 2 * HEADS_PER_GROUP)

    def rows(v):
        return jnp.pad(per_group(v), ((0, 0), (0, LANES - 2 * HEADS_PER_GROUP))).reshape(SSD_GROUPS, 1, LANES)
    a_cols = jnp.broadcast_to(per_group(a_log)[:, :, None], (SSD_GROUPS, 2 * HEADS_PER_GROUP, LANES))
    return rows(a_log), a_cols, rows(dt_bias), jnp.repeat(d_skip, SSD_HEADDIM).reshape(1, SSD_W)


def _run_pass(x, mod_all_rows, s_h0, s_s0, layers, hgrn_lb, final_norm_w, consts, width, rows, want_state):
    batch, seq_len, _ = x.shape
    m = batch * seq_len
    x2 = x.reshape(m, D_MODEL)
    tl, tu = consts[0], consts[1]
    prev_h, prev_s = [], []
    new_h, new_s = None, None
    hn = _norm_call(x2, mod_all_rows[0], layers[0]['norm_w1'], seq_len)
    for l, p in enumerate(layers):
        mod3 = mod_all_rows[l]
        last = l == DEPTH - 1
        o_h, new_h = _hgrn_call(hn, p['w_in'], hgrn_lb, p['hgrn_norm_w'], consts,
                                s_h0, prev_h, l, batch, seq_len, want_state)
        o_s, new_s = _ssd_call(hn, p['w_in'], p['ssd'], (tl, tu),
                               s_s0, prev_s, l, batch, seq_len, width, rows, want_state)
        prev_h.append(new_h)
        prev_s.append(new_s)
        x1, u = _outup_call(x2, o_h, o_s, mod3, p['norm_w2'], p['w_out'], p['ffn_up'], l, seq_len)
        x2, hn = _ffndown_call(x1, u, mod3, p['ffn_conv_w'], p['ffn_conv_b'], p['ffn_down'], l,
                               final_norm_w if last else layers[l + 1]['norm_w1'],
                               None if last else mod_all_rows[l + 1], seq_len, width, rows)
    if want_state and DEPTH == 1:
        new_h, new_s = new_h[:, None], new_s[:, None]
    return x2.reshape(batch, seq_len, D_MODEL), new_h, new_s


def kernel(x_prompt, x_sample, c, state_hgrn, state_ssd, c_ctx, norm_w, final_norm_w, w_ada, b_ada,
           w_in, w_out, hgrn_lb, hgrn_norm_w, ssd_conv_w, ssd_conv_b, ssd_a_log, ssd_dt_bias, ssd_d,
           ssd_norm_w, ffn_up, ffn_conv_w, ffn_conv_b, ffn_down):
    dec_batch = c.shape[0]
    consts = _scan_constants() + (_level_signs(),)

    cond8 = jnp.concatenate([c_ctx[None], c, jnp.zeros((8 - 1 - dec_batch, D_MODEL), F32)], axis=0)
    mod_all = _mod_call(cond8, w_ada, b_ada)
    mod_ctx = [mod_all[l, 0:1].reshape(1, 1, 6 * D_MODEL) for l in range(DEPTH)]
    mod_lat = [mod_all[l, 1:1 + dec_batch].reshape(dec_batch, 1, 6 * D_MODEL) for l in range(DEPTH)]

    w_in_t = jnp.swapaxes(w_in, 1, 2)
    w_out_b, ffn_up_b, ffn_down_b = w_out.astype(BF16), ffn_up.astype(BF16), ffn_down.astype(BF16)
    layers = []
    for l in range(DEPTH):
        a_rows, a_cols, b_rows, d_rows = _ssd_param_rows(ssd_a_log[l], ssd_dt_bias[l], ssd_d[l])
        layers.append(dict(
            norm_w1=norm_w[l, 0].reshape(1, D_MODEL), norm_w2=norm_w[l, 1].reshape(1, D_MODEL),
            w_in=w_in_t,
            w_out=w_out_b,
            hgrn_norm_w=hgrn_norm_w[l].reshape(1, HGRN_DV),
            ssd=dict(conv_w=ssd_conv_w[l].reshape(9, CONV_CH), conv_b=ssd_conv_b[l].reshape(1, CONV_CH),
                     a_log_rows=a_rows, a_log_cols=a_cols, dt_bias_rows=b_rows, d_rows=d_rows,
                     norm_w=ssd_norm_w[l].reshape(1, SSD_W)),
            ffn_up=ffn_up_b,
            ffn_conv_w=ffn_conv_w[l].reshape(9, 2 * D_FF), ffn_conv_b=ffn_conv_b[l].reshape(1, 2 * D_FF),
            ffn_down=ffn_down_b,
        ))
    fnw = final_norm_w.reshape(1, D_MODEL)

    y_prompt, new_h, new_s = _run_pass(x_prompt, mod_ctx, None, None, layers, hgrn_lb, fnw, consts,
                                       width=x_prompt.shape[1], rows=1, want_state=True)
    y_sample, _, _ = _run_pass(x_sample, mod_lat, state_hgrn, state_ssd, layers, hgrn_lb, fnw, consts,
                               width=GRID_W, rows=x_sample.shape[1] // GRID_W, want_state=False)
    return (y_prompt, y_sample, new_h, new_s)
```

```python
import functools
import math

import numpy as np
import jax
import jax.numpy as jnp
from jax import lax
from jax.experimental import pallas as pl
from jax.experimental.pallas import tpu as pltpu

F32 = jnp.float32
BF16 = jnp.bfloat16

D_MODEL = 1024
DEPTH = 2
GRID_W = 64
HGRN_HEADS = 8
HGRN_DK = 128
HGRN_DV = 128
HGRN_KW = HGRN_HEADS * HGRN_DK
HGRN_W = HGRN_HEADS * HGRN_DV
SSD_W = 1024
SSD_HEADDIM = 64
SSD_HEADS = 16
SSD_GROUPS = 4
SSD_STATE = 128
HEADS_PER_GROUP = SSD_HEADS // SSD_GROUPS
GROUP_W = SSD_W // SSD_GROUPS
MIX_W = HGRN_W + SSD_W
CONV_CH = SSD_W + 2 * SSD_GROUPS * SSD_STATE
D_FF = 2816
MAIN_COLS = 3 * HGRN_KW + 2 * HGRN_W + SSD_W + CONV_CH
DT_COLS = 2 * SSD_HEADS
EPS = 1e-6
LOG2E = math.log2(math.e)

LANES = 128
CHUNK = 128
N_LEVELS = 7
TOKENS_PER_STEP = 1024
VMEM_LIMIT = 56 * 1024 * 1024


def _cparams(sem):
    return pltpu.CompilerParams(dimension_semantics=sem, vmem_limit_bytes=VMEM_LIMIT)


def _sigmoid(x):
    return 1.0 / (1.0 + jnp.exp(-x))


def _silu(x):
    return x * _sigmoid(x)


def _softplus(x):
    return jnp.maximum(x, 0.0) + jnp.log1p(jnp.exp(-jnp.abs(x)))


def _rms(x):
    return x * lax.rsqrt(jnp.mean(x * x, axis=-1, keepdims=True) + EPS)


def _split3(x):
    hi = x.astype(BF16)
    r = x - hi.astype(F32)
    mid = r.astype(BF16)
    lo = (r - mid.astype(F32)).astype(BF16)
    return hi, mid, lo


def _sum_rows_exact(w01, x):
    n = x.shape[1]
    hi, mid, lo = _split3(x)
    p = jnp.dot(w01, jnp.concatenate([hi, mid, lo], axis=1), preferred_element_type=F32)
    return p[:, :n] + p[:, n:2 * n] + p[:, 2 * n:]


def _sum_cols_exact(x, w01):
    m = x.shape[0]
    hi, mid, lo = _split3(x)
    p = jnp.dot(jnp.concatenate([hi, mid, lo], axis=0), w01, preferred_element_type=F32)
    return p[:m] + p[m:2 * m] + p[2 * m:]


def _dot_nt(a, b):
    return lax.dot_general(a, b, (((1,), (1,)), ((), ())), preferred_element_type=F32)


def _dot_tn(a, b):
    return lax.dot_general(a, b, (((0,), (0,)), ((), ())), preferred_element_type=F32)


def _dwconv(x, w_ref, b_ref, width, rows):
    seq, _ = x.shape
    t = lax.broadcasted_iota(jnp.int32, x.shape, 0)
    col = jnp.bitwise_and(t, width - 1)
    xl = jnp.where(col == 0, 0.0, pltpu.roll(x, 1, axis=0))
    xr = jnp.where(col == width - 1, 0.0, pltpu.roll(x, seq - 1, axis=0))

    def hrow(kh, a, b, c):
        return w_ref[3 * kh:3 * kh + 1, :] * a + w_ref[3 * kh + 1:3 * kh + 2, :] * b \
            + w_ref[3 * kh + 2:3 * kh + 3, :] * c

    out = b_ref[0:1, :] + hrow(1, xl, x, xr)
    if rows > 1:
        inner = seq - width
        up = hrow(0, xl[:inner], x[:inner], xr[:inner])
        out = jnp.concatenate([out[:width], out[width:] + up], axis=0)
        dn = hrow(2, xl[width:], x[width:], xr[width:])
        out = jnp.concatenate([out[:inner] + dn, out[inner:]], axis=0)
    return out


def _scan_constants():
    c = CHUNK
    t = np.arange(c)[:, None]
    i = np.arange(c)[None, :]
    x = t ^ i
    lv = np.where(x == 0, 0, np.floor(np.log2(np.maximum(x, 1))).astype(np.int64) + 1)
    lv = np.where(i > t, -lv, lv)
    return jnp.asarray(i <= t, BF16), jnp.asarray(i >= t, BF16), jnp.asarray(lv, jnp.int32)


def _level_signs():
    t = np.arange(CHUNK)[:, None]
    sg = [np.where(((t >> lev) & 1) == 1, 1.0, -1.0) * np.ones((1, LANES)) for lev in range(N_LEVELS)]
    return jnp.asarray(np.concatenate(sg, 0), F32)


def _mod_kernel(cond_ref, w_ref, b_ref, o_ref):
    a = _silu(cond_ref[...]).astype(BF16)
    o_ref[...] = jnp.dot(a, w_ref[...].astype(BF16), preferred_element_type=F32) + b_ref[...]


def _mod_call(cond8, w_ada, b_ada):
    n = 6 * D_MODEL
    tn = D_MODEL
    return pl.pallas_call(
        _mod_kernel,
        grid=(DEPTH, n // tn),
        in_specs=[
            pl.BlockSpec((8, D_MODEL), lambda l, j: (0, 0)),
            pl.BlockSpec((None, D_MODEL, tn), lambda l, j: (l, 0, j)),
            pl.BlockSpec((None, 1, tn), lambda l, j: (l, 0, j)),
        ],
        out_specs=pl.BlockSpec((None, 8, tn), lambda l, j: (l, 0, j)),
        out_shape=jax.ShapeDtypeStruct((DEPTH, 8, n), F32),
        compiler_params=_cparams(("arbitrary", "arbitrary")),
        name="mod",
    )(cond8, w_ada, b_ada.reshape(DEPTH, 1, n))


def _norm_kernel(x_ref, sh_ref, sc_ref, nw_ref, hn_ref):
    h = _rms(x_ref[...]) * nw_ref[...] * (1.0 + sc_ref[...]) + sh_ref[...]
    hn_ref[...] = h.astype(BF16)


def _norm_call(x2, mod3, nw, seq_len):
    m = x2.shape[0]
    tm = 512
    bc = mod3.shape[0]
    seq_of = (lambda i: (i * tm) // seq_len) if bc > 1 else (lambda i: 0)
    return pl.pallas_call(
        _norm_kernel,
        grid=(m // tm,),
        in_specs=[
            pl.BlockSpec((tm, D_MODEL), lambda i: (i, 0)),
            pl.BlockSpec((None, 1, D_MODEL), lambda i: (seq_of(i), 0, 0)),
            pl.BlockSpec((None, 1, D_MODEL), lambda i: (seq_of(i), 0, 1)),
            pl.BlockSpec((1, D_MODEL), lambda i: (0, 0)),
        ],
        out_specs=pl.BlockSpec((tm, D_MODEL), lambda i: (i, 0)),
        out_shape=jax.ShapeDtypeStruct((m, D_MODEL), BF16),
        compiler_params=_cparams(("arbitrary",)),
        name="norm",
    )(x2, mod3, mod3, nw)


def _hgrn_gates(f_pre, lbd):
    ea = jnp.exp(-jnp.abs(f_pre))
    log2_sig = jnp.minimum(f_pre, 0.0) * LOG2E - jnp.log2(1.0 + ea)
    sig_neg = jnp.where(f_pre >= 0.0, ea, 1.0) / (1.0 + ea)
    if lbd is None:
        return log2_sig, sig_neg
    a = jnp.log2(lbd)
    b = jnp.log2(1.0 - lbd) + log2_sig
    log2_f = jnp.maximum(a, b) + jnp.log2(1.0 + jnp.exp2(-jnp.abs(a - b)))
    return log2_f, (1.0 - lbd) * sig_neg


def _block_mid(scr, m):
    return jnp.concatenate(
        [jnp.broadcast_to(scr[r0 + m - 1:r0 + m, :], (2 * m, scr.shape[1])) for r0 in range(0, CHUNK, 2 * m)],
        axis=0)


def _mid_distance(c, scr, m):
    pieces = []
    for r0 in range(0, CHUNK, 2 * m):
        mid = jnp.broadcast_to(scr[r0 + m - 1:r0 + m, :], (m, scr.shape[1]))
        pieces += [mid - c[r0:r0 + m], c[r0 + m:r0 + 2 * m] - mid]
    return jnp.concatenate(pieces, axis=0)


def _pick_halves(up_val, low_val, m):
    pieces = []
    for r0 in range(0, CHUNK, 2 * m):
        pieces += [low_val[r0:r0 + m], up_val[r0 + m:r0 + 2 * m]]
    return jnp.concatenate(pieces, axis=0)


def _hgrn_kernel(*refs, layer, seq_len, seqs, has_s0, n_prev, want_state):
    it = iter(refs)
    hn_ref = next(it)
    w_refs = [next(it) for _ in range(5)]
    lb_ref, nw_ref, tl_ref, lv_ref, sg_ref = (next(it) for _ in range(5))
    s0_ref = next(it) if has_s0 else None
    prev_refs = [next(it) for _ in range(n_prev)]
    o_ref = next(it)
    sf_ref = next(it) if want_state else None
    if n_prev:
        for i, pref in enumerate(prev_refs):
            sf_ref[:, i] = pref[...]
        sf_own = sf_ref.at[:, layer]
    else:
        sf_own = sf_ref
    (w_scr, p_scr, acc_scr, qc_scr, u_scr, dl_scr, st_scr, cumf_scr, cumb_scr) = (next(it) for _ in range(9))

    c = CHUNK
    dk = HGRN_DK
    n_chunks = seqs * seq_len // c
    cps = seq_len // c

    @pl.when(pl.program_id(1) == 0)
    def _():
        for i, w_ref in enumerate(w_refs):
            w_scr[i * LANES:(i + 1) * LANES, :] = w_ref[...].astype(BF16)

    p_scr[...] = _dot_nt(hn_ref[...], w_scr[...])

    lb = None
    if layer > 0:
        lbr = lb_ref[...]
        e = jnp.exp(lbr - jnp.max(lbr, axis=0, keepdims=True))
        sm = e / jnp.sum(e, axis=0, keepdims=True)
        lb = sm[1]
        for i in range(2, layer + 1):
            lb = lb + sm[i]

    lvs = lv_ref[...]
    lv = jnp.abs(lvs)
    tl = tl_ref[...]
    zero = jnp.zeros((c, dk), F32)
    odd = jnp.bitwise_and(lax.broadcasted_iota(jnp.int32, (c, dk), 0), 1) == 1

    def intra(ci, carry):
        r0 = pl.multiple_of(ci * c, c)
        sl = pl.ds(r0, c)
        q = _silu(p_scr[sl, 0:dk]) * (dk ** -0.5)
        lf_f, kf = _hgrn_gates(p_scr[sl, dk:2 * dk], None if lb is None else lb[0:1, :])
        lf_b, kb = _hgrn_gates(p_scr[sl, 2 * dk:3 * dk], None if lb is None else lb[1:2, :])
        vb = p_scr[sl, 3 * dk:4 * dk].astype(BF16)

        cum_f = _sum_rows_exact(tl, lf_f)
        cum_b = _sum_rows_exact(tl, lf_b)
        cx_b = cum_b - lf_b
        cumf_scr[...] = cum_f
        cumb_scr[...] = cum_b
        tot_f = cum_f[c - 1:c, :]
        tot_b = cum_b[c - 1:c, :]

        a_sum = _dot_nt(q.astype(BF16), (kf + kb).astype(BF16))

        f_f = jnp.exp2(lf_f)
        f_b = jnp.exp2(lf_b)
        qf = q * f_f
        qb = q * f_b
        p_f = _dot_nt(qf.astype(BF16), kf.astype(BF16))
        p_b = _dot_nt(qb.astype(BF16), kb.astype(BF16))
        a_sum = jnp.where(lvs == 1, p_f, jnp.where(lvs == -1, p_b, a_sum))
        p_f = _dot_nt(jnp.where(odd, qf * pltpu.roll(f_f, 1, axis=0), qf).astype(BF16),
                      jnp.where(odd, kf, kf * pltpu.roll(f_f, c - 1, axis=0)).astype(BF16))
        p_b = _dot_nt(jnp.where(odd, qb, qb * pltpu.roll(f_b, c - 1, axis=0)).astype(BF16),
                      jnp.where(odd, kb * pltpu.roll(f_b, 1, axis=0), kb).astype(BF16))
        a_sum = jnp.where(lvs == 2, p_f, jnp.where(lvs == -2, p_b, a_sum))

        sgn = sg_ref[2 * c:3 * c, :]
        ef = jnp.exp2((cum_f - _block_mid(cumf_scr, 4)) * sgn)
        eb = jnp.exp2((cx_b - _block_mid(cumb_scr, 4)) * sgn)
        p_f = _dot_nt((q * ef).astype(BF16), (kf * ef).astype(BF16))
        p_b = _dot_nt((q * eb).astype(BF16), (kb * eb).astype(BF16))
        a_sum = jnp.where(lvs == 3, p_f, jnp.where(lvs == -3, p_b, a_sum))

        for lev in range(4, N_LEVELS + 1):
            m = 1 << (lev - 1)
            ef = jnp.exp2(_mid_distance(cum_f, cumf_scr, m))
            eb = jnp.exp2(_mid_distance(cx_b, cumb_scr, m))
            qe = q * _pick_halves(ef, eb, m)
            ke = _pick_halves(kb, kf, m) * _pick_halves(eb, ef, m)
            lhs = jnp.concatenate([_pick_halves(qe, zero, m), _pick_halves(zero, qe, m)], axis=1)
            rhs = jnp.concatenate([_pick_halves(zero, ke, m), _pick_halves(ke, zero, m)], axis=1)
            a_sum = jnp.where(lv == lev, _dot_nt(lhs.astype(BF16), rhs.astype(BF16)), a_sum)
        acc_scr[sl, :] = jnp.dot(a_sum.astype(BF16), vb, preferred_element_type=F32)

        qc_scr[sl, :] = jnp.concatenate(
            [q * jnp.exp2(cum_f), q * jnp.exp2(jnp.minimum(tot_b - cx_b, 0.0))], axis=1).astype(BF16)
        u_scr[ci, 0] = _dot_tn(vb, (kf * jnp.exp2(jnp.minimum(tot_f - cum_f, 0.0))).astype(BF16))
        u_scr[ci, 1] = _dot_tn(vb, (kb * jnp.exp2(jnp.minimum(cx_b, 0.0))).astype(BF16))
        dl_scr[ci, 0:1, :] = jnp.exp2(tot_f)
        dl_scr[ci, 1:2, :] = jnp.exp2(tot_b)
        return carry

    lax.fori_loop(0, n_chunks, intra, 0, unroll=True)

    def init(s, d):
        return s0_ref[s, d].T if has_s0 else jnp.zeros((HGRN_DV, dk), F32)

    st = None
    for ci in range(n_chunks):
        s = ci // cps
        if ci % cps == 0:
            st = init(s, 0)
        st_scr[ci, :, 0:dk] = st.astype(BF16)
        st = st * dl_scr[ci, 0:1, :] + u_scr[ci, 0]
        if want_state and ci % cps == cps - 1:
            sf_own[s, 0] = st.T
    for ci in reversed(range(n_chunks)):
        s = ci // cps
        if ci % cps == cps - 1:
            st = init(s, 1)
        st_scr[ci, :, dk:2 * dk] = st.astype(BF16)
        st = st * dl_scr[ci, 1:2, :] + u_scr[ci, 1]
        if want_state and ci % cps == 0:
            sf_own[s, 1] = st.T

    nw = nw_ref[...]

    for ci in range(n_chunks):
        sl = slice(ci * c, (ci + 1) * c)
        o = acc_scr[sl, :] + _dot_nt(qc_scr[sl, :], st_scr[ci])
        o_ref[sl, :] = (_rms(o) * nw * _silu(p_scr[sl, 4 * dk:5 * dk])).astype(o_ref.dtype)


def _hgrn_call(hn, w_in_t, lb_raw, nw, consts, s0, prev_states, layer, batch, seq_len, want_state):
    tl, _, lv, sg = consts
    has_s0 = s0 is not None
    m = hn.shape[0]
    tokens = TOKENS_PER_STEP
    seqs = tokens // seq_len
    n_chunks = tokens // CHUNK
    full = lambda arr: pl.BlockSpec(arr.shape, lambda h, b: (0,) * arr.ndim)
    state_spec = pl.BlockSpec((seqs, None, 2, None, HGRN_DK, HGRN_DV), lambda h, b: (b, layer, 0, h, 0, 0))
    wcol = lambda part: pl.BlockSpec((None, LANES, D_MODEL),
                                     lambda h, b, part=part: (layer, part * HGRN_HEADS + h, 0))
    in_specs = [pl.BlockSpec((tokens, D_MODEL), lambda h, b: (b, 0)),
                wcol(0), wcol(1), wcol(2), wcol(3), wcol(4),
                pl.BlockSpec((DEPTH, 2, LANES), lambda h, b: (0, 0, h)),
                pl.BlockSpec((1, HGRN_DV), lambda h, b: (0, 0)),
                full(tl), full(lv), full(sg)]
    args = [hn, w_in_t, w_in_t, w_in_t, w_in_t, w_in_t, lb_raw, nw, tl, lv, sg]
    if has_s0:
        in_specs.append(state_spec)
        args.append(s0)
    out_specs = [pl.BlockSpec((tokens, LANES), lambda h, b: (b, h))]
    out_shape = [jax.ShapeDtypeStruct((m, HGRN_W), BF16)]
    n_prev = 0
    if want_state:
        own_spec = pl.BlockSpec((seqs, 2, None, HGRN_DK, HGRN_DV), lambda h, b: (b, 0, h, 0, 0))
        own_shape = (batch, 2, HGRN_HEADS, HGRN_DK, HGRN_DV)
        if layer == DEPTH - 1 and prev_states:
            n_prev = len(prev_states)
            in_specs += [own_spec] * n_prev
            args += list(prev_states)
            out_specs.append(pl.BlockSpec((seqs, DEPTH, 2, None, HGRN_DK, HGRN_DV),
                                          lambda h, b: (b, 0, 0, h, 0, 0)))
            out_shape.append(jax.ShapeDtypeStruct((batch, DEPTH) + own_shape[1:], F32))
        else:
            out_specs.append(own_spec)
            out_shape.append(jax.ShapeDtypeStruct(own_shape, F32))
    scratch = [pltpu.VMEM((5 * LANES, D_MODEL), BF16),
               pltpu.VMEM((tokens, 5 * LANES), F32),
               pltpu.VMEM((tokens, HGRN_DV), F32),
               pltpu.VMEM((tokens, 2 * HGRN_DK), BF16),
               pltpu.VMEM((n_chunks, 2, HGRN_DV, HGRN_DK), F32),
               pltpu.VMEM((n_chunks, 8, HGRN_DK), F32),
               pltpu.VMEM((n_chunks, HGRN_DV, 2 * HGRN_DK), BF16),
               pltpu.VMEM((CHUNK, HGRN_DK), F32),
               pltpu.VMEM((CHUNK, HGRN_DK), F32)]
    res = pl.pallas_call(
        functools.partial(_hgrn_kernel, layer=layer, seq_len=seq_len, seqs=seqs, has_s0=has_s0,
                          n_prev=n_prev, want_state=want_state),
        grid=(HGRN_HEADS, m // tokens),
        in_specs=in_specs, out_specs=out_specs, out_shape=out_shape,
        scratch_shapes=scratch,
        compiler_params=_cparams(("arbitrary", "arbitrary")),
        name="hgrn",
    )(*args)
    return (res[0], res[1]) if want_state else (res[0], None)


def _ssd_kernel(*refs, layer, seq_len, seqs, width, rows, has_s0, n_prev, want_state):
    it = iter(refs)
    (hn_ref, wz_ref, wx_ref, wb_ref, wc_ref, wdt_ref, cwx_ref, cwb_ref, cwc_ref, cbx_ref, cbb_ref, cbc_ref,
     alog_ref, acol_ref, dtb_ref, dsk_ref, nw_ref, tl_ref, tu_ref) = (next(it) for _ in range(19))
    s0_ref = next(it) if has_s0 else None
    prev_refs = [next(it) for _ in range(n_prev)]
    o_ref = next(it)
    sf_ref = next(it) if want_state else None
    if n_prev:
        for i, pref in enumerate(prev_refs):
            sf_ref[:, i] = pref[...]
        sf_own = sf_ref.at[:, layer]
    else:
        sf_own = sf_ref
    (w_scr, p_scr, xc_scr, bc_scr, cc_scr, dtg_scr, acc_scr, ecum_scr, u_scr, dl_scr, st_scr) = (
        next(it) for _ in range(11))

    q = CHUNK
    n_chunks = seqs * seq_len // q
    cps = seq_len // q
    hp = HEADS_PER_GROUP
    p = SSD_HEADDIM
    gw = GROUP_W
    n = SSD_STATE
    dt0 = 2 * gw + 2 * n

    @pl.when(pl.program_id(1) == 0)
    def _():
        w_scr[0:gw, :] = wz_ref[...].astype(BF16)
        w_scr[gw:2 * gw, :] = wx_ref[...].astype(BF16)
        w_scr[2 * gw:2 * gw + n, :] = wb_ref[...].astype(BF16)
        w_scr[2 * gw + n:2 * gw + 2 * n, :] = wc_ref[...].astype(BF16)
        w_scr[dt0:dt0 + DT_COLS, :] = wdt_ref[...].astype(BF16)
        w_scr[dt0 + DT_COLS:dt0 + LANES, :] = jnp.zeros((LANES - DT_COLS, D_MODEL), BF16)

    hn = hn_ref[...]
    p_scr[...] = _dot_nt(hn, w_scr[0:gw, :])
    px = _dot_nt(hn, w_scr[gw:2 * gw, :])
    for s in range(gw // LANES):
        cs = slice(s * LANES, (s + 1) * LANES)
        xc_scr[:, cs] = _silu(_dwconv(px[:, cs], cwx_ref[:, cs], cbx_ref[:, cs], width, rows))
    pbc = _dot_nt(hn, w_scr[2 * gw:2 * gw + 2 * n, :])
    bc_scr[...] = _silu(_dwconv(pbc[:, 0:n], cwb_ref, cbb_ref, width, rows))
    cc_scr[...] = _silu(_dwconv(pbc[:, n:2 * n], cwc_ref, cbc_ref, width, rows)).astype(BF16)
    g = pl.program_id(0)
    dt_all = _dot_nt(hn, w_scr[dt0:dt0 + LANES, :])
    lane = lax.broadcasted_iota(jnp.int32, dt_all.shape, 1)
    dt_fw = pltpu.roll(dt_all, jnp.bitwise_and(LANES - hp * g, LANES - 1), axis=1)
    dt_bw = pltpu.roll(dt_all, jnp.bitwise_and(LANES - (SSD_HEADS - hp) - hp * g, LANES - 1), axis=1)
    dtg_scr[...] = _softplus(jnp.where(lane < hp, dt_fw, dt_bw) + dtb_ref[...])
    neg_a = -jnp.exp(alog_ref[...])
    neg_a_t = -jnp.exp(acol_ref[...])

    tl = tl_ref[...]
    tu = tu_ref[...]
    ti = lax.broadcasted_iota(jnp.int32, (q, q), 0)
    si = lax.broadcasted_iota(jnp.int32, (q, q), 1)
    lower = si <= ti
    upper = si >= ti

    def intra(ci, carry):
        r0 = pl.multiple_of(ci * q, q)
        sl = pl.ds(r0, q)
        dt = dtg_scr[sl, :]
        da = dt * neg_a
        cum_f = _sum_rows_exact(tl, da)
        cum_b = _sum_rows_exact(tu, da)
        dt_t = dt.T[0:2 * hp, :]
        da_t = dt_t * neg_a_t
        row_f = _sum_cols_exact(da_t, tu)
        row_b = _sum_cols_exact(da_t, tl)
        wr_f = jnp.exp(row_f[:, q - 1:q] - row_f) * dt_t
        wr_b = jnp.exp(row_b[:, 0:1] - row_b) * dt_t
        xb = xc_scr[sl, :].astype(BF16)
        bm = bc_scr[sl, :]
        bt = bm.T
        gmat = _dot_nt(cc_scr[sl, :], bm.astype(BF16))
        ecum_scr[sl, 0:LANES] = jnp.exp(cum_f)
        ecum_scr[sl, LANES:2 * LANES] = jnp.exp(cum_b)
        ys, u_f, u_b = [], [], []
        for j in range(hp):
            jf, jb = j, hp + j
            dec_f = jnp.exp(jnp.minimum(cum_f[:, jf:jf + 1] - row_f[jf:jf + 1, :], 0.0)) * dt_t[jf:jf + 1, :]
            dec_b = jnp.exp(jnp.minimum(cum_b[:, jb:jb + 1] - row_b[jb:jb + 1, :], 0.0)) * dt_t[jb:jb + 1, :]
            mh = gmat * (jnp.where(lower, dec_f, 0.0) + jnp.where(upper, dec_b, 0.0))
            xh = xb[:, j * p:(j + 1) * p]
            ys.append(jnp.dot(mh.astype(BF16), xh, preferred_element_type=F32))
            u_f.append(jnp.dot((bt * wr_f[jf:jf + 1, :]).astype(BF16), xh, preferred_element_type=F32))
            u_b.append(jnp.dot((bt * wr_b[jb:jb + 1, :]).astype(BF16), xh, preferred_element_type=F32))
        acc_scr[sl, :] = jnp.concatenate(ys, axis=1)
        u_scr[ci, 0] = jnp.concatenate(u_f, axis=1)
        u_scr[ci, 1] = jnp.concatenate(u_b, axis=1)
        dl_scr[ci, 0:1, :] = jnp.exp(cum_f[q - 1:q, :])
        dl_scr[ci, 1:2, :] = jnp.exp(cum_b[0:1, :])
        return carry

    lax.fori_loop(0, n_chunks, intra, 0, unroll=4)

    def head_scale(vec, lane0):
        return jnp.concatenate(
            [jnp.broadcast_to(vec[0:1, lane0 + j:lane0 + j + 1], (1, p)) for j in range(hp)], axis=1)

    def init(s, d):
        return s0_ref[s, d].reshape(hp * p, n).T if has_s0 else jnp.zeros((n, hp * p), F32)

    st = None
    for ci in range(n_chunks):
        s = ci // cps
        if ci % cps == 0:
            st = init(s, 0)
        st_scr[ci, :, 0:hp * p] = st.astype(BF16)
        st = st * head_scale(dl_scr[ci, 0:1, :], 0) + u_scr[ci, 0]
        if want_state and ci % cps == cps - 1:
            sf_own[s, 0] = st.T.reshape(hp, p, n)
    for ci in reversed(range(n_chunks)):
        s = ci // cps
        if ci % cps == cps - 1:
            st = init(s, 1)
        st_scr[ci, :, hp * p:2 * hp * p] = st.astype(BF16)
        st = st * head_scale(dl_scr[ci, 1:2, :], hp) + u_scr[ci, 1]
        if want_state and ci % cps == 0:
            sf_own[s, 1] = st.T.reshape(hp, p, n)

    dsk = dsk_ref[...]
    nw = nw_ref[...]
    for ci in range(n_chunks):
        sl = slice(ci * q, (ci + 1) * q)
        yi = jnp.dot(cc_scr[sl, :], st_scr[ci], preferred_element_type=F32)
        ec = ecum_scr[sl, :]
        parts = []
        for j in range(hp):
            parts.append(yi[:, j * p:(j + 1) * p] * ec[:, j:j + 1]
                         + yi[:, (hp + j) * p:(hp + j + 1) * p] * ec[:, LANES + hp + j:LANES + hp + j + 1])
        y = acc_scr[sl, :] + jnp.concatenate(parts, axis=1) + dsk * xc_scr[sl, :]
        y = y * _silu(p_scr[sl, 0:gw])
        o_ref[sl, :] = (_rms(y) * nw).astype(o_ref.dtype)


def _ssd_call(hn, w_in_t, p, consts, s0, prev_states, layer, batch, seq_len, width, rows, want_state):
    tl, tu = consts
    has_s0 = s0 is not None
    m = hn.shape[0]
    tokens = TOKENS_PER_STEP
    seqs = tokens // seq_len
    assert seqs == 1 or rows == 1
    n_chunks = tokens // CHUNK
    wcols = 2 * GROUP_W + 2 * SSD_STATE + LANES
    xb = SSD_W // LANES
    z0 = 5 * HGRN_KW // GROUP_W
    x0 = z0 + SSD_W // GROUP_W
    b0 = (5 * HGRN_KW + 2 * SSD_W) // LANES
    c0 = b0 + SSD_GROUPS
    d0 = MAIN_COLS // DT_COLS
    full = lambda arr: pl.BlockSpec(arr.shape, lambda g, b: (0,) * arr.ndim)
    in_specs = [
        pl.BlockSpec((tokens, D_MODEL), lambda g, b: (b, 0)),
        pl.BlockSpec((None, GROUP_W, D_MODEL), lambda g, b: (layer, z0 + g, 0)),
        pl.BlockSpec((None, GROUP_W, D_MODEL), lambda g, b: (layer, x0 + g, 0)),
        pl.BlockSpec((None, LANES, D_MODEL), lambda g, b: (layer, b0 + g, 0)),
        pl.BlockSpec((None, LANES, D_MODEL), lambda g, b: (layer, c0 + g, 0)),
        pl.BlockSpec((None, DT_COLS, D_MODEL), lambda g, b: (layer, d0, 0)),
        pl.BlockSpec((9, GROUP_W), lambda g, b: (0, g)),
        pl.BlockSpec((9, LANES), lambda g, b: (0, xb + g)),
        pl.BlockSpec((9, LANES), lambda g, b: (0, xb + SSD_GROUPS + g)),
        pl.BlockSpec((1, GROUP_W), lambda g, b: (0, g)),
        pl.BlockSpec((1, LANES), lambda g, b: (0, xb + g)),
        pl.BlockSpec((1, LANES), lambda g, b: (0, xb + SSD_GROUPS + g)),
        pl.BlockSpec((None, 1, LANES), lambda g, b: (g, 0, 0)),
        pl.BlockSpec((None, 2 * HEADS_PER_GROUP, LANES), lambda g, b: (g, 0, 0)),
        pl.BlockSpec((None, 1, LANES), lambda g, b: (g, 0, 0)),
        pl.BlockSpec((1, GROUP_W), lambda g, b: (0, g)),
        pl.BlockSpec((1, GROUP_W), lambda g, b: (0, g)),
        full(tl), full(tu),
    ]
    args = [hn, w_in_t, w_in_t, w_in_t, w_in_t, w_in_t, p['conv_w'], p['conv_w'], p['conv_w'],
            p['conv_b'], p['conv_b'], p['conv_b'],
            p['a_log_rows'], p['a_log_cols'], p['dt_bias_rows'], p['d_rows'], p['norm_w'], tl, tu]
    state_spec = pl.BlockSpec((seqs, None, 2, HEADS_PER_GROUP, SSD_HEADDIM, SSD_STATE),
                              lambda g, b: (b, layer, 0, g, 0, 0))
    if has_s0:
        in_specs.append(state_spec)
        args.append(s0)
    out_specs = [pl.BlockSpec((tokens, GROUP_W), lambda g, b: (b, g))]
    out_shape = [jax.ShapeDtypeStruct((m, SSD_W), BF16)]
    n_prev = 0
    if want_state:
        own_spec = pl.BlockSpec((seqs, 2, HEADS_PER_GROUP, SSD_HEADDIM, SSD_STATE), lambda g, b: (b, 0, g, 0, 0))
        own_shape = (batch, 2, SSD_HEADS, SSD_HEADDIM, SSD_STATE)
        if layer == DEPTH - 1 and prev_states:
            n_prev = len(prev_states)
            in_specs += [own_spec] * n_prev
            args += list(prev_states)
            out_specs.append(pl.BlockSpec((seqs, DEPTH, 2, HEADS_PER_GROUP, SSD_HEADDIM, SSD_STATE),
                                          lambda g, b: (b, 0, 0, g, 0, 0)))
            out_shape.append(jax.ShapeDtypeStruct((batch, DEPTH) + own_shape[1:], F32))
        else:
            out_specs.append(own_spec)
            out_shape.append(jax.ShapeDtypeStruct(own_shape, F32))
    hpp = HEADS_PER_GROUP * SSD_HEADDIM
    scratch = [pltpu.VMEM((wcols, D_MODEL), BF16),
               pltpu.VMEM((tokens, GROUP_W), F32),
               pltpu.VMEM((tokens, GROUP_W), F32),
               pltpu.VMEM((tokens, SSD_STATE), F32),
               pltpu.VMEM((tokens, SSD_STATE), BF16),
               pltpu.VMEM((tokens, LANES), F32),
               pltpu.VMEM((tokens, GROUP_W), F32),
               pltpu.VMEM((tokens, 2 * LANES), F32),
               pltpu.VMEM((n_chunks, 2, SSD_STATE, hpp), F32),
               pltpu.VMEM((n_chunks, 8, LANES), F32),
               pltpu.VMEM((n_chunks, SSD_STATE, 2 * hpp), BF16)]
    res = pl.pallas_call(
        functools.partial(_ssd_kernel, layer=layer, seq_len=seq_len, seqs=seqs, width=width, rows=rows,
                          has_s0=has_s0, n_prev=n_prev, want_state=want_state),
        grid=(SSD_GROUPS, m // tokens),
        in_specs=in_specs, out_specs=out_specs, out_shape=out_shape,
        scratch_shapes=scratch,
        compiler_params=_cparams(("arbitrary", "arbitrary")),
        name="ssd",
    )(*args)
    return (res[0], res[1]) if want_state else (res[0], None)


def _outup_kernel(x_ref, oh_ref, os_ref, g1_ref, sh_ref, sc_ref, nw_ref, wo_ref, wu_ref,
                  x1_ref, u_ref, h_scr):
    @pl.when(pl.program_id(1) == 0)
    def _():
        mix = jnp.dot(oh_ref[...], wo_ref[0:HGRN_W, :], preferred_element_type=F32)
        mix = mix + jnp.dot(os_ref[...], wo_ref[HGRN_W:MIX_W, :], preferred_element_type=F32)
        x1 = x_ref[...] + g1_ref[...] * mix
        x1_ref[...] = x1
        h = _rms(x1) * nw_ref[...] * (1.0 + sc_ref[...]) + sh_ref[...]
        h_scr[...] = h.astype(BF16)

    u_ref[...] = jnp.dot(h_scr[...], wu_ref[...], preferred_element_type=F32).astype(u_ref.dtype)


def _outup_call(x2, o_h, o_s, mod3, nw, w_out, w_up, layer, seq_len):
    m = x2.shape[0]
    tm, tn = 1024, 512
    bc = mod3.shape[0]
    seq_of = (lambda i: (i * tm) // seq_len) if bc > 1 else (lambda i: 0)
    modspec = lambda part: pl.BlockSpec((None, 1, D_MODEL), lambda i, j, part=part: (seq_of(i), 0, part))
    return pl.pallas_call(
        _outup_kernel,
        grid=(m // tm, 2 * D_FF // tn),
        in_specs=[
            pl.BlockSpec((tm, D_MODEL), lambda i, j: (i, 0)),
            pl.BlockSpec((tm, HGRN_W), lambda i, j: (i, 0)),
            pl.BlockSpec((tm, SSD_W), lambda i, j: (i, 0)),
            modspec(2), modspec(3), modspec(4),
            pl.BlockSpec((1, D_MODEL), lambda i, j: (0, 0)),
            pl.BlockSpec((None, MIX_W, D_MODEL), lambda i, j: (layer, 0, 0)),
            pl.BlockSpec((None, D_MODEL, tn), lambda i, j: (layer, 0, j)),
        ],
        out_specs=[
            pl.BlockSpec((tm, D_MODEL), lambda i, j: (i, 0)),
            pl.BlockSpec((tm, tn), lambda i, j: (i, j)),
        ],
        out_shape=[jax.ShapeDtypeStruct((m, D_MODEL), F32),
                   jax.ShapeDtypeStruct((m, 2 * D_FF), BF16)],
        scratch_shapes=[pltpu.VMEM((tm, D_MODEL), BF16)],
        compiler_params=_cparams(("arbitrary", "arbitrary")),
        name="outup",
    )(x2, o_h, o_s, mod3, mod3, mod3, nw, w_out, w_up)


def _ffndown_kernel(*refs, width, rows, final):
    it = iter(refs)
    x1_ref, ug_ref, uv_ref, g2_ref, cwg_ref, cwv_ref, cbg_ref, cbv_ref, wd_ref, nw_ref = (
        next(it) for _ in range(10))
    sh_ref, sc_ref = (None, None) if final else (next(it), next(it))
    o_ref = next(it)
    hn_ref = None if final else next(it)
    acc_scr, act_scr = next(it), next(it)
    k = pl.program_id(1)

    for s in range(ug_ref.shape[1] // LANES):
        cs = slice(s * LANES, (s + 1) * LANES)
        gate = _dwconv(ug_ref[:, cs].astype(F32), cwg_ref[:, cs], cbg_ref[:, cs], width, rows)
        val = _dwconv(uv_ref[:, cs].astype(F32), cwv_ref[:, cs], cbv_ref[:, cs], width, rows)
        act_scr[:, cs] = (_silu(gate) * val).astype(BF16)
    part = jnp.dot(act_scr[...], wd_ref[...], preferred_element_type=F32)

    @pl.when(k == 0)
    def _():
        acc_scr[...] = part

    @pl.when(k > 0)
    def _():
        acc_scr[...] += part

    @pl.when(k == pl.num_programs(1) - 1)
    def _():
        x2 = x1_ref[...] + g2_ref[...] * acc_scr[...]
        if final:
            o_ref[...] = _rms(x2) * nw_ref[...]
        else:
            o_ref[...] = x2
            hn_ref[...] = (_rms(x2) * nw_ref[...] * (1.0 + sc_ref[...]) + sh_ref[...]).astype(BF16)


def _ffndown_call(x1, u, mod3, conv_w, conv_b, w_down, layer, nw, mod3_next, seq_len, width, rows):
    final = mod3_next is None
    m = x1.shape[0]
    tokens = TOKENS_PER_STEP
    assert tokens == seq_len or rows == 1
    nk = 2
    tk = D_FF // nk
    bc = mod3.shape[0]
    seq_of = (lambda b: (b * tokens) // seq_len) if bc > 1 else (lambda b: 0)
    row_spec = pl.BlockSpec((tokens, D_MODEL), lambda b, k: (b, 0))
    modspec = lambda part: pl.BlockSpec((None, 1, D_MODEL), lambda b, k, part=part: (seq_of(b), 0, part))
    in_specs = [
        row_spec,
        pl.BlockSpec((tokens, tk), lambda b, k: (b, k)),
        pl.BlockSpec((tokens, tk), lambda b, k: (b, nk + k)),
        modspec(5),
        pl.BlockSpec((9, tk), lambda b, k: (0, k)),
        pl.BlockSpec((9, tk), lambda b, k: (0, nk + k)),
        pl.BlockSpec((1, tk), lambda b, k: (0, k)),
        pl.BlockSpec((1, tk), lambda b, k: (0, nk + k)),
        pl.BlockSpec((None, tk, D_MODEL), lambda b, k: (layer, k, 0)),
        pl.BlockSpec((1, D_MODEL), lambda b, k: (0, 0)),
    ]
    args = [x1, u, u, mod3, conv_w, conv_w, conv_b, conv_b, w_down, nw]
    out_specs = [row_spec]
    out_shape = [jax.ShapeDtypeStruct((m, D_MODEL), F32)]
    if not final:
        in_specs += [modspec(0), modspec(1)]
        args += [mod3_next, mod3_next]
        out_specs.append(row_spec)
        out_shape.append(jax.ShapeDtypeStruct((m, D_MODEL), BF16))
    res = pl.pallas_call(
        functools.partial(_ffndown_kernel, width=width, rows=rows, final=final),
        grid=(m // tokens, nk),
        in_specs=in_specs, out_specs=out_specs, out_shape=out_shape,
        scratch_shapes=[pltpu.VMEM((tokens, D_MODEL), F32), pltpu.VMEM((tokens, tk), BF16)],
        compiler_params=_cparams(("arbitrary", "arbitrary")),
        name="ffndown",
    )(*args)
    return (res[0], None) if final else (res[0], res[1])


def _ssd_param_rows(a_log, dt_bias, d_skip):
    def per_group(v):
        return v.reshape(2, SSD_GROUPS, HEADS_PER_GROUP).transpose(1, 0, 2).reshape(SSD_GROUPS, 2 * HEADS_PER_GROUP)

    def rows(v):
        return jnp.pad(per_group(v), ((0, 0), (0, LANES - 2 * HEADS_PER_GROUP))).reshape(SSD_GROUPS, 1, LANES)
    a_cols = jnp.broadcast_to(per_group(a_log)[:, :, None], (SSD_GROUPS, 2 * HEADS_PER_GROUP, LANES))
    return rows(a_log), a_cols, rows(dt_bias), jnp.repeat(d_skip, SSD_HEADDIM).reshape(1, SSD_W)


def _run_pass(x, mod_all_rows, s_h0, s_s0, layers, hgrn_lb, final_norm_w, consts, width, rows, want_state):
    batch, seq_len, _ = x.shape
    m = batch * seq_len
    x2 = x.reshape(m, D_MODEL)
    tl, tu = consts[0], consts[1]
    prev_h, prev_s = [], []
    new_h, new_s = None, None
    hn = _norm_call(x2, mod_all_rows[0], layers[0]['norm_w1'], seq_len)
    for l, p in enumerate(layers):
        mod3 = mod_all_rows[l]
        last = l == DEPTH - 1
        o_h, new_h = _hgrn_call(hn, p['w_in'], hgrn_lb, p['hgrn_norm_w'], consts,
                                s_h0, prev_h, l, batch, seq_len, want_state)
        o_s, new_s = _ssd_call(hn, p['w_in'], p['ssd'], (tl, tu),
                               s_s0, prev_s, l, batch, seq_len, width, rows, want_state)
        prev_h.append(new_h)
        prev_s.append(new_s)
        x1, u = _outup_call(x2, o_h, o_s, mod3, p['norm_w2'], p['w_out'], p['ffn_up'], l, seq_len)
        x2, hn = _ffndown_call(x1, u, mod3, p['ffn_conv_w'], p['ffn_conv_b'], p['ffn_down'], l,
                               final_norm_w if last else layers[l + 1]['norm_w1'],
                               None if last else mod_all_rows[l + 1], seq_len, width, rows)
    if want_state and DEPTH == 1:
        new_h, new_s = new_h[:, None], new_s[:, None]
    return x2.reshape(batch, seq_len, D_MODEL), new_h, new_s


def kernel(x_prompt, x_sample, c, state_hgrn, state_ssd, c_ctx, norm_w, final_norm_w, w_ada, b_ada,
           w_in, w_out, hgrn_lb, hgrn_norm_w, ssd_conv_w, ssd_conv_b, ssd_a_log, ssd_dt_bias, ssd_d,
           ssd_norm_w, ffn_up, ffn_conv_w, ffn_conv_b, ffn_down):
    dec_batch = c.shape[0]
    consts = _scan_constants() + (_level_signs(),)

    cond8 = jnp.concatenate([c_ctx[None], c, jnp.zeros((8 - 1 - dec_batch, D_MODEL), F32)], axis=0)
    mod_all = _mod_call(cond8, w_ada, b_ada)
    mod_ctx = [mod_all[l, 0:1].reshape(1, 1, 6 * D_MODEL) for l in range(DEPTH)]
    mod_lat = [mod_all[l, 1:1 + dec_batch].reshape(dec_batch, 1, 6 * D_MODEL) for l in range(DEPTH)]

    w_in_t = jnp.swapaxes(w_in, 1, 2)
    w_out_b, ffn_up_b, ffn_down_b = w_out.astype(BF16), ffn_up.astype(BF16), ffn_down.astype(BF16)
    layers = []
    for l in range(DEPTH):
        a_rows, a_cols, b_rows, d_rows = _ssd_param_rows(ssd_a_log[l], ssd_dt_bias[l], ssd_d[l])
        layers.append(dict(
            norm_w1=norm_w[l, 0].reshape(1, D_MODEL), norm_w2=norm_w[l, 1].reshape(1, D_MODEL),
            w_in=w_in_t,
            w_out=w_out_b,
            hgrn_norm_w=hgrn_norm_w[l].reshape(1, HGRN_DV),
            ssd=dict(conv_w=ssd_conv_w[l].reshape(9, CONV_CH), conv_b=ssd_conv_b[l].reshape(1, CONV_CH),
                     a_log_rows=a_rows, a_log_cols=a_cols, dt_bias_rows=b_rows, d_rows=d_rows,
                     norm_w=ssd_norm_w[l].reshape(1, SSD_W)),
            ffn_up=ffn_up_b,
            ffn_conv_w=ffn_conv_w[l].reshape(9, 2 * D_FF), ffn_conv_b=ffn_conv_b[l].reshape(1, 2 * D_FF),
            ffn_down=ffn_down_b,
        ))
    fnw = final_norm_w.reshape(1, D_MODEL)

    y_prompt, new_h, new_s = _run_pass(x_prompt, mod_ctx, None, None, layers, hgrn_lb, fnw, consts,
                                       width=x_prompt.shape[1], rows=1, want_state=True)
    y_sample, _, _ = _run_pass(x_sample, mod_lat, state_hgrn, state_ssd, layers, hgrn_lb, fnw, consts,
                               width=GRID_W, rows=x_sample.shape[1] // GRID_W, want_state=False)
    return (y_prompt, y_sample, new_h, new_s)
```
